```python
import jax, jax.numpy as jnp
from jax import lax
import numpy as np

D_MODEL = 4096
BATCH = 8
SEQ = 2048
DEPTH = 1
DEC_BATCH = 8
DEC_SEQ = 64
PAST_LEN = 1024

CHUNK = 64
NORM_EPS = 1e-6
D_A = D_MODEL // 2
HEAD_A = 64
H_A = D_A // HEAD_A
R_W = max(32, int(round(1.8 * D_A ** 0.5 / 32)) * 32)
R_A = max(32, int(round(1.8 * D_A ** 0.5 / 32)) * 32)
R_G = max(32, int(round(0.6 * D_A ** 0.8 / 32)) * 32)
GN_EPS = 64e-5
SHIFT_SIZES = (D_A, D_A, D_A, R_W, R_A, R_G)
D_SHIFT = 3 * D_A + R_W + R_A + R_G
H_Q = 16
DH = 128
N_KV = 4
REP = H_Q // N_KV
H_I = 16
D_IDX = 64
TOPK_MAX = 256
Q_BLOCK = 64
D_FF = ((8 * D_MODEL + 2) // 3 + 255) // 256 * 256

IN_SIZES = (D_SHIFT, H_Q * DH, N_KV * DH, N_KV * DH, H_I * D_IDX, D_IDX, H_I, D_MODEL, D_MODEL)
D_IN = sum(IN_SIZES)

kernel_name = "rwkv7_dsa_alibi_gated_parallel_streaming_step"


def _rmsnorm(x, g):
    xf = x.astype(jnp.float32)
    y = xf * lax.rsqrt(jnp.mean(xf * xf, axis=-1, keepdims=True) + NORM_EPS)
    return (y * g.astype(jnp.float32)).astype(x.dtype)


def _wkv_step(S, inp):
    r, w, k, v, kk, a = inp
    sa = jnp.einsum('bhvk,bhk->bhv', S, -kk)
    S = S * w[:, :, None, :] + sa[..., None] * (kk * a)[:, :, None, :] + v[..., None] * k[:, :, None, :]
    y = jnp.einsum('bhvk,bhk->bhv', S, r)
    return S, y


def _rwkv_mix(zs, S0, w0, w2, a0, a2, g2, k_k, k_a, r_k, lnx_g, lnx_b):
    f32 = jnp.float32
    B, T, _ = zs.shape
    r, k, v, xw, xa, xg = jnp.split(zs, np.cumsum(SHIFT_SIZES)[:-1], axis=-1)
    wl = -jax.nn.softplus(-(w0 + jnp.tanh(xw) @ w2).astype(f32)) - 0.5
    decay = jnp.exp(-jnp.exp(wl))
    a = jax.nn.sigmoid((a0 + xa @ a2).astype(f32))
    g = jax.nn.sigmoid(xg) @ g2
    kf = k.astype(f32)
    hd = lambda t: t.reshape(B, T, H_A, HEAD_A)
    kk = hd(kf * k_k.astype(f32))
    kk = kk / jnp.maximum(jnp.sqrt(jnp.sum(kk * kk, axis=-1, keepdims=True)), 1e-12)
    k_h = hd(kf * (1.0 + (a - 1.0) * k_a.astype(f32)))
    r_h, v_h = hd(r.astype(f32)), hd(v.astype(f32))
    seq = (r_h, hd(decay), k_h, v_h, kk, hd(a))
    seq = tuple(jnp.moveaxis(t, 1, 0) for t in seq)
    S_T, y = lax.scan(_wkv_step, S0.astype(f32), seq)
    y = jnp.moveaxis(y, 0, 1)
    mu = jnp.mean(y, axis=-1, keepdims=True)
    var = jnp.mean(jnp.square(y - mu), axis=-1, keepdims=True)
    yn = ((y - mu) * lax.rsqrt(var + GN_EPS)).reshape(B, T, D_A) * lnx_g.astype(f32) + lnx_b.astype(f32)
    bonus = jnp.sum(r_h * k_h * r_k.astype(f32), axis=-1, keepdims=True) * v_h
    out = (yn + bonus.reshape(B, T, D_A)) * g.astype(f32)
    return out.astype(zs.dtype), S_T


def _alibi_slopes():
    return jnp.exp2(-8.0 * jnp.arange(1, H_Q + 1, dtype=jnp.float32) / H_Q)


def _dsa_block(q, qi, wi, q_pos, k_all, v_all, ki_all, k_pos, topk):
    f32 = jnp.float32
    B, Tq = q.shape[:2]
    allowed = (k_pos[None, :] // CHUNK) <= (q_pos[:, None] // CHUNK)
    sc = jnp.einsum('bqhd,bld->bqhl', qi.astype(f32), ki_all.astype(f32))
    isc = jnp.einsum('bqh,bqhl->bql', wi.astype(f32), jax.nn.relu(sc)) * (H_I ** -0.5 * D_IDX ** -0.5)
    isc = jnp.where(allowed[None], isc, -jnp.inf)
    top_val, idx = lax.top_k(isc, topk)
    valid = jnp.isfinite(top_val)
    kg = jax.vmap(lambda kb, ib: kb[ib])(k_all, idx).astype(f32)
    vg = jax.vmap(lambda vb, ib: vb[ib])(v_all, idx).astype(f32)
    dist = jnp.abs(q_pos[None, :, None] - k_pos[idx]).astype(f32)
    qg = q.astype(f32).reshape(B, Tq, N_KV, REP, DH)
    s = jnp.einsum('bqgrd,bqkgd->bqgrk', qg, kg) * (DH ** -0.5)
    s = s - _alibi_slopes().reshape(N_KV, REP)[None, None, :, :, None] * dist[:, :, None, None, :]
    s = jnp.where(valid[:, :, None, None, :], s, -jnp.inf)
    p = jax.nn.softmax(s, axis=-1)
    o = jnp.einsum('bqgrk,bqkgd->bqgrd', p, vg)
    return o.reshape(B, Tq, H_Q * DH).astype(q.dtype)


def _dsa_attend(q, qi, wi, k_all, v_all, ki_all, P):
    B, T = q.shape[:2]
    L = P + T
    topk = min(TOPK_MAX, L // 4)
    k_pos = jnp.arange(L, dtype=jnp.int32)
    if T > Q_BLOCK and T % Q_BLOCK == 0:
        def blk(i):
            s0 = i * Q_BLOCK
            sl = lambda t: lax.dynamic_slice_in_dim(t, s0, Q_BLOCK, axis=1)
            q_pos = P + s0 + jnp.arange(Q_BLOCK, dtype=jnp.int32)
            return _dsa_block(sl(q), sl(qi), sl(wi), q_pos, k_all, v_all, ki_all, k_pos, topk)
        out = lax.map(blk, jnp.arange(T // Q_BLOCK, dtype=jnp.int32))
        return jnp.moveaxis(out, 0, 1).reshape(B, T, H_Q * DH)
    q_pos = P + jnp.arange(T, dtype=jnp.int32)
    return _dsa_block(q, qi, wi, q_pos, k_all, v_all, ki_all, k_pos, topk)


def _layer(x, shift_prev, wkv0, past_k, past_v, past_ki, lp):
    B, T, _ = x.shape
    h = _rmsnorm(x, lp['norm1_g'])
    z = h @ lp['w_in']
    zsh, q, kx, vx, qi, ki, wi, ga, gb = jnp.split(z, np.cumsum(IN_SIZES)[:-1], axis=-1)
    zprev = jnp.concatenate([shift_prev.astype(z.dtype), zsh[:, :-1]], axis=1)
    zs = zsh + (zprev - zsh) * lp['mu_shift']
    y_a, wkv_new = _rwkv_mix(zs, wkv0, lp['w0'], lp['w2'], lp['a0'], lp['a2'], lp['g2'],
                             lp['k_k'], lp['k_a'], lp['r_k'], lp['lnx_g'], lp['lnx_b'])
    new_shift = zsh[:, -1:]
    q = q.reshape(B, T, H_Q, DH)
    kx = kx.reshape(B, T, N_KV, DH)
    vx = vx.reshape(B, T, N_KV, DH)
    qi = qi.reshape(B, T, H_I, D_IDX)
    P = past_k.shape[1]
    k_all = jnp.concatenate([past_k.astype(kx.dtype), kx], axis=1)
    v_all = jnp.concatenate([past_v.astype(vx.dtype), vx], axis=1)
    ki_all = jnp.concatenate([past_ki.astype(ki.dtype), ki], axis=1)
    y_b = _dsa_attend(q, qi, wi, k_all, v_all, ki_all, P)
    merged = jax.nn.sigmoid(ga) * (y_a @ lp['w_br_a']) + jax.nn.sigmoid(gb) * (y_b @ lp['w_br_b'])
    x = x + merged @ lp['w_out']
    h2 = _rmsnorm(x, lp['norm2_g'])
    x = x + (jax.nn.silu(h2 @ lp['w_ffn_gate']) * (h2 @ lp['w_ffn_up'])) @ lp['w_ffn_down']
    return x, kx, vx, ki, new_shift, wkv_new.astype(x.dtype)


def setup_inputs(seed: int = 0) -> dict:
    key = jax.random.key(seed)
    ks = jax.random.split(key, 32)
    f32 = jnp.float32
    n = lambda k, shape, scale: jax.random.normal(k, shape, f32) * scale
    L = DEPTH
    return {
        'x_prompt': n(ks[0], (BATCH, SEQ, D_MODEL), 1.0),
        'x_sample': n(ks[1], (DEC_BATCH, DEC_SEQ, D_MODEL), 1.0),
        'cache_k': n(ks[2], (L, DEC_BATCH, PAST_LEN, N_KV, DH), 1.0),
        'cache_v': n(ks[3], (L, DEC_BATCH, PAST_LEN, N_KV, DH), 1.0),
        'cache_kidx': n(ks[4], (L, DEC_BATCH, PAST_LEN, D_IDX), 1.0),
        'state_shift': n(ks[5], (L, DEC_BATCH, 1, D_SHIFT), 1.0),
        'state_wkv': n(ks[6], (L, DEC_BATCH, H_A, HEAD_A, HEAD_A), 0.5),
        'norm1_g': 1.0 + n(ks[7], (L, D_MODEL), 0.02),
        'w_in': n(ks[8], (L, D_MODEL, D_IN), D_MODEL ** -0.5),
        'mu_shift': jax.random.uniform(ks[9], (L, D_SHIFT), f32),
        'w0': n(ks[10], (L, D_A), 0.5),
        'w2': n(ks[11], (L, R_W, D_A), 0.1 * R_W ** -0.5),
        'a0': n(ks[12], (L, D_A), 0.1),
        'a2': n(ks[13], (L, R_A, D_A), R_A ** -0.5),
        'g2': n(ks[14], (L, R_G, D_A), R_G ** -0.5),
        'k_k': 0.85 + n(ks[15], (L, D_A), 0.05),
        'k_a': 1.0 + n(ks[16], (L, D_A), 0.05),
        'r_k': n(ks[17], (L, H_A, HEAD_A), 0.1),
        'lnx_g': 1.0 + n(ks[18], (L, D_A), 0.02),
        'lnx_b': n(ks[19], (L, D_A), 0.02),
        'w_br_a': n(ks[20], (L, D_A, D_MODEL), D_A ** -0.5),
        'w_br_b': n(ks[21], (L, H_Q * DH, D_MODEL), (H_Q * DH) ** -0.5),
        'w_out': n(ks[22], (L, D_MODEL, D_MODEL), D_MODEL ** -0.5),
        'norm2_g': 1.0 + n(ks[23], (L, D_MODEL), 0.02),
        'w_ffn_gate': n(ks[24], (L, D_MODEL, D_FF), D_MODEL ** -0.5),
        'w_ffn_up': n(ks[25], (L, D_MODEL, D_FF), D_MODEL ** -0.5),
        'w_ffn_down': n(ks[26], (L, D_FF, D_MODEL), D_FF ** -0.5),
        'norm_f_g': 1.0 + n(ks[27], (D_MODEL,), 0.02),
    }


def reference(x_prompt, x_sample, cache_k, cache_v, cache_kidx, state_shift, state_wkv,
              norm1_g, w_in, mu_shift, w0, w2, a0, a2, g2, k_k, k_a, r_k, lnx_g, lnx_b,
              w_br_a, w_br_b, w_out, norm2_g, w_ffn_gate, w_ffn_up, w_ffn_down, norm_f_g):
    B = x_prompt.shape[0]
    dt = x_prompt.dtype
    xp, xs = x_prompt, x_sample
    pk_l, pv_l, pki_l, psh_l, pwkv_l = [], [], [], [], []
    sk_l, sv_l, ski_l, ssh_l, swkv_l = [], [], [], [], []
    for l in range(DEPTH):
        lp = dict(norm1_g=norm1_g[l], w_in=w_in[l], mu_shift=mu_shift[l], w0=w0[l], w2=w2[l],
                  a0=a0[l], a2=a2[l], g2=g2[l], k_k=k_k[l], k_a=k_a[l], r_k=r_k[l],
                  lnx_g=lnx_g[l], lnx_b=lnx_b[l], w_br_a=w_br_a[l], w_br_b=w_br_b[l],
                  w_out=w_out[l], norm2_g=norm2_g[l], w_ffn_gate=w_ffn_gate[l],
                  w_ffn_up=w_ffn_up[l], w_ffn_down=w_ffn_down[l])
        xp, pk, pv, pki, psh, pwkv = _layer(
            xp, jnp.zeros((B, 1, D_SHIFT), dt), jnp.zeros((B, H_A, HEAD_A, HEAD_A), dt),
            jnp.zeros((B, 0, N_KV, DH), dt), jnp.zeros((B, 0, N_KV, DH), dt),
            jnp.zeros((B, 0, D_IDX), dt), lp)
        xs, sk, sv, ski, ssh, swkv = _layer(
            xs, state_shift[l], state_wkv[l], cache_k[l], cache_v[l], cache_kidx[l], lp)
        pk_l.append(pk); pv_l.append(pv); pki_l.append(pki); psh_l.append(psh); pwkv_l.append(pwkv)
        sk_l.append(sk); sv_l.append(sv); ski_l.append(ski); ssh_l.append(ssh); swkv_l.append(swkv)
    y_prompt = _rmsnorm(xp, norm_f_g)
    y_sample = _rmsnorm(xs, norm_f_g)
    return (y_prompt, y_sample,
            jnp.stack(pk_l), jnp.stack(pv_l), jnp.stack(pki_l), jnp.stack(psh_l), jnp.stack(pwkv_l),
            jnp.stack(sk_l), jnp.stack(sv_l), jnp.stack(ski_l), jnp.stack(ssh_l), jnp.stack(swkv_l))
```

```python
import functools

import jax
import jax.numpy as jnp
from jax import lax
from jax.experimental import pallas as pl
from jax.experimental.pallas import tpu as pltpu

CHUNK = 64
NORM_EPS = 1e-6
GN_EPS = 64e-5
TOPK_MAX = 256

LANES = 128
SUBLANES = 8
V7X_VMEM_LIMIT_BYTES = 56 * 1024 * 1024

WKV_CHUNK = 64
NEG_BIG = -1e30

_F32 = jnp.float32
_BF16 = jnp.bfloat16


def _round_up(n, m):
    return (n + m - 1) // m * m


def _pick_tile(n, pref, align):
    if n <= pref:
        return n
    t = pref // align * align
    while t >= align:
        if n % t == 0:
            return t
        t -= align
    return n


def _compiler_params(semantics):
    return pltpu.CompilerParams(dimension_semantics=semantics,
                                vmem_limit_bytes=V7X_VMEM_LIMIT_BYTES)


def _rmsnorm_kernel(x_ref, g_ref, o_ref):
    x = x_ref[...].astype(_F32)
    ms = jnp.mean(x * x, axis=-1, keepdims=True)
    o_ref[...] = (x * lax.rsqrt(ms + NORM_EPS) * g_ref[...]).astype(o_ref.dtype)


def _rmsnorm(x, g, out_dtype):
    m, d = x.shape
    tm = _pick_tile(m, 512, SUBLANES)
    return pl.pallas_call(
        _rmsnorm_kernel,
        grid=(m // tm,),
        in_specs=[pl.BlockSpec((tm, d), lambda i: (i, 0)),
                  pl.BlockSpec((1, d), lambda i: (0, 0))],
        out_specs=pl.BlockSpec((tm, d), lambda i: (i, 0)),
        out_shape=jax.ShapeDtypeStruct((m, d), out_dtype),
        compiler_params=_compiler_params(("parallel",)),
        name="rmsnorm",
    )(x, g.reshape(1, d).astype(_F32))


def _matmul_kernel(*refs, n_pairs, n_extras, n_outs, n_k, head_major, epilogue):
    x_refs = refs[:n_pairs]
    w_refs = refs[n_pairs:2 * n_pairs]
    e_refs = refs[2 * n_pairs:2 * n_pairs + n_extras]
    o_refs = refs[2 * n_pairs + n_extras:2 * n_pairs + n_extras + n_outs]
    acc_refs = refs[2 * n_pairs + n_extras + n_outs:]

    def load_x(i):
        if head_major[i]:
            xr = x_refs[i]
            return jnp.concatenate([xr[h] for h in range(xr.shape[0])], axis=1)
        return x_refs[i][...]

    def partial(i):
        return jnp.dot(load_x(i), w_refs[i][...], preferred_element_type=_F32)

    def finish(accs):
        outs = epilogue(accs, [e[...] for e in e_refs])
        for o_ref, val in zip(o_refs, outs):
            if len(o_ref.shape) == 3:
                for h in range(o_ref.shape[0]):
                    o_ref[h] = val[:, h * LANES:(h + 1) * LANES].astype(o_ref.dtype)
            else:
                o_ref[...] = val.astype(o_ref.dtype)

    if n_k == 1:
        finish([partial(i) for i in range(n_pairs)])
        return

    k = pl.program_id(2)

    @pl.when(k == 0)
    def _():
        for i in range(n_pairs):
            acc_refs[i][...] = partial(i)

    @pl.when(k > 0)
    def _():
        for i in range(n_pairs):
            acc_refs[i][...] += partial(i)

    @pl.when(k == n_k - 1)
    def _():
        finish([a[...] for a in acc_refs])


def _matmul(xs, ws, extras, epilogue, outs, *, tm_pref=1024, tn_pref=512, tk=None, name="matmul"):
    n_pairs = len(xs)
    head_major = tuple(x.ndim == 3 for x in xs)
    m = xs[0].shape[1] if head_major[0] else xs[0].shape[0]
    kdim, n = ws[0].shape
    tm = _pick_tile(m, tm_pref, SUBLANES * 2)
    tn = _pick_tile(n, tn_pref, LANES)
    if tk is None:
        tk = kdim
    n_k = kdim // tk
    assert kdim % tk == 0 and m % tm == 0 and n % tn == 0
    n_j = n // tn

    tks = [tk if n_k > 1 else w.shape[0] for w in ws]
    assert all(w.shape == (kdim, n) for w in ws) or n_k == 1
    in_specs = []
    for x, hm, tki in zip(xs, head_major, tks):
        if hm:
            assert n_k == 1
            in_specs.append(pl.BlockSpec((x.shape[0], tm, LANES), lambda i, j, k: (0, i, 0)))
        else:
            in_specs.append(pl.BlockSpec((tm, tki), lambda i, j, k: (i, k)))
    for w, tki in zip(ws, tks):
        in_specs.append(pl.BlockSpec((tki, tn), lambda i, j, k: (k, j)))
    for e in extras:
        te = e.shape[1] // n_j
        in_specs.append(pl.BlockSpec((tm, te), lambda i, j, k: (i, j)))

    out_specs, out_shapes = [], []
    for n_cols, dtype, hm in outs:
        to = n_cols // n_j
        if hm:
            assert to % LANES == 0
            out_specs.append(pl.BlockSpec((to // LANES, tm, LANES), lambda i, j, k: (j, i, 0)))
            out_shapes.append(jax.ShapeDtypeStruct((n_cols // LANES, m, LANES), dtype))
        else:
            out_specs.append(pl.BlockSpec((tm, to), lambda i, j, k: (i, j)))
            out_shapes.append(jax.ShapeDtypeStruct((m, n_cols), dtype))

    scratch = [pltpu.VMEM((tm, tn), _F32) for _ in range(n_pairs)] if n_k > 1 else []
    kern = functools.partial(
        _matmul_kernel, n_pairs=n_pairs, n_extras=len(extras), n_outs=len(outs),
        n_k=n_k, head_major=head_major, epilogue=epilogue)
    res = pl.pallas_call(
        kern,
        grid=(m // tm, n_j, n_k),
        in_specs=in_specs,
        out_specs=out_specs,
        out_shape=out_shapes,
        scratch_shapes=scratch,
        compiler_params=_compiler_params(("parallel", "parallel", "arbitrary")),
        name=name,
    )(*xs, *ws, *extras)
    return res


def _sigmoid(x):
    return 1.0 / (1.0 + jnp.exp(-x))


def _shift_rows(z, prev_row):
    rolled = pltpu.roll(z, 1, 0)
    row = lax.broadcasted_iota(jnp.int32, z.shape, 0)
    return jnp.where(row == 0, prev_row, rolled)


def _cumsum_rows(x):
    n = x.shape[0]
    row = lax.broadcasted_iota(jnp.int32, x.shape, 0)
    s = 1
    while s < n:
        x = x + jnp.where(row >= s, pltpu.roll(x, s, 0), 0.0)
        s *= 2
    return x


def _dot(a, b):
    return jnp.dot(a.astype(_BF16), b.astype(_BF16), preferred_element_type=_F32)


def _dot_nt(a, b):
    return lax.dot_general(a.astype(_BF16), b.astype(_BF16), (((1,), (1,)), ((), ())),
                           preferred_element_type=_F32)


def _wkv_kernel(zr_ref, zk_ref, zv_ref, zl_ref, sprev_ref, lprev_ref, s0_ref,
                mu_ref, mul_ref, w0_ref, a0_ref, kkp_ref, kap_ref, rk_ref, lg_ref, lb_ref,
                w2_ref, a2_ref, g2_ref,
                y_ref, sout_ref,
                st_ref, prev_ref, lprev_scr, *, n_chunks, head):
    c_idx = pl.program_id(2)
    C = WKV_CHUNK
    P = 2 * head

    @pl.when(c_idx == 0)
    def _():
        st_ref[...] = s0_ref[0, 0]
        prev_ref[...] = sprev_ref[0, 0]
        lprev_scr[...] = lprev_ref[0]

    lane = lax.broadcasted_iota(jnp.int32, (C, P), 1)
    m0 = (lane < head).astype(_F32)
    m1 = 1.0 - m0
    r2 = lax.broadcasted_iota(jnp.int32, (2 * C, 2 * C), 0)
    c2 = lax.broadcasted_iota(jnp.int32, (2 * C, 2 * C), 1)
    same = (r2 // C) == (c2 // C)
    tril_s = jnp.where(same & (c2 < r2), 1.0, 0.0)
    tril_i = jnp.where(same & (c2 <= r2), 1.0, 0.0)
    eye = jnp.where(r2 == c2, 1.0, 0.0)
    eye_p = jnp.where(lax.broadcasted_iota(jnp.int32, (P, P), 0)
                      == lax.broadcasted_iota(jnp.int32, (P, P), 1), 1.0, 0.0)

    def stack(x):
        return jnp.concatenate([x * m0, x * m1], axis=0)

    def head_sum(x):
        s0 = jnp.sum(x * m0, axis=-1, keepdims=True)
        s1 = jnp.sum(x * m1, axis=-1, keepdims=True)
        return s0 * m0 + s1 * m1

    n_sub = zr_ref.shape[0] // C
    for sc in range(n_sub):
        rows = pl.ds(sc * C, C)
        zr = zr_ref[rows, :]
        zk = zk_ref[rows, :]
        zv = zv_ref[rows, :]
        zl = zl_ref[rows, :]
        r = zr + (_shift_rows(zr, prev_ref[0:1, :]) - zr) * mu_ref[0:1, :]
        k = zk + (_shift_rows(zk, prev_ref[1:2, :]) - zk) * mu_ref[1:2, :]
        v = zv + (_shift_rows(zv, prev_ref[2:3, :]) - zv) * mu_ref[2:3, :]
        xl = zl + (_shift_rows(zl, lprev_scr[...]) - zl) * mul_ref[...]
        prev_ref[0:1, :] = zr[C - 1:C, :]
        prev_ref[1:2, :] = zk[C - 1:C, :]
        prev_ref[2:3, :] = zv[C - 1:C, :]
        lprev_scr[...] = zl[C - 1:C, :]
        nw = w2_ref.shape[0]
        na = a2_ref.shape[0]
        xw = xl[:, 0:nw]
        xa = xl[:, nw:nw + na]
        xg = xl[:, nw + na:]
        wpre = w0_ref[...] + _dot(jnp.tanh(xw), w2_ref[...])
        u = -wpre
        softplus = jnp.maximum(u, 0.0) + jnp.log(1.0 + jnp.exp(-jnp.abs(u)))
        logw = -jnp.exp(-softplus - 0.5)
        a = _sigmoid(a0_ref[...] + _dot(xa, a2_ref[...]))
        g = _dot(_sigmoid(xg), g2_ref[...])
        kk = k * kkp_ref[...]
        kk = kk / jnp.maximum(jnp.sqrt(head_sum(kk * kk)), 1e-12)
        kh = k * (1.0 + (a - 1.0) * kap_ref[...])
        alpha = -kk
        beta = kk * a

        cum = _cumsum_rows(logw)
        cum_c = cum[C - 1:C, :]
        e_neg = jnp.exp(-cum)
        e_end = jnp.exp(cum_c - cum)
        a_t = alpha * jnp.exp(cum - logw)
        r_t = r * jnp.exp(cum)
        b_t = beta * e_neg
        k_t = kh * e_neg
        b_h = beta * e_end
        k_h = kh * e_end

        a2s, r2s, b2s, k2s, v2s = stack(a_t), stack(r_t), stack(b_t), stack(k_t), stack(v)
        bh2, kh2 = stack(b_h), stack(k_h)
        st = st_ref[...]

        n_ba = _dot_nt(a2s, b2s) * tril_s
        n_ka = _dot_nt(a2s, k2s) * tril_s
        p_br = _dot_nt(r2s, b2s) * tril_i
        p_kr = _dot_nt(r2s, k2s) * tril_i
        t_inv = eye + n_ba
        pw = n_ba
        s = 2
        while s < C:
            pw = _dot(pw, pw)
            t_inv = t_inv + _dot(t_inv, pw)
            s *= 2
        x2 = _dot(a2s, st) + _dot(n_ka, v2s)
        e2 = _dot(t_inv, x2)
        y2 = _dot(r2s, st) + _dot(p_br, e2) + _dot(p_kr, v2s)
        dec_col = jnp.sum(eye_p * jnp.exp(cum_c), axis=-1, keepdims=True)
        st_new = dec_col * st + _dot(bh2.T, e2) + _dot(kh2.T, v2s)
        st_ref[...] = st_new

        y = y2[0:C, :] + y2[C:2 * C, :]
        mean = head_sum(y) * (1.0 / head)
        yc = y - mean
        var = head_sum(yc * yc) * (1.0 / head)
        yn = yc * lax.rsqrt(var + GN_EPS) * lg_ref[...] + lb_ref[...]
        bonus = head_sum(r * kh * rk_ref[...]) * v
        y_ref[rows, :] = ((yn + bonus) * g).astype(y_ref.dtype)

    @pl.when(c_idx == n_chunks - 1)
    def _():
        sout_ref[0, 0] = st_ref[...]


def _wkv(z_rkv, z_lora, shift_prev, s0_bd, params, *, batch, seq, row0, head, lora_sizes):
    d_a = z_rkv.shape[1] // 3
    pw = 2 * head
    n_pairs = d_a // pw
    lp = z_lora.shape[1]
    tb = _pick_tile(seq, 256, WKV_CHUNK)
    n_chunks = seq // tb
    assert row0 % tb == 0
    rb0 = row0 // tb
    rpb = seq // tb

    def rows(b, c):
        return rb0 + b * rpb + c

    vec = lambda name: params[name]
    in_specs = [
        pl.BlockSpec((tb, pw), lambda b, p, c: (rows(b, c), p)),
        pl.BlockSpec((tb, pw), lambda b, p, c: (rows(b, c), n_pairs + p)),
        pl.BlockSpec((tb, pw), lambda b, p, c: (rows(b, c), 2 * n_pairs + p)),
        pl.BlockSpec((tb, lp), lambda b, p, c: (rows(b, c), 0)),
        pl.BlockSpec((1, 1, SUBLANES, pw), lambda b, p, c: (b, p, 0, 0)),
        pl.BlockSpec((1, 1, lp), lambda b, p, c: (b, 0, 0)),
        pl.BlockSpec((1, 1, pw, pw), lambda b, p, c: (b, p, 0, 0)),
        pl.BlockSpec((SUBLANES, pw), lambda b, p, c: (p, 0)),
        pl.BlockSpec((1, lp), lambda b, p, c: (0, 0)),
    ]
    for _ in range(7):
        in_specs.append(pl.BlockSpec((1, pw), lambda b, p, c: (0, p)))
    nw, na, ng = lora_sizes
    in_specs += [pl.BlockSpec((nw, pw), lambda b, p, c: (0, p)),
                 pl.BlockSpec((na, pw), lambda b, p, c: (0, p)),
                 pl.BlockSpec((ng, pw), lambda b, p, c: (0, p))]
    out_specs = [pl.BlockSpec((tb, pw), lambda b, p, c: (b * rpb + c, p)),
                 pl.BlockSpec((1, 1, pw, pw), lambda b, p, c: (b, p, 0, 0))]
    out_shape = [jax.ShapeDtypeStruct((batch * seq, d_a), _BF16),
                 jax.ShapeDtypeStruct((batch, n_pairs, pw, pw), _F32)]
    kern = functools.partial(_wkv_kernel, n_chunks=n_chunks, head=head)
    return pl.pallas_call(
        kern,
        grid=(batch, n_pairs, n_chunks),
        in_specs=in_specs,
        out_specs=out_specs,
        out_shape=out_shape,
        scratch_shapes=[pltpu.VMEM((pw, pw), _F32),
                        pltpu.VMEM((SUBLANES, pw), _F32),
                        pltpu.VMEM((1, lp), _F32)],
        compiler_params=_compiler_params(("parallel", "parallel", "arbitrary")),
        name="wkv7_chunked",
    )(z_rkv, z_rkv, z_rkv, z_lora, shift_prev["rkv"], shift_prev["lora"], s0_bd,
      vec("mu_rkv"), vec("mu_lora"), vec("w0"), vec("a0"), vec("k_k"), vec("k_a"),
      vec("r_k"), vec("lnx_g"), vec("lnx_b"), vec("w2"), vec("a2"), vec("g2"))


def _dsa_kernel(q_ref, k_ref, v_ref, qi_ref, wi_ref, ki_ref, o_ref, bias_ref, dist_ref,
                *, tq, l_true, q_pos0, topk, n_idx_heads, idx_scale, n_q_heads):
    qb = pl.program_id(1)
    h = pl.program_id(2)
    lp = k_ref.shape[1]

    @pl.when(h == 0)
    def _():
        q_pos = q_pos0 + qb * tq + lax.broadcasted_iota(jnp.int32, (tq, lp), 0)
        k_pos = lax.broadcasted_iota(jnp.int32, (tq, lp), 1)
        allowed = ((k_pos // CHUNK) <= (q_pos // CHUNK)) & (k_pos < l_true)
        dist_ref[...] = jnp.abs(q_pos - k_pos).astype(_F32)
        ki = ki_ref[0]
        wi = wi_ref[...]
        isc = jnp.zeros((tq, lp), _F32)
        for hi in range(n_idx_heads):
            sc = lax.dot_general(qi_ref[hi], ki, (((1,), (1,)), ((), ())),
                                 preferred_element_type=_F32)
            isc = isc + jnp.maximum(sc, 0.0) * wi[:, hi:hi + 1]
        isc = isc * idx_scale
        bits = pltpu.bitcast(isc, jnp.int32)
        key = jnp.where(bits < 0, bits ^ jnp.int32(0x7FFFFFFF), bits)
        int_min = jnp.int32(-2 ** 31)
        key = jnp.where(allowed, key, int_min)

        def count_ge(cand):
            return jnp.sum(jnp.where(key >= cand, 1.0, 0.0), axis=-1, keepdims=True)

        kf = jnp.float32(topk)
        thr = jnp.where(count_ge(jnp.zeros((tq, 1), jnp.int32)) >= kf,
                        jnp.zeros((tq, 1), jnp.int32), jnp.full((tq, 1), int_min, jnp.int32))

        def body(i, thr):
            cand = thr + (jnp.int32(1) << (jnp.int32(30) - i))
            return jnp.where(count_ge(cand) >= kf, cand, thr)

        thr = lax.fori_loop(0, 31, body, thr)
        n_gt = jnp.sum(jnp.where(key > thr, 1.0, 0.0), axis=-1, keepdims=True)
        need = kf - n_gt
        tie = key == thr
        n_bits = max(1, (lp - 1).bit_length())

        def count_tie_below(bound):
            return jnp.sum(jnp.where(tie & (k_pos < bound), 1.0, 0.0), axis=-1, keepdims=True)

        def tbody(i, bound):
            cand = bound + (jnp.int32(1) << (jnp.int32(n_bits - 1) - i))
            return jnp.where(count_tie_below(cand) <= need, cand, bound)

        bound = jnp.where(count_tie_below(jnp.full((tq, 1), 1 << n_bits, jnp.int32)) <= need,
                          jnp.full((tq, 1), 1 << n_bits, jnp.int32), jnp.zeros((tq, 1), jnp.int32))
        bound = lax.fori_loop(0, n_bits, tbody, bound)
        sel = allowed & ((key > thr) | (tie & (k_pos < bound)))
        bias_ref[...] = jnp.where(sel, 0.0, NEG_BIG)

    hv = jnp.full((1, 1), h + 1, jnp.int32).astype(_F32)
    slope = jnp.exp(hv * (-8.0 / n_q_heads * 0.6931471805599453))
    s = lax.dot_general(q_ref[0], k_ref[0], (((1,), (1,)), ((), ())),
                        preferred_element_type=_F32)
    s = s - slope * dist_ref[...] + bias_ref[...]
    m = jnp.max(s, axis=-1, keepdims=True)
    p = jnp.exp(s - m)
    l = jnp.sum(p, axis=-1, keepdims=True)
    o = jnp.dot(p.astype(_BF16), v_ref[0], preferred_element_type=_F32)
    o_ref[0] = (o / l).astype(o_ref.dtype)


def _dsa(q_hm, k_hm, v_hm, qi_hm, wi, ki, *, batch, seq, row0, l_true, q_pos0, n_kv):
    n_q, _, dh = q_hm.shape
    n_i, _, d_idx = qi_hm.shape
    lp = k_hm.shape[1]
    rep = n_q // n_kv
    tq = _pick_tile(seq, 128, CHUNK)
    nqb = seq // tq
    rb0 = row0 // tq
    assert row0 % tq == 0
    topk = min(TOPK_MAX, l_true // 4)
    kern = functools.partial(
        _dsa_kernel, tq=tq, l_true=l_true, q_pos0=q_pos0, topk=topk, n_idx_heads=n_i,
        idx_scale=float(n_i) ** -0.5 * float(d_idx) ** -0.5, n_q_heads=n_q)
    return pl.pallas_call(
        kern,
        grid=(batch, nqb, n_q),
        in_specs=[
            pl.BlockSpec((1, tq, dh), lambda b, i, h: (h, rb0 + b * nqb + i, 0)),
            pl.BlockSpec((1, lp, dh), lambda b, i, h: (b * n_kv + h // rep, 0, 0)),
            pl.BlockSpec((1, lp, dh), lambda b, i, h: (b * n_kv + h // rep, 0, 0)),
            pl.BlockSpec((n_i, tq, d_idx), lambda b, i, h: (0, rb0 + b * nqb + i, 0)),
            pl.BlockSpec((tq, n_i), lambda b, i, h: (rb0 + b * nqb + i, 0)),
            pl.BlockSpec((1, lp, d_idx), lambda b, i, h: (b, 0, 0)),
        ],
        out_specs=pl.BlockSpec((1, tq, dh), lambda b, i, h: (h, b * nqb + i, 0)),
        out_shape=jax.ShapeDtypeStruct((n_q, batch * seq, dh), _BF16),
        scratch_shapes=[pltpu.VMEM((tq, lp), _F32), pltpu.VMEM((tq, lp), _F32)],
        compiler_params=_compiler_params(("parallel", "parallel", "arbitrary")),
        name="dsa_attention",
    )(q_hm, k_hm, v_hm, qi_hm, wi, ki)


def _pad_rows(w, rows):
    return jnp.pad(w, ((0, rows - w.shape[0]), (0, 0)))


def _pad_cols(w, cols):
    return jnp.pad(w, ((0, 0), (0, cols - w.shape[1])))


def _to_block_diag_t(s, head):
    b, h = s.shape[:2]
    st = jnp.swapaxes(s, -1, -2).reshape(b, h // 2, 2, head, head).astype(_F32)
    z = jnp.zeros_like(st[:, :, 0])
    top = jnp.concatenate([st[:, :, 0], z], axis=-1)
    bot = jnp.concatenate([z, st[:, :, 1]], axis=-1)
    return jnp.concatenate([top, bot], axis=-2)


def _from_block_diag_t(sbd, head):
    b, hp = sbd.shape[:2]
    s0 = sbd[:, :, :head, :head]
    s1 = sbd[:, :, head:, head:]
    st = jnp.stack([s0, s1], axis=2).reshape(b, hp * 2, head, head)
    return jnp.swapaxes(st, -1, -2)


def kernel(x_prompt, x_sample, cache_k, cache_v, cache_kidx, state_shift, state_wkv, norm1_g, w_in, mu_shift, w0, w2, a0, a2, g2, k_k, k_a, r_k, lnx_g, lnx_b, w_br_a, w_br_b, w_out, norm2_g, w_ffn_gate, w_ffn_up, w_ffn_down, norm_f_g):
    depth = w_in.shape[0]
    assert depth == 1
    bp, tp, d_model = x_prompt.shape
    bs, ts, _ = x_sample.shape
    _, _, past, n_kv, dh = cache_k.shape
    d_idx = cache_kidx.shape[-1]
    d_shift = state_shift.shape[-1]
    _, _, h_a, head, _ = state_wkv.shape
    d_a = h_a * head
    r_w, r_a, r_g = w2.shape[1], a2.shape[1], g2.shape[1]
    n_q = w_br_b.shape[1] // dh
    d_in = w_in.shape[-1]
    h_i = (d_in - d_shift - n_q * dh - 2 * n_kv * dh - d_idx - 2 * d_model) // (d_idx + 1)
    d_ff = w_ffn_gate.shape[-1]
    l = 0
    mp, ms = bp * tp, bs * ts
    m_tot = mp + ms

    win = w_in[l]
    o = 0
    def take(n):
        nonlocal o
        blk = win[:, o:o + n]
        o += n
        return blk
    w_rkv = take(3 * d_a).astype(_BF16)
    nwp, nap, ngp = _round_up(r_w, LANES), _round_up(r_a, LANES), _round_up(r_g, LANES)
    w_lw, w_la, w_lg = take(r_w), take(r_a), take(r_g)
    w_lora = jnp.concatenate([_pad_cols(w_lw, nwp), _pad_cols(w_la, nap), _pad_cols(w_lg, ngp)],
                             axis=1).astype(_BF16)
    w_q = take(n_q * dh).astype(_BF16)
    w_k = take(n_kv * dh).astype(_BF16)
    w_v = take(n_kv * dh).astype(_BF16)
    w_qi = take(h_i * d_idx).astype(_BF16)
    w_ki = take(d_idx)
    w_wi = take(h_i)
    w_kiwi = jnp.concatenate([_pad_cols(w_ki, LANES), _pad_cols(w_wi, LANES)], axis=1).astype(_BF16)
    w_ga = take(d_model).astype(_BF16)
    w_gb = take(d_model).astype(_BF16)

    h1 = jnp.concatenate([
        _rmsnorm(x_prompt.reshape(mp, d_model), norm1_g[l], _BF16),
        _rmsnorm(x_sample.reshape(ms, d_model), norm1_g[l], _BF16)], axis=0)

    ident = lambda accs, ex: [accs[0]]
    (z_rkv,) = _matmul([h1], [w_rkv], [], ident, [(3 * d_a, _F32, False)], name="proj_rkv")
    (z_lora,) = _matmul([h1], [w_lora], [], ident, [(w_lora.shape[1], _F32, False)], name="proj_lora")
    q_scale = float(dh) ** -0.5
    (q_hm,) = _matmul([h1], [w_q], [], lambda accs, ex: [accs[0] * q_scale],
                      [(n_q * dh, _BF16, True)], name="proj_q")
    (kx,) = _matmul([h1], [w_k], [], ident, [(n_kv * dh, _F32, False)], name="proj_k")
    (vx,) = _matmul([h1], [w_v], [], ident, [(n_kv * dh, _F32, False)], name="proj_v")
    (qi,) = _matmul([h1], [w_qi], [], ident, [(h_i * d_idx, _BF16, False)], name="proj_qi")
    (kiwi,) = _matmul([h1], [w_kiwi], [], ident, [(2 * LANES, _F32, False)], name="proj_kiwi")
    (ga,) = _matmul([h1], [w_ga], [], ident, [(d_model, _BF16, False)], name="proj_ga")
    (gb,) = _matmul([h1], [w_gb], [], ident, [(d_model, _BF16, False)], name="proj_gb")
    ki_new = kiwi[:, :d_idx]
    wi = kiwi[:, LANES:LANES + h_i]

    mu = mu_shift[l]
    lp = nwp + nap + ngp
    def lora_row(vec):
        return jnp.concatenate([
            jnp.pad(vec[3 * d_a:3 * d_a + r_w], (0, nwp - r_w)),
            jnp.pad(vec[3 * d_a + r_w:3 * d_a + r_w + r_a], (0, nap - r_a)),
            jnp.pad(vec[3 * d_a + r_w + r_a:], (0, ngp - r_g))])
    n_pairs = d_a // (2 * head)
    def rkv_rows(vec):
        lead = vec.shape[:-1]
        x = vec[..., :3 * d_a].reshape(lead + (3, n_pairs, 2 * head))
        x = jnp.moveaxis(x, -3, -2)
        return jnp.pad(x, [(0, 0)] * (len(lead) + 1) + [(0, SUBLANES - 3), (0, 0)])
    row = lambda v: v[l].reshape(1, d_a).astype(_F32)
    params = dict(
        mu_rkv=rkv_rows(mu).reshape(n_pairs * SUBLANES, 2 * head),
        mu_lora=lora_row(mu).reshape(1, lp),
        w0=row(w0), a0=row(a0), k_k=row(k_k), k_a=row(k_a),
        r_k=r_k[l].reshape(1, d_a).astype(_F32), lnx_g=row(lnx_g), lnx_b=row(lnx_b),
        w2=_pad_rows(w2[l], nwp).astype(_BF16), a2=_pad_rows(a2[l], nap).astype(_BF16),
        g2=_pad_rows(g2[l], ngp).astype(_BF16))

    def shift_state(s):
        s = s[:, 0]
        return dict(rkv=rkv_rows(s), lora=jax.vmap(lora_row)(s).reshape(-1, 1, lp))

    zeros_shift = jnp.zeros((bp, 1, d_shift), _F32)
    ya_p, sp_bd = _wkv(z_rkv, z_lora, shift_state(zeros_shift),
                       jnp.zeros((bp, n_pairs, 2 * head, 2 * head), _F32), params,
                       batch=bp, seq=tp, row0=0, head=head, lora_sizes=(nwp, nap, ngp))
    ya_s, ss_bd = _wkv(z_rkv[mp:], z_lora[mp:], shift_state(state_shift[l]),
                       _to_block_diag_t(state_wkv[l], head), params,
                       batch=bs, seq=ts, row0=0, head=head, lora_sizes=(nwp, nap, ngp))
    y_a = jnp.concatenate([ya_p, ya_s], axis=0)
    wkv_p = _from_block_diag_t(sp_bd, head)
    wkv_s = _from_block_diag_t(ss_bd, head)

    qi_hm = jnp.moveaxis(qi.reshape(m_tot, h_i, d_idx), 1, 0)
    def kv_heads(x, b, t):
        return jnp.moveaxis(x.reshape(b, t, n_kv, dh), 2, 1)
    def pad_len(x, lpad):
        cfg = [(0, 0)] * x.ndim
        cfg[-2] = (0, lpad - x.shape[-2])
        return jnp.pad(x, cfg)
    lpp = _round_up(tp, LANES)
    kp = pad_len(kv_heads(kx[:mp], bp, tp), lpp).astype(_BF16).reshape(bp * n_kv, lpp, dh)
    vp = pad_len(kv_heads(vx[:mp], bp, tp), lpp).astype(_BF16).reshape(bp * n_kv, lpp, dh)
    kip = pad_len(ki_new[:mp].reshape(bp, tp, d_idx), lpp).astype(_BF16)
    ob_p = _dsa(q_hm, kp, vp, qi_hm, wi, kip, batch=bp, seq=tp, row0=0, l_true=tp, q_pos0=0, n_kv=n_kv)
    ls = past + ts
    lps = _round_up(ls, LANES)
    k_all = jnp.concatenate([jnp.moveaxis(cache_k[l], 2, 1), kv_heads(kx[mp:], bs, ts)], axis=2)
    v_all = jnp.concatenate([jnp.moveaxis(cache_v[l], 2, 1), kv_heads(vx[mp:], bs, ts)], axis=2)
    ki_all = jnp.concatenate([cache_kidx[l], ki_new[mp:].reshape(bs, ts, d_idx)], axis=1)
    ks = pad_len(k_all, lps).astype(_BF16).reshape(bs * n_kv, lps, dh)
    vs = pad_len(v_all, lps).astype(_BF16).reshape(bs * n_kv, lps, dh)
    kis = pad_len(ki_all, lps).astype(_BF16)
    ob_s = _dsa(q_hm[:, mp:], ks, vs, qi_hm[:, mp:], wi[mp:], kis, batch=bs, seq=ts, row0=0,
                l_true=ls, q_pos0=past, n_kv=n_kv)
    y_b = jnp.concatenate([ob_p, ob_s], axis=1)

    def merge_ep(accs, ex):
        return [_sigmoid(ex[0].astype(_F32)) * accs[0] + _sigmoid(ex[1].astype(_F32)) * accs[1]]
    (merged,) = _matmul([y_a, y_b], [w_br_a[l].astype(_BF16), w_br_b[l].astype(_BF16)], [ga, gb],
                        merge_ep, [(d_model, _BF16, False)], tm_pref=512, name="branch_merge")
    x0 = jnp.concatenate([x_prompt.reshape(mp, d_model), x_sample.reshape(ms, d_model)], axis=0)
    (x1,) = _matmul([merged], [w_out[l].astype(_BF16)], [x0], lambda accs, ex: [ex[0] + accs[0]],
                    [(d_model, _F32, False)], name="out_proj")

    h2 = _rmsnorm(x1, norm2_g[l], _BF16)
    tn_ff = 512
    d_ffp = _round_up(d_ff, tn_ff)
    wg = _pad_cols(w_ffn_gate[l], d_ffp).astype(_BF16).reshape(d_model, d_ffp // tn_ff, tn_ff)
    wu = _pad_cols(w_ffn_up[l], d_ffp).astype(_BF16).reshape(d_model, d_ffp // tn_ff, tn_ff)
    w_gu = jnp.concatenate([wg, wu], axis=2).reshape(d_model, 2 * d_ffp)
    def swiglu_ep(accs, ex):
        gte, up = accs[0][:, :tn_ff], accs[0][:, tn_ff:]
        return [gte * _sigmoid(gte) * up]
    (u,) = _matmul([h2], [w_gu], [], swiglu_ep, [(d_ffp, _BF16, False)], tn_pref=2 * tn_ff,
                   name="ffn_up")
    wd = _pad_rows(w_ffn_down[l], d_ffp).astype(_BF16)
    tk_dn = _pick_tile(d_ffp, 2048, LANES)
    (x2,) = _matmul([u], [wd], [x1], lambda accs, ex: [ex[0] + accs[0]],
                    [(d_model, _F32, False)], tm_pref=512, tn_pref=1024, tk=tk_dn, name="ffn_down")
    y = _rmsnorm(x2, norm_f_g, _F32)

    y_p = y[:mp].reshape(bp, tp, d_model)
    y_s = y[mp:].reshape(bs, ts, d_model)
    kx4 = lambda x, b, t: x.reshape(1, b, t, n_kv, dh)
    zsh_last = lambda rows: jnp.concatenate(
        [z_rkv[rows], z_lora[rows, :r_w], z_lora[rows, nwp:nwp + r_a], z_lora[rows, nwp + nap:nwp + nap + r_g]],
        axis=-1)
    last_p = jnp.arange(bp) * tp + (tp - 1)
    last_s = mp + jnp.arange(bs) * ts + (ts - 1)
    return (y_p, y_s,
            kx4(kx[:mp], bp, tp), kx4(vx[:mp], bp, tp), ki_new[:mp].reshape(1, bp, tp, d_idx),
            zsh_last(last_p).reshape(1, bp, 1, d_shift), wkv_p[None],
            kx4(kx[mp:], bs, ts), kx4(vx[mp:], bs, ts), ki_new[mp:].reshape(1, bs, ts, d_idx),
            zsh_last(last_s).reshape(1, bs, 1, d_shift), wkv_s[None])
```

```python
import functools
import math

import jax
import jax.numpy as jnp
from jax import lax
from jax.experimental import pallas as pl
from jax.experimental.pallas import tpu as pltpu

CHUNK = 64
NORM_EPS = 1e-6
GN_EPS = 64e-5
TOPK_MAX = 256

LANES = 128
SUBLANES = 8
V7X_VMEM_LIMIT_BYTES = 56 * 1024 * 1024

WKV_CHUNK = 64
WKV_PAIRS_PER_STEP = 2
DIST_EXCLUDED = 1e30
DSA_GROUPS = 4

_F32 = jnp.float32
_BF16 = jnp.bfloat16


def _round_up(n, m):
    return (n + m - 1) // m * m


def _pick_tile(n, pref, align):
    if n <= pref:
        return n
    t = pref // align * align
    while t >= align:
        if n % t == 0:
            return t
        t -= align
    return n


def _compiler_params(semantics):
    return pltpu.CompilerParams(dimension_semantics=semantics,
                                vmem_limit_bytes=V7X_VMEM_LIMIT_BYTES)


def _sigmoid(x):
    return 1.0 / (1.0 + jnp.exp(-x))


def _rms(x, g):
    x = x.astype(_F32)
    ms = jnp.mean(x * x, axis=-1, keepdims=True)
    return x * lax.rsqrt(ms + NORM_EPS) * g


def _rmsnorm_cat_kernel(xa_ref, xb_ref, g_ref, o_ref, *, n_a):
    i = pl.program_id(0)

    @pl.when(i < n_a)
    def _():
        o_ref[...] = _rms(xa_ref[...], g_ref[...]).astype(o_ref.dtype)

    @pl.when(i >= n_a)
    def _():
        o_ref[...] = _rms(xb_ref[...], g_ref[...]).astype(o_ref.dtype)


def _rmsnorm_cat(xa, xb, g, out_dtype):
    (ma, d), mb = xa.shape, xb.shape[0]
    tm = _pick_tile(math.gcd(ma, mb), 512, SUBLANES)
    n_a = ma // tm
    return pl.pallas_call(
        functools.partial(_rmsnorm_cat_kernel, n_a=n_a),
        grid=((ma + mb) // tm,),
        in_specs=[pl.BlockSpec((tm, d), lambda i: (jnp.minimum(i, n_a - 1), 0)),
                  pl.BlockSpec((tm, d), lambda i: (jnp.maximum(i - n_a, 0), 0)),
                  pl.BlockSpec((1, d), lambda i: (0, 0))],
        out_specs=pl.BlockSpec((tm, d), lambda i: (i, 0)),
        out_shape=jax.ShapeDtypeStruct((ma + mb, d), out_dtype),
        compiler_params=_compiler_params(("arbitrary",)),
        name="rmsnorm_cat",
    )(xa, xb, g.reshape(1, d).astype(_F32))


def _rmsnorm_rows_kernel(x_ref, g_ref, o_ref):
    o_ref[...] = _rms(x_ref[...], g_ref[...]).astype(o_ref.dtype)


def _rmsnorm_rows(x, g, out_dtype, row0, rows):
    d = x.shape[1]
    tm = _pick_tile(math.gcd(row0, rows) if row0 else rows, 512, SUBLANES)
    rb0 = row0 // tm
    return pl.pallas_call(
        _rmsnorm_rows_kernel,
        grid=(rows // tm,),
        in_specs=[pl.BlockSpec((tm, d), lambda i: (rb0 + i, 0)),
                  pl.BlockSpec((1, d), lambda i: (0, 0))],
        out_specs=pl.BlockSpec((tm, d), lambda i: (i, 0)),
        out_shape=jax.ShapeDtypeStruct((rows, d), out_dtype),
        compiler_params=_compiler_params(("parallel",)),
        name="rmsnorm_rows",
    )(x, g.reshape(1, d).astype(_F32))


def _matmul_kernel(*refs, n_pairs, n_extras, n_outs, n_k, head_major, out_head_w, epilogue):
    x_refs = refs[:n_pairs]
    w_refs = refs[n_pairs:2 * n_pairs]
    e_refs = refs[2 * n_pairs:2 * n_pairs + n_extras]
    o_refs = refs[2 * n_pairs + n_extras:2 * n_pairs + n_extras + n_outs]
    acc_refs = refs[2 * n_pairs + n_extras + n_outs:]

    def load_x(i):
        if head_major[i]:
            xr = x_refs[i]
            return jnp.concatenate([xr[h] for h in range(xr.shape[0])], axis=1)
        return x_refs[i][...]

    def partial(i):
        return jnp.dot(load_x(i), w_refs[i][...], preferred_element_type=_F32)

    def finish(accs):
        outs = epilogue(accs, [e[...] for e in e_refs])
        for o_ref, val, hw in zip(o_refs, outs, out_head_w):
            if hw:
                for h in range(o_ref.shape[0]):
                    o_ref[h] = val[:, h * hw:(h + 1) * hw].astype(o_ref.dtype)
            else:
                o_ref[...] = val.astype(o_ref.dtype)

    if n_k == 1:
        finish([partial(i) for i in range(n_pairs)])
        return

    k = pl.program_id(2)

    @pl.when(k == 0)
    def _():
        for i in range(n_pairs):
            acc_refs[i][...] = partial(i)

    @pl.when(k > 0)
    def _():
        for i in range(n_pairs):
            acc_refs[i][...] += partial(i)

    @pl.when(k == n_k - 1)
    def _():
        finish([a[...] for a in acc_refs])


def _matmul(xs, ws, extras, epilogue, outs, *, tm, tn, tk=None, name="matmul"):
    n_pairs = len(xs)
    head_major = tuple(x.ndim == 3 for x in xs)
    m = xs[0].shape[1] if head_major[0] else xs[0].shape[0]
    kdim, n = ws[0].shape
    if tk is None:
        tk = kdim
    n_k = kdim // tk
    assert kdim % tk == 0 and m % tm == 0 and n % tn == 0
    n_j = n // tn
    tks = [tk if n_k > 1 else w.shape[0] for w in ws]
    assert all(w.shape == (kdim, n) for w in ws) or n_k == 1

    in_specs = []
    for x, hm, tki in zip(xs, head_major, tks):
        if hm:
            assert n_k == 1
            in_specs.append(pl.BlockSpec((x.shape[0], tm, LANES), lambda i, j, k: (0, i, 0)))
        else:
            in_specs.append(pl.BlockSpec((tm, tki), lambda i, j, k: (i, k)))
    for w, tki in zip(ws, tks):
        in_specs.append(pl.BlockSpec((tki, tn), lambda i, j, k: (k, j)))
    for arr, cols, imap in extras:
        in_specs.append(pl.BlockSpec((tm, cols), lambda i, j, k, imap=imap: imap(i, j)))

    out_specs, out_shapes, out_head_w = [], [], []
    for n_cols, dtype, hw in outs:
        to = n_cols // n_j
        out_head_w.append(hw)
        if hw:
            assert to % hw == 0
            out_specs.append(pl.BlockSpec((to // hw, tm, hw), lambda i, j, k: (j, i, 0)))
            out_shapes.append(jax.ShapeDtypeStruct((n_cols // hw, m, hw), dtype))
        else:
            out_specs.append(pl.BlockSpec((tm, to), lambda i, j, k: (i, j)))
            out_shapes.append(jax.ShapeDtypeStruct((m, n_cols), dtype))

    scratch = [pltpu.VMEM((tm, tn), _F32) for _ in range(n_pairs)] if n_k > 1 else []
    kern = functools.partial(
        _matmul_kernel, n_pairs=n_pairs, n_extras=len(extras), n_outs=len(outs),
        n_k=n_k, head_major=head_major, out_head_w=tuple(out_head_w), epilogue=epilogue)
    return pl.pallas_call(
        kern,
        grid=(m // tm, n_j, n_k),
        in_specs=in_specs,
        out_specs=out_specs,
        out_shape=out_shapes,
        scratch_shapes=scratch,
        compiler_params=_compiler_params(("parallel", "parallel", "arbitrary")),
        name=name,
    )(*xs, *ws, *[e[0] for e in extras])


def _shift_rows(z, prev_row):
    rolled = pltpu.roll(z, 1, 0)
    row = lax.broadcasted_iota(jnp.int32, z.shape, 0)
    return jnp.where(row == 0, prev_row, rolled)


def _cumsum_rows(x):
    n = x.shape[0]
    row = lax.broadcasted_iota(jnp.int32, x.shape, 0)
    s = 1
    while s < n:
        x = x + jnp.where(row >= s, pltpu.roll(x, s, 0), 0.0)
        s *= 2
    return x


def _dot(a, b):
    return jnp.dot(a.astype(_BF16), b.astype(_BF16), preferred_element_type=_F32)


def _dot_nt(a, b):
    return lax.dot_general(a.astype(_BF16), b.astype(_BF16), (((1,), (1,)), ((), ())),
                           preferred_element_type=_F32)


def _wkv_kernel(*refs, n_chunks, head, n_pp, aliased):
    (zr_ref, zk_ref, zv_ref, zl_ref, sprev_ref, lprev_ref, s0_ref,
     mu_ref, mul_ref, w0_ref, a0_ref, kkp_ref, kap_ref, rk_ref, lg_ref, lb_ref,
     w2_ref, a2_ref, g2_ref) = refs[:19]
    y_ref, sout_ref, st_ref, prev_ref, lprev_scr = refs[19 + aliased:]
    c_idx = pl.program_id(2)
    C = WKV_CHUNK
    P = 2 * head
    assert P == 2 * C
    tb = zr_ref.shape[0]
    n_sub = tb // C

    @pl.when(c_idx == 0)
    def _():
        st_ref[...] = s0_ref[0]
        prev_ref[...] = sprev_ref[0, 0]
        lprev_scr[...] = lprev_ref[0]

    lane = lax.broadcasted_iota(jnp.int32, (1, P), 1)
    m0 = (lane < head).astype(_F32)
    m1 = 1.0 - m0
    r2 = lax.broadcasted_iota(jnp.int32, (P, P), 0)
    c2 = lax.broadcasted_iota(jnp.int32, (P, P), 1)
    same = (r2 // C) == (c2 // C)
    tril_s = jnp.where(same & (c2 < r2), 1.0, 0.0)
    tril_i = jnp.where(same & (c2 <= r2), 1.0, 0.0)
    eye = jnp.where(r2 == c2, 1.0, 0.0)

    def stack(x):
        return jnp.concatenate([x * m0, x * m1], axis=0)

    def head_sum(x):
        s0 = jnp.sum(x * m0, axis=-1, keepdims=True)
        s1 = jnp.sum(x * m1, axis=-1, keepdims=True)
        return s0 * m0 + s1 * m1

    zl = zl_ref[...]
    xl = zl + (_shift_rows(zl, lprev_scr[...]) - zl) * mul_ref[...]
    lprev_scr[...] = zl[tb - 1:tb, :]
    nw = w2_ref.shape[0]
    na = a2_ref.shape[0]
    tw = jnp.tanh(xl[:, 0:nw]).astype(_BF16)
    xa = xl[:, nw:nw + na].astype(_BF16)
    sg = _sigmoid(xl[:, nw + na:]).astype(_BF16)

    pairs = []
    for pi in range(n_pp):
        ls = slice(pi * P, (pi + 1) * P)
        zr, zk, zv = zr_ref[:, ls], zk_ref[:, ls], zv_ref[:, ls]
        r = zr + (_shift_rows(zr, prev_ref[0:1, ls]) - zr) * mu_ref[0:1, ls]
        k = zk + (_shift_rows(zk, prev_ref[1:2, ls]) - zk) * mu_ref[1:2, ls]
        v = zv + (_shift_rows(zv, prev_ref[2:3, ls]) - zv) * mu_ref[2:3, ls]
        prev_ref[0:1, ls] = zr[tb - 1:tb, :]
        prev_ref[1:2, ls] = zk[tb - 1:tb, :]
        prev_ref[2:3, ls] = zv[tb - 1:tb, :]
        u = -(w0_ref[:, ls] + jnp.dot(tw, w2_ref[:, ls], preferred_element_type=_F32))
        softplus = jnp.maximum(u, 0.0) + jnp.log(1.0 + jnp.exp(-jnp.abs(u)))
        logw = -jnp.exp(-softplus - 0.5)
        a = _sigmoid(a0_ref[:, ls] + jnp.dot(xa, a2_ref[:, ls], preferred_element_type=_F32))
        g = jnp.dot(sg, g2_ref[:, ls], preferred_element_type=_F32)
        kk = k * kkp_ref[:, ls]
        kk = kk / jnp.maximum(jnp.sqrt(head_sum(kk * kk)), 1e-12)
        kh = k * (1.0 + (a - 1.0) * kap_ref[:, ls])
        pairs.append(dict(ls=ls, r=r, v=v, logw=logw, kk=kk, kh=kh, beta=kk * a, g=g,
                          bonus=head_sum(r * kh * rk_ref[:, ls]) * v))

    states = [st_ref[pi] for pi in range(n_pp)]
    group = n_sub
    for sc0 in range(0, n_sub, group):
        items = []
        for sc in range(sc0, sc0 + group):
            rs = slice(sc * C, (sc + 1) * C)
            for pi, pr in enumerate(pairs):
                lw = pr["logw"][rs]
                cum = _cumsum_rows(lw)
                cum_c = cum[C - 1:C, :]
                e_neg = jnp.exp(-cum)
                e_end = jnp.exp(cum_c - cum)
                items.append(dict(
                    pi=pi, rs=rs,
                    a2s=stack(-pr["kk"][rs] * jnp.exp(cum - lw)),
                    r2s=stack(pr["r"][rs] * jnp.exp(cum)),
                    b2s=stack(pr["beta"][rs] * e_neg),
                    k2s=stack(pr["kh"][rs] * e_neg),
                    v2s=stack(pr["v"][rs]),
                    bh2t=stack(pr["beta"][rs] * e_end).T,
                    kh2t=stack(pr["kh"][rs] * e_end).T,
                    dec_col=jnp.sum(eye * jnp.exp(cum_c), axis=-1, keepdims=True)))
        for it in items:
            nn = _dot_nt(jnp.concatenate([it["a2s"], it["r2s"]], axis=0),
                         jnp.concatenate([it["b2s"], it["k2s"]], axis=0))
            it["n_ba"] = nn[:P, :P] * tril_s
            it["n_ka"] = nn[:P, P:] * tril_s
            it["p_br"] = nn[P:, :P] * tril_i
            it["p_kr"] = nn[P:, P:] * tril_i
            it["t_inv"] = eye + it["n_ba"]
            it["pw"] = it["n_ba"]
        s = 2
        while s < C:
            for it in items:
                it["pw"] = _dot(it["pw"], it["pw"])
            for it in items:
                it["t_inv"] = it["t_inv"] + _dot(it["t_inv"], it["pw"])
            s *= 2
        for it in items:
            it["nkv"] = _dot(it["n_ka"], it["v2s"])
        for it in items:
            it["tt"] = _dot(it["t_inv"], jnp.concatenate([it["a2s"], it["nkv"]], axis=1))
        for it in items:
            it["mg"] = _dot(it["bh2t"], it["tt"])
        for it in items:
            it["pg"] = _dot(it["p_br"], it["tt"])
        for it in items:
            it["g_c"] = it["mg"][:, P:] + _dot(it["kh2t"], it["v2s"])
        for it in items:
            it["yg"] = it["pg"][:, P:] + _dot(it["p_kr"], it["v2s"])
        for it in items:
            pi, rs, pr = it["pi"], it["rs"], pairs[it["pi"]]
            ls = pr["ls"]
            st = states[pi]
            y2 = _dot(it["r2s"] + it["pg"][:, :P], st) + it["yg"]
            states[pi] = it["dec_col"] * st + _dot(it["mg"][:, :P], st) + it["g_c"]
            y = y2[0:C, :] + y2[C:2 * C, :]
            mean = head_sum(y) * (1.0 / head)
            yc = y - mean
            var = head_sum(yc * yc) * (1.0 / head)
            yn = yc * lax.rsqrt(var + GN_EPS) * lg_ref[:, ls] + lb_ref[:, ls]
            y_ref[rs, ls] = ((yn + pr["bonus"][rs]) * pr["g"][rs]).astype(y_ref.dtype)
    for pi in range(n_pp):
        st_ref[pi] = states[pi]

    @pl.when(c_idx == n_chunks - 1)
    def _():
        sout_ref[0] = st_ref[...]


def _wkv(z, shift_prev, s0_bd, params, *, batch, seq, row0, head, d_a, lora_sizes, m_out, out_buf):
    pw = 2 * head
    n_pairs = d_a // pw
    n_pp = min(WKV_PAIRS_PER_STEP, n_pairs)
    gw = n_pp * pw
    n_groups = n_pairs // n_pp
    lp = sum(lora_sizes)
    assert (3 * d_a) % lp == 0
    tb = _pick_tile(seq, 256, WKV_CHUNK)
    n_chunks = seq // tb
    assert row0 % tb == 0
    rb0 = row0 // tb
    rpb = seq // tb
    aliased = out_buf is not None

    def rows(b, c):
        return rb0 + b * rpb + c

    in_specs = [
        pl.BlockSpec((tb, gw), lambda b, p, c: (rows(b, c), p)),
        pl.BlockSpec((tb, gw), lambda b, p, c: (rows(b, c), n_groups + p)),
        pl.BlockSpec((tb, gw), lambda b, p, c: (rows(b, c), 2 * n_groups + p)),
        pl.BlockSpec((tb, lp), lambda b, p, c: (rows(b, c), 3 * d_a // lp)),
        pl.BlockSpec((1, 1, SUBLANES, gw), lambda b, p, c: (b, p, 0, 0)),
        pl.BlockSpec((1, 1, lp), lambda b, p, c: (b, 0, 0)),
        pl.BlockSpec((1, n_pp, pw, pw), lambda b, p, c: (b, p, 0, 0)),
        pl.BlockSpec((SUBLANES, gw), lambda b, p, c: (p, 0)),
        pl.BlockSpec((1, lp), lambda b, p, c: (0, 0)),
    ]
    for _ in range(7):
        in_specs.append(pl.BlockSpec((1, gw), lambda b, p, c: (0, p)))
    nw, na, ng = lora_sizes
    in_specs += [pl.BlockSpec((nw, gw), lambda b, p, c: (0, p)),
                 pl.BlockSpec((na, gw), lambda b, p, c: (0, p)),
                 pl.BlockSpec((ng, gw), lambda b, p, c: (0, p))]
    args = [z, z, z, z, shift_prev["rkv"], shift_prev["lora"], s0_bd,
            params["mu_rkv"], params["mu_lora"], params["w0"], params["a0"], params["k_k"],
            params["k_a"], params["r_k"], params["lnx_g"], params["lnx_b"],
            params["w2"], params["a2"], params["g2"]]
    aliases = {}
    if aliased:
        in_specs.append(pl.BlockSpec(memory_space=pl.ANY))
        aliases = {len(args): 0}
        args.append(out_buf)
    out_specs = [pl.BlockSpec((tb, gw), lambda b, p, c: (rows(b, c), p)),
                 pl.BlockSpec((1, n_pp, pw, pw), lambda b, p, c: (b, p, 0, 0))]
    out_shape = [jax.ShapeDtypeStruct((m_out, d_a), _BF16),
                 jax.ShapeDtypeStruct((batch, n_pairs, pw, pw), _F32)]
    kern = functools.partial(_wkv_kernel, n_chunks=n_chunks, head=head, n_pp=n_pp, aliased=int(aliased))
    return pl.pallas_call(
        kern,
        grid=(batch, n_groups, n_chunks),
        in_specs=in_specs,
        out_specs=out_specs,
        out_shape=out_shape,
        scratch_shapes=[pltpu.VMEM((n_pp, pw, pw), _F32),
                        pltpu.VMEM((SUBLANES, gw), _F32),
                        pltpu.VMEM((1, lp), _F32)],
        input_output_aliases=aliases,
        compiler_params=_compiler_params(("parallel", "parallel", "arbitrary")),
        name="wkv7_chunked",
    )(*args)


def _dsa_kernel(*refs, tq, l_ext, l_true, q_pos0, topk, idx_scale, n_kv, dh, d_idx, aliased):
    q_ref, k_ref, v_ref, qi_ref, wi_ref, ki_ref = refs[:6]
    o_ref, kbf, vbf, kibf, wib, isc_scr, key_scr, dm_scr = refs[6 + aliased:]
    qb = pl.program_id(1)
    n_q = q_ref.shape[0]
    n_i = qi_ref.shape[0]
    rep = n_q // n_kv

    @pl.when(qb == 0)
    def _():
        for g in range(n_kv):
            kbf[g] = k_ref[0:l_ext, g * dh:(g + 1) * dh].astype(_BF16)
            vbf[g] = v_ref[0:l_ext, g * dh:(g + 1) * dh].astype(_BF16)
        kibf[...] = ki_ref[0:l_ext, 0:d_idx].astype(_BF16)

    wi = wi_ref[...]
    for hi in range(n_i):
        wib[hi] = jnp.broadcast_to(wi[:, hi:hi + 1], (tq, LANES))
    isc_scr[...] = jnp.zeros((tq, l_ext), _F32)
    ki = kibf[...]

    hs = 4 if n_i % 4 == 0 else 1

    def idx_body(i, carry):
        qs = qi_ref[pl.ds(i * hs, hs)].reshape(hs * tq, d_idx)
        sc = lax.dot_general(qs, ki, (((1,), (1,)), ((), ())), preferred_element_type=_F32)
        acc = isc_scr[...]
        for j in range(hs):
            acc = acc + (jnp.maximum(sc[j * tq:(j + 1) * tq], 0.0)
                         * jnp.tile(wib[i * hs + j], (1, l_ext // LANES)))
        isc_scr[...] = acc
        return carry

    lax.fori_loop(0, n_i // hs, idx_body, 0)

    q_pos = q_pos0 + qb * tq + lax.broadcasted_iota(jnp.int32, (tq, l_ext), 0)
    k_pos = lax.broadcasted_iota(jnp.int32, (tq, l_ext), 1)
    allowed = ((k_pos // CHUNK) <= (q_pos // CHUNK)) & (k_pos < l_true)
    bits = pltpu.bitcast(isc_scr[...] * idx_scale, jnp.int32)
    int_min = jnp.int32(-2 ** 31)
    key_scr[...] = jnp.where(allowed, jnp.where(bits < 0, bits ^ jnp.int32(0x7FFFFFFF), bits), int_min)

    def count_ge(cand):
        return jnp.sum(jnp.where(key_scr[...] >= cand, 1.0, 0.0), axis=-1, keepdims=True)

    kf = jnp.float32(topk)
    zero = jnp.zeros((tq, 1), jnp.int32)
    thr = jnp.where(count_ge(zero) >= kf, zero, jnp.full((tq, 1), int_min, jnp.int32))

    def thr_body(i, thr):
        cand = thr + (jnp.int32(1) << (jnp.int32(30) - i))
        return jnp.where(count_ge(cand) >= kf, cand, thr)

    thr = lax.fori_loop(0, 31, thr_body, thr)
    key = key_scr[...]
    n_gt = jnp.sum(jnp.where(key > thr, 1.0, 0.0), axis=-1, keepdims=True)
    n_eq = jnp.sum(jnp.where(key == thr, 1.0, 0.0), axis=-1, keepdims=True)
    need = kf - n_gt
    n_bits = max(1, (l_ext - 1).bit_length())
    full = jnp.full((tq, 1), 1 << n_bits, jnp.int32)

    def tie_bound():
        def count_tie_below(bound):
            hit = jnp.where(key_scr[...] == thr, jnp.where(k_pos < bound, 1.0, 0.0), 0.0)
            return jnp.sum(hit, axis=-1, keepdims=True)

        def tbody(i, bound):
            cand = bound + (jnp.int32(1) << (jnp.int32(n_bits - 1) - i))
            return jnp.where(count_tie_below(cand) <= need, cand, bound)

        return lax.fori_loop(0, n_bits, tbody, zero)

    has_excess_ties = jnp.max(n_eq - need) > 0.0
    bound = lax.cond(has_excess_ties, tie_bound, lambda: full)
    sel = allowed & ((key > thr) | ((key == thr) & (k_pos < bound)))
    dm_scr[...] = jnp.where(sel, jnp.abs(q_pos - k_pos).astype(_F32), DIST_EXCLUDED)

    log2 = 0.6931471805599453

    hu = 2 if rep % 2 == 0 else 1

    def head_body(i, carry):
        g = (i * hu) // rep
        kg, vg = kbf[g], vbf[g]
        dm = dm_scr[...]
        ss = [lax.dot_general(q_ref[i * hu + j], kg, (((1,), (1,)), ((), ())),
                              preferred_element_type=_F32) for j in range(hu)]
        for j in range(hu):
            h = i * hu + j
            slope = jnp.exp(jnp.full((1, 1), h + 1, jnp.int32).astype(_F32)
                            * (-8.0 / n_q * log2))
            s = ss[j] - slope * dm
            m = jnp.max(s, axis=-1, keepdims=True)
            p = jnp.exp(s - m)
            l = jnp.sum(p, axis=-1, keepdims=True)
            o = jnp.dot(p.astype(_BF16), vg, preferred_element_type=_F32)
            o_ref[h] = (o / l).astype(o_ref.dtype)
        return carry

    lax.fori_loop(0, n_q // hu, head_body, 0)


def _dsa(q_hm, qi_hm, kv_src, k_col, v_col, ki_src, ki_col, wi_src, wi_col, *, batch, n_qb, tq,
         q_row0, q_stride, kv_rows, kv_stride, l_ext, l_true, q_pos0, topk, n_kv, m_out, out_buf):
    n_q, _, dh = q_hm.shape
    n_i, _, d_idx = qi_hm.shape
    assert q_row0 % tq == 0 and q_stride % tq == 0 and l_ext % LANES == 0 and l_ext <= kv_rows
    qr0, qst = q_row0 // tq, q_stride // tq
    aliased = out_buf is not None
    kern = functools.partial(
        _dsa_kernel, tq=tq, l_ext=l_ext, l_true=l_true, q_pos0=q_pos0, topk=topk,
        idx_scale=float(n_i) ** -0.5 * float(d_idx) ** -0.5, n_kv=n_kv, dh=dh, d_idx=d_idx,
        aliased=int(aliased))
    qrow = lambda b, i: qr0 + b * qst + i
    in_specs = [
        pl.BlockSpec((n_q, tq, dh), lambda b, i: (0, qrow(b, i), 0)),
        pl.BlockSpec((kv_rows, n_kv * dh), lambda b, i: (b * kv_stride, k_col)),
        pl.BlockSpec((kv_rows, n_kv * dh), lambda b, i: (b * kv_stride, v_col)),
        pl.BlockSpec((n_i, tq, d_idx), lambda b, i: (0, qrow(b, i), 0)),
        pl.BlockSpec((tq, LANES), lambda b, i: (qrow(b, i), wi_col)),
        pl.BlockSpec((kv_rows, LANES), lambda b, i: (b * kv_stride, ki_col)),
    ]
    args = [q_hm, kv_src, kv_src, qi_hm, wi_src, ki_src]
    aliases = {}
    if aliased:
        in_specs.append(pl.BlockSpec(memory_space=pl.ANY))
        aliases = {len(args): 0}
        args.append(out_buf)
    return pl.pallas_call(
        kern,
        grid=(batch, n_qb),
        in_specs=in_specs,
        out_specs=pl.BlockSpec((n_q, tq, dh), lambda b, i: (0, qrow(b, i), 0)),
        out_shape=jax.ShapeDtypeStruct((n_q, m_out, dh), _BF16),
        scratch_shapes=[pltpu.VMEM((n_kv, l_ext, dh), _BF16), pltpu.VMEM((n_kv, l_ext, dh), _BF16),
                        pltpu.VMEM((l_ext, d_idx), _BF16), pltpu.VMEM((n_i, tq, LANES), _F32),
                        pltpu.VMEM((tq, l_ext), _F32), pltpu.VMEM((tq, l_ext), jnp.int32),
                        pltpu.VMEM((tq, l_ext), _F32)],
        input_output_aliases=aliases,
        compiler_params=_compiler_params(("parallel", "arbitrary")),
        name="dsa_attention",
    )(*args)


def _pad_rows(w, rows):
    return jnp.pad(w, ((0, rows - w.shape[0]), (0, 0)))


def _pad_cols(w, cols):
    return jnp.pad(w, ((0, 0), (0, cols - w.shape[1])))


def _to_block_diag_t(s, head):
    b, h = s.shape[:2]
    st = jnp.swapaxes(s, -1, -2).reshape(b, h // 2, 2, head, head).astype(_F32)
    z = jnp.zeros_like(st[:, :, 0])
    top = jnp.concatenate([st[:, :, 0], z], axis=-1)
    bot = jnp.concatenate([z, st[:, :, 1]], axis=-1)
    return jnp.concatenate([top, bot], axis=-2)


def _from_block_diag_t(sbd, head):
    b, hp = sbd.shape[:2]
    s0 = sbd[:, :, :head, :head]
    s1 = sbd[:, :, head:, head:]
    st = jnp.stack([s0, s1], axis=2).reshape(b, hp * 2, head, head)
    return jnp.swapaxes(st, -1, -2)


def kernel(x_prompt, x_sample, cache_k, cache_v, cache_kidx, state_shift, state_wkv, norm1_g, w_in, mu_shift, w0, w2, a0, a2, g2, k_k, k_a, r_k, lnx_g, lnx_b, w_br_a, w_br_b, w_out, norm2_g, w_ffn_gate, w_ffn_up, w_ffn_down, norm_f_g):
    depth = w_in.shape[0]
    assert depth == 1
    bp, tp, d_model = x_prompt.shape
    bs, ts, _ = x_sample.shape
    _, _, past, n_kv, dh = cache_k.shape
    d_idx = cache_kidx.shape[-1]
    d_shift = state_shift.shape[-1]
    _, _, h_a, head, _ = state_wkv.shape
    d_a = h_a * head
    r_w, r_a, r_g = w2.shape[1], a2.shape[1], g2.shape[1]
    n_q = w_br_b.shape[1] // dh
    d_in = w_in.shape[-1]
    h_i = (d_in - d_shift - n_q * dh - 2 * n_kv * dh - d_idx - 2 * d_model) // (d_idx + 1)
    d_ff = w_ffn_gate.shape[-1]
    kvw = n_kv * dh
    l = 0
    mp, ms = bp * tp, bs * ts
    m_tot = mp + ms
    xp2, xs2 = x_prompt.reshape(mp, d_model), x_sample.reshape(ms, d_model)
    tm_io = _pick_tile(math.gcd(mp, ms), 512, 2 * SUBLANES)
    tm_mm = _pick_tile(m_tot, 1024, 2 * SUBLANES)
    n_io_p = mp // tm_io

    win = w_in[l]
    o = 0
    def take(n):
        nonlocal o
        blk = win[:, o:o + n]
        o += n
        return blk
    nwp, nap, ngp = _round_up(r_w, LANES), _round_up(r_a, LANES), _round_up(r_g, LANES)
    lp = nwp + nap + ngp
    w_rkv, w_lw, w_la, w_lg = take(3 * d_a), take(r_w), take(r_a), take(r_g)
    w_rkvl = jnp.concatenate([w_rkv, _pad_cols(w_lw, nwp), _pad_cols(w_la, nap), _pad_cols(w_lg, ngp)],
                             axis=1).astype(_BF16)
    w_q = take(n_q * dh).astype(_BF16)
    w_k, w_v = take(kvw), take(kvw)
    w_qi = take(h_i * d_idx).astype(_BF16)
    w_ki, w_wi = take(d_idx), take(h_i)
    w_kv = jnp.concatenate([w_k, w_v, _pad_cols(w_ki, LANES), _pad_cols(w_wi, LANES)], axis=1).astype(_BF16)
    w_gates = jnp.concatenate([take(d_model), take(d_model)], axis=1).astype(_BF16)

    h1 = _rmsnorm_cat(xp2, xs2, norm1_g[l], _BF16)
    ident = lambda accs, ex: [accs[0]]
    n_rkvl = 3 * d_a + lp
    (z_a,) = _matmul([h1], [w_rkvl], [], ident, [(n_rkvl, _F32, None)],
                     tm=tm_mm, tn=_pick_tile(n_rkvl, 512, LANES), name="proj_rkv_lora")
    n_kvz = 2 * kvw + 2 * LANES
    (z_kv,) = _matmul([h1], [w_kv], [], ident, [(n_kvz, _F32, None)],
                      tm=tm_mm, tn=_pick_tile(n_kvz, 640, LANES), name="proj_kv_idx")
    q_scale = float(dh) ** -0.5
    (q_hm,) = _matmul([h1], [w_q], [], lambda accs, ex: [accs[0] * q_scale],
                      [(n_q * dh, _BF16, dh)], tm=tm_mm, tn=_pick_tile(n_q * dh, 512, dh), name="proj_q")
    (qi_hm,) = _matmul([h1], [w_qi], [], ident, [(h_i * d_idx, _BF16, d_idx)],
                       tm=tm_mm, tn=_pick_tile(h_i * d_idx, 512, LANES), name="proj_qi")
    (gates,) = _matmul([h1], [w_gates], [], ident, [(2 * d_model, _BF16, None)],
                       tm=tm_mm, tn=_pick_tile(2 * d_model, 512, LANES), name="proj_gates")
    kx, vx = z_kv[:, :kvw], z_kv[:, kvw:2 * kvw]
    ki_new = z_kv[:, 2 * kvw:2 * kvw + d_idx]

    mu = mu_shift[l]
    def lora_row(vec):
        return jnp.concatenate([
            jnp.pad(vec[3 * d_a:3 * d_a + r_w], (0, nwp - r_w)),
            jnp.pad(vec[3 * d_a + r_w:3 * d_a + r_w + r_a], (0, nap - r_a)),
            jnp.pad(vec[3 * d_a + r_w + r_a:], (0, ngp - r_g))])
    n_pairs = d_a // (2 * head)
    gw = min(WKV_PAIRS_PER_STEP, n_pairs) * 2 * head
    n_groups = d_a // gw
    def rkv_rows(vec):
        lead = vec.shape[:-1]
        x = vec[..., :3 * d_a].reshape(lead + (3, n_groups, gw))
        x = jnp.moveaxis(x, -3, -2)
        return jnp.pad(x, [(0, 0)] * (len(lead) + 1) + [(0, SUBLANES - 3), (0, 0)])
    row = lambda v: v[l].reshape(1, d_a).astype(_F32)
    params = dict(
        mu_rkv=rkv_rows(mu).reshape(n_groups * SUBLANES, gw),
        mu_lora=lora_row(mu).reshape(1, lp),
        w0=row(w0), a0=row(a0), k_k=row(k_k), k_a=row(k_a),
        r_k=r_k[l].reshape(1, d_a).astype(_F32), lnx_g=row(lnx_g), lnx_b=row(lnx_b),
        w2=_pad_rows(w2[l], nwp).astype(_BF16), a2=_pad_rows(a2[l], nap).astype(_BF16),
        g2=_pad_rows(g2[l], ngp).astype(_BF16))

    def shift_state(s):
        s = s[:, 0]
        return dict(rkv=rkv_rows(s), lora=jax.vmap(lora_row)(s).reshape(-1, 1, lp))

    wkv_kw = dict(head=head, d_a=d_a, lora_sizes=(nwp, nap, ngp), m_out=m_tot)
    y_a, sp_bd = _wkv(z_a, shift_state(jnp.zeros((bp, 1, d_shift), _F32)),
                      jnp.zeros((bp, n_pairs, 2 * head, 2 * head), _F32), params,
                      batch=bp, seq=tp, row0=0, out_buf=None, **wkv_kw)
    y_a, ss_bd = _wkv(z_a, shift_state(state_shift[l]), _to_block_diag_t(state_wkv[l], head), params,
                      batch=bs, seq=ts, row0=mp, out_buf=y_a, **wkv_kw)
    wkv_p = _from_block_diag_t(sp_bd, head)
    wkv_s = _from_block_diag_t(ss_bd, head)

    topk_p = min(TOPK_MAX, tp // 4)
    tq_p = _pick_tile(tp, 128, CHUNK)
    n_grp = max(1, min(DSA_GROUPS, tp // tq_p))
    grp = tp // n_grp
    wi_col = (2 * kvw + LANES) // LANES
    ki_col = 2 * kvw // LANES
    y_b = None
    for gi in range(n_grp):
        l_ext = (gi + 1) * grp
        kv_rows = l_ext
        while tp % kv_rows:
            kv_rows += grp
        y_b = _dsa(q_hm, qi_hm, z_kv, 0, 1, z_kv, ki_col, z_kv, wi_col,
                   batch=bp, n_qb=grp // tq_p, tq=tq_p, q_row0=gi * grp, q_stride=tp,
                   kv_rows=kv_rows, kv_stride=tp // kv_rows, l_ext=l_ext, l_true=l_ext,
                   q_pos0=gi * grp, topk=topk_p, n_kv=n_kv, m_out=m_tot, out_buf=y_b)
    ls = past + ts
    lps = _round_up(ls, LANES)
    def with_cache(cache, new, width):
        x = jnp.concatenate([cache.reshape(bs, past, width), new.reshape(bs, ts, width)], axis=1)
        return jnp.pad(x, ((0, 0), (0, lps - ls), (0, 0))).reshape(bs * lps, width)
    kv_s = jnp.concatenate([with_cache(cache_k[l], kx[mp:], kvw), with_cache(cache_v[l], vx[mp:], kvw)], axis=1)
    ki_s = _pad_cols(with_cache(cache_kidx[l], ki_new[mp:], d_idx), LANES)
    y_b = _dsa(q_hm, qi_hm, kv_s, 0, 1, ki_s, 0, z_kv, wi_col,
               batch=bs, n_qb=1, tq=ts, q_row0=mp, q_stride=ts, kv_rows=lps, kv_stride=1, l_ext=lps, l_true=ls,
               q_pos0=past, topk=min(TOPK_MAX, ls // 4), n_kv=n_kv, m_out=m_tot, out_buf=y_b)

    tn_mg = _pick_tile(d_model, 512, LANES)
    nj_mg = d_model // tn_mg
    def merge_ep(accs, ex):
        return [_sigmoid(ex[0].astype(_F32)) * accs[0] + _sigmoid(ex[1].astype(_F32)) * accs[1]]
    (merged,) = _matmul([y_a, y_b], [w_br_a[l].astype(_BF16), w_br_b[l].astype(_BF16)],
                        [(gates, tn_mg, lambda i, j: (i, j)), (gates, tn_mg, lambda i, j: (i, nj_mg + j))],
                        merge_ep, [(d_model, _BF16, None)], tm=_pick_tile(m_tot, 512, 2 * SUBLANES),
                        tn=tn_mg, name="branch_merge")
    tn_res = _pick_tile(d_model, 1024, LANES)
    res_extras = lambda: [(xp2, tn_res, lambda i, j: (jnp.minimum(i, n_io_p - 1), j)),
                          (xs2, tn_res, lambda i, j: (jnp.maximum(i - n_io_p, 0), j))]
    def resid_ep(accs, ex):
        return [jnp.where(pl.program_id(0) < n_io_p, ex[0], ex[1]) + accs[0]]
    (x1,) = _matmul([merged], [w_out[l].astype(_BF16)], res_extras(), resid_ep,
                    [(d_model, _F32, None)], tm=tm_io, tn=tn_res, name="out_proj")

    h2 = _rmsnorm_rows(x1, norm2_g[l], _BF16, 0, m_tot)
    tn_ff = 512
    d_ffp = _round_up(d_ff, tn_ff)
    wg = _pad_cols(w_ffn_gate[l], d_ffp).astype(_BF16).reshape(d_model, d_ffp // tn_ff, tn_ff)
    wu = _pad_cols(w_ffn_up[l], d_ffp).astype(_BF16).reshape(d_model, d_ffp // tn_ff, tn_ff)
    w_gu = jnp.concatenate([wg, wu], axis=2).reshape(d_model, 2 * d_ffp)
    def swiglu_ep(accs, ex):
        gte, up = accs[0][:, :tn_ff], accs[0][:, tn_ff:]
        return [gte * _sigmoid(gte) * up]
    (u,) = _matmul([h2], [w_gu], [], swiglu_ep, [(d_ffp, _BF16, None)], tm=tm_mm, tn=2 * tn_ff,
                   name="ffn_up")
    wd = _pad_rows(w_ffn_down[l], d_ffp).astype(_BF16)
    tk_dn = d_ffp // 2 if (d_ffp // 2) % LANES == 0 else d_ffp
    (x2,) = _matmul([u], [wd], [(x1, tn_res, lambda i, j: (i, j))], lambda accs, ex: [ex[0] + accs[0]],
                    [(d_model, _F32, None)], tm=tm_io, tn=tn_res, tk=tk_dn, name="ffn_down")
    y_p = _rmsnorm_rows(x2, norm_f_g, _F32, 0, mp).reshape(bp, tp, d_model)
    y_s = _rmsnorm_rows(x2, norm_f_g, _F32, mp, ms).reshape(bs, ts, d_model)

    kx4 = lambda x, b, t: x.reshape(1, b, t, n_kv, dh)
    def zsh_last(rows):
        zl = z_a[rows]
        return jnp.concatenate([zl[:, :3 * d_a + r_w], zl[:, 3 * d_a + nwp:3 * d_a + nwp + r_a],
                                zl[:, 3 * d_a + nwp + nap:3 * d_a + nwp + nap + r_g]], axis=-1)
    last_p = jnp.arange(bp) * tp + (tp - 1)
    last_s = mp + jnp.arange(bs) * ts + (ts - 1)
    return (y_p, y_s,
            kx4(kx[:mp], bp, tp), kx4(vx[:mp], bp, tp), ki_new[:mp].reshape(1, bp, tp, d_idx),
            zsh_last(last_p).reshape(1, bp, 1, d_shift), wkv_p[None],
            kx4(kx[mp:], bs, ts), kx4(vx[mp:], bs, ts), ki_new[mp:].reshape(1, bs, ts, d_idx),
            zsh_last(last_s).reshape(1, bs, 1, d_shift), wkv_s[None])
```

```python
import functools
import math

import jax
import jax.numpy as jnp
from jax import lax
from jax.experimental import pallas as pl
from jax.experimental.pallas import tpu as pltpu

CHUNK = 64
NORM_EPS = 1e-6
GN_EPS = 64e-5
TOPK_MAX = 256

LANES = 128
SUBLANES = 8
V7X_VMEM_LIMIT_BYTES = 56 * 1024 * 1024

WKV_CHUNK = 64
WKV_PAIRS_PER_STEP = 2
DIST_EXCLUDED = 1e30
DSA_GROUPS = 4

_F32 = jnp.float32
_BF16 = jnp.bfloat16


def _round_up(n, m):
    return (n + m - 1) // m * m


def _pick_tile(n, pref, align):
    if n <= pref:
        return n
    t = pref // align * align
    while t >= align:
        if n % t == 0:
            return t
        t -= align
    return n


def _compiler_params(semantics):
    return pltpu.CompilerParams(dimension_semantics=semantics,
                                vmem_limit_bytes=V7X_VMEM_LIMIT_BYTES)


def _sigmoid(x):
    return 1.0 / (1.0 + jnp.exp(-x))


def _rms(x, g):
    x = x.astype(_F32)
    ms = jnp.mean(x * x, axis=-1, keepdims=True)
    return x * lax.rsqrt(ms + NORM_EPS) * g


def _rmsnorm_cat_kernel(xa_ref, xb_ref, g_ref, o_ref, *, n_a):
    i = pl.program_id(0)

    @pl.when(i < n_a)
    def _():
        o_ref[...] = _rms(xa_ref[...], g_ref[...]).astype(o_ref.dtype)

    @pl.when(i >= n_a)
    def _():
        o_ref[...] = _rms(xb_ref[...], g_ref[...]).astype(o_ref.dtype)


def _rmsnorm_cat(xa, xb, g, out_dtype):
    (ma, d), mb = xa.shape, xb.shape[0]
    tm = _pick_tile(math.gcd(ma, mb), 512, SUBLANES)
    n_a = ma // tm
    return pl.pallas_call(
        functools.partial(_rmsnorm_cat_kernel, n_a=n_a),
        grid=((ma + mb) // tm,),
        in_specs=[pl.BlockSpec((tm, d), lambda i: (jnp.minimum(i, n_a - 1), 0)),
                  pl.BlockSpec((tm, d), lambda i: (jnp.maximum(i - n_a, 0), 0)),
                  pl.BlockSpec((1, d), lambda i: (0, 0))],
        out_specs=pl.BlockSpec((tm, d), lambda i: (i, 0)),
        out_shape=jax.ShapeDtypeStruct((ma + mb, d), out_dtype),
        compiler_params=_compiler_params(("arbitrary",)),
        name="rmsnorm_cat",
    )(xa, xb, g.reshape(1, d).astype(_F32))


def _rmsnorm_rows_kernel(x_ref, g_ref, o_ref):
    o_ref[...] = _rms(x_ref[...], g_ref[...]).astype(o_ref.dtype)


def _rmsnorm_rows(x, g, out_dtype, row0, rows):
    d = x.shape[1]
    tm = _pick_tile(math.gcd(row0, rows) if row0 else rows, 512, SUBLANES)
    rb0 = row0 // tm
    return pl.pallas_call(
        _rmsnorm_rows_kernel,
        grid=(rows // tm,),
        in_specs=[pl.BlockSpec((tm, d), lambda i: (rb0 + i, 0)),
                  pl.BlockSpec((1, d), lambda i: (0, 0))],
        out_specs=pl.BlockSpec((tm, d), lambda i: (i, 0)),
        out_shape=jax.ShapeDtypeStruct((rows, d), out_dtype),
        compiler_params=_compiler_params(("parallel",)),
        name="rmsnorm_rows",
    )(x, g.reshape(1, d).astype(_F32))


def _matmul_kernel(*refs, n_pairs, n_extras, n_outs, n_k, head_major, out_head_w, aliased, epilogue):
    n_x = len(head_major)
    x_refs = refs[:n_x]
    w_refs = refs[n_x:n_x + n_pairs]
    e_refs = refs[n_x + n_pairs:n_x + n_pairs + n_extras]
    o_refs = refs[n_x + n_pairs + n_extras + aliased:n_x + n_pairs + n_extras + aliased + n_outs]
    acc_refs = refs[n_x + n_pairs + n_extras + aliased + n_outs:]

    def load_x(i):
        if head_major[i]:
            xr = x_refs[i]
            return jnp.concatenate([xr[h] for h in range(xr.shape[0])], axis=1)
        return x_refs[i][...]

    def partial(i):
        return jnp.dot(load_x(i if n_x > 1 else 0), w_refs[i][...], preferred_element_type=_F32)

    def finish(accs):
        outs = epilogue(accs, [e[...] for e in e_refs])
        for o_ref, val, hw in zip(o_refs, outs, out_head_w):
            if hw:
                for h in range(o_ref.shape[0]):
                    o_ref[h] = val[:, h * hw:(h + 1) * hw].astype(o_ref.dtype)
            else:
                o_ref[...] = val.astype(o_ref.dtype)

    if n_k == 1:
        finish([partial(i) for i in range(n_pairs)])
        return

    k = pl.program_id(2)

    @pl.when(k == 0)
    def _():
        for i in range(n_pairs):
            acc_refs[i][...] = partial(i)

    @pl.when(k > 0)
    def _():
        for i in range(n_pairs):
            acc_refs[i][...] += partial(i)

    @pl.when(k == n_k - 1)
    def _():
        finish([a[...] for a in acc_refs])


def _matmul(xs, ws, extras, epilogue, outs, *, tm, tn, tk=None, row0=0, rows=None, out_buf=None,
            name="matmul"):
    n_pairs = len(ws)
    assert len(xs) in (1, n_pairs)
    head_major = tuple(x.ndim == 3 for x in xs)
    m = xs[0].shape[1] if head_major[0] else xs[0].shape[0]
    rows = m if rows is None else rows
    kdim, n = ws[0].shape
    if tk is None:
        tk = kdim
    n_k = kdim // tk
    assert kdim % tk == 0 and rows % tm == 0 and row0 % tm == 0 and n % tn == 0
    n_j = n // tn
    rb0 = row0 // tm
    tks = [tk if n_k > 1 else w.shape[0] for w in ws]
    assert all(w.shape == (kdim, n) for w in ws) or n_k == 1

    in_specs = []
    for x, hm, tki in zip(xs, head_major, tks):
        if hm:
            assert n_k == 1
            in_specs.append(pl.BlockSpec((x.shape[0], tm, LANES), lambda i, j, k: (0, rb0 + i, 0)))
        else:
            in_specs.append(pl.BlockSpec((tm, tki), lambda i, j, k: (rb0 + i, k)))
    for w, tki in zip(ws, tks):
        in_specs.append(pl.BlockSpec((tki, tn), lambda i, j, k: (k, j)))
    for arr, cols, imap in extras:
        in_specs.append(pl.BlockSpec((tm, cols), lambda i, j, k, imap=imap: imap(i, j)))
    args = [*xs, *ws, *[e[0] for e in extras]]
    aliases = {}
    if out_buf is not None:
        assert len(outs) == 1
        in_specs.append(pl.BlockSpec(memory_space=pl.ANY))
        aliases = {len(args): 0}
        args.append(out_buf)

    out_specs, out_shapes, out_head_w = [], [], []
    for n_cols, dtype, hw in outs:
        to = n_cols // n_j
        out_head_w.append(hw)
        if hw:
            assert to % hw == 0
            out_specs.append(pl.BlockSpec((to // hw, tm, hw), lambda i, j, k: (j, rb0 + i, 0)))
            out_shapes.append(jax.ShapeDtypeStruct((n_cols // hw, m, hw), dtype))
        else:
            out_specs.append(pl.BlockSpec((tm, to), lambda i, j, k: (rb0 + i, j)))
            out_shapes.append(jax.ShapeDtypeStruct((m, n_cols), dtype))

    scratch = [pltpu.VMEM((tm, tn), _F32) for _ in range(n_pairs)] if n_k > 1 else []
    kern = functools.partial(
        _matmul_kernel, n_pairs=n_pairs, n_extras=len(extras), n_outs=len(outs), n_k=n_k,
        head_major=head_major, out_head_w=tuple(out_head_w), aliased=len(aliases), epilogue=epilogue)
    return pl.pallas_call(
        kern,
        grid=(rows // tm, n_j, n_k),
        in_specs=in_specs,
        out_specs=out_specs,
        out_shape=out_shapes,
        scratch_shapes=scratch,
        input_output_aliases=aliases,
        compiler_params=_compiler_params(("parallel", "parallel", "arbitrary")),
        name=name,
    )(*args)


def _shift_rows(z, prev_row):
    rolled = pltpu.roll(z, 1, 0)
    row = lax.broadcasted_iota(jnp.int32, z.shape, 0)
    return jnp.where(row == 0, prev_row, rolled)


def _cumsum_rows(x):
    n = x.shape[0]
    row = lax.broadcasted_iota(jnp.int32, x.shape, 0)
    s = 1
    while s < n:
        x = x + jnp.where(row >= s, pltpu.roll(x, s, 0), 0.0)
        s *= 2
    return x


def _dot(a, b):
    return jnp.dot(a.astype(_BF16), b.astype(_BF16), preferred_element_type=_F32)


def _dot_nt(a, b):
    return lax.dot_general(a.astype(_BF16), b.astype(_BF16), (((1,), (1,)), ((), ())),
                           preferred_element_type=_F32)


def _wkv_kernel(*refs, n_chunks, head, n_pp, aliased):
    (zr_ref, zk_ref, zv_ref, zl_ref, sprev_ref, lprev_ref, s0_ref,
     mu_ref, mul_ref, w0_ref, a0_ref, kkp_ref, kap_ref, rk_ref, lg_ref, lb_ref,
     w2_ref, a2_ref, g2_ref) = refs[:19]
    y_ref, sout_ref, st_ref, prev_ref, lprev_scr = refs[19 + aliased:]
    c_idx = pl.program_id(2)
    C = WKV_CHUNK
    P = 2 * head
    assert P == 2 * C
    tb = zr_ref.shape[0]
    n_sub = tb // C

    @pl.when(c_idx == 0)
    def _():
        st_ref[...] = s0_ref[0]
        prev_ref[...] = sprev_ref[0, 0]
        lprev_scr[...] = lprev_ref[0]

    lane = lax.broadcasted_iota(jnp.int32, (1, P), 1)
    m0 = (lane < head).astype(_F32)
    m1 = 1.0 - m0
    r2 = lax.broadcasted_iota(jnp.int32, (P, P), 0)
    c2 = lax.broadcasted_iota(jnp.int32, (P, P), 1)
    same = (r2 // C) == (c2 // C)
    tril_s = jnp.where(same & (c2 < r2), 1.0, 0.0)
    tril_i = jnp.where(same & (c2 <= r2), 1.0, 0.0)
    eye = jnp.where(r2 == c2, 1.0, 0.0)

    def stack(x):
        return jnp.concatenate([x * m0, x * m1], axis=0)

    def head_sum(x):
        s0 = jnp.sum(x * m0, axis=-1, keepdims=True)
        s1 = jnp.sum(x * m1, axis=-1, keepdims=True)
        return s0 * m0 + s1 * m1

    zl = zl_ref[...]
    xl = zl + (_shift_rows(zl, lprev_scr[...]) - zl) * mul_ref[...]
    lprev_scr[...] = zl[tb - 1:tb, :]
    nw = w2_ref.shape[0]
    na = a2_ref.shape[0]
    tw = jnp.tanh(xl[:, 0:nw]).astype(_BF16)
    xa = xl[:, nw:nw + na].astype(_BF16)
    sg = _sigmoid(xl[:, nw + na:]).astype(_BF16)

    pairs = []
    for pi in range(n_pp):
        ls = slice(pi * P, (pi + 1) * P)
        zr, zk, zv = zr_ref[:, ls], zk_ref[:, ls], zv_ref[:, ls]
        r = zr + (_shift_rows(zr, prev_ref[0:1, ls]) - zr) * mu_ref[0:1, ls]
        k = zk + (_shift_rows(zk, prev_ref[1:2, ls]) - zk) * mu_ref[1:2, ls]
        v = zv + (_shift_rows(zv, prev_ref[2:3, ls]) - zv) * mu_ref[2:3, ls]
        prev_ref[0:1, ls] = zr[tb - 1:tb, :]
        prev_ref[1:2, ls] = zk[tb - 1:tb, :]
        prev_ref[2:3, ls] = zv[tb - 1:tb, :]
        u = -(w0_ref[:, ls] + jnp.dot(tw, w2_ref[:, ls], preferred_element_type=_F32))
        softplus = jnp.maximum(u, 0.0) + jnp.log(1.0 + jnp.exp(-jnp.abs(u)))
        logw = -jnp.exp(-softplus - 0.5)
        a = _sigmoid(a0_ref[:, ls] + jnp.dot(xa, a2_ref[:, ls], preferred_element_type=_F32))
        g = jnp.dot(sg, g2_ref[:, ls], preferred_element_type=_F32)
        kk = k * kkp_ref[:, ls]
        kk = kk / jnp.maximum(jnp.sqrt(head_sum(kk * kk)), 1e-12)
        kh = k * (1.0 + (a - 1.0) * kap_ref[:, ls])
        pairs.append(dict(ls=ls, r=r, v=v, logw=logw, kk=kk, kh=kh, beta=kk * a, g=g,
                          bonus=head_sum(r * kh * rk_ref[:, ls]) * v))

    states = [st_ref[pi] for pi in range(n_pp)]
    group = n_sub
    for sc0 in range(0, n_sub, group):
        items = []
        for sc in range(sc0, sc0 + group):
            rs = slice(sc * C, (sc + 1) * C)
            for pi, pr in enumerate(pairs):
                lw = pr["logw"][rs]
                cum = _cumsum_rows(lw)
                cum_c = cum[C - 1:C, :]
                e_neg = jnp.exp(-cum)
                e_end = jnp.exp(cum_c - cum)
                items.append(dict(
                    pi=pi, rs=rs,
                    a2s=stack(-pr["kk"][rs] * jnp.exp(cum - lw)),
                    r2s=stack(pr["r"][rs] * jnp.exp(cum)),
                    b2s=stack(pr["beta"][rs] * e_neg),
                    k2s=stack(pr["kh"][rs] * e_neg),
                    v2s=stack(pr["v"][rs]),
                    bh2t=stack(pr["beta"][rs] * e_end).T,
                    kh2t=stack(pr["kh"][rs] * e_end).T,
                    dec_col=jnp.sum(eye * jnp.exp(cum_c), axis=-1, keepdims=True)))
        for it in items:
            nn = _dot_nt(jnp.concatenate([it["a2s"], it["r2s"]], axis=0),
                         jnp.concatenate([it["b2s"], it["k2s"]], axis=0))
            it["n_ba"] = nn[:P, :P] * tril_s
            it["n_ka"] = nn[:P, P:] * tril_s
            it["p_br"] = nn[P:, :P] * tril_i
            it["p_kr"] = nn[P:, P:] * tril_i
            it["t_inv"] = eye + it["n_ba"]
            it["pw"] = it["n_ba"]
        s = 2
        while s < C:
            for it in items:
                it["pw"] = _dot(it["pw"], it["pw"])
            for it in items:
                it["t_inv"] = it["t_inv"] + _dot(it["t_inv"], it["pw"])
            s *= 2
        for it in items:
            it["nkv"] = _dot(it["n_ka"], it["v2s"])
        for it in items:
            it["tt"] = _dot(it["t_inv"], jnp.concatenate([it["a2s"], it["nkv"]], axis=1))
        for it in items:
            it["mg"] = _dot(it["bh2t"], it["tt"])
        for it in items:
            it["pg"] = _dot(it["p_br"], it["tt"])
        for it in items:
            it["g_c"] = it["mg"][:, P:] + _dot(it["kh2t"], it["v2s"])
        for it in items:
            it["yg"] = it["pg"][:, P:] + _dot(it["p_kr"], it["v2s"])
        for it in items:
            pi, rs, pr = it["pi"], it["rs"], pairs[it["pi"]]
            ls = pr["ls"]
            st = states[pi]
            y2 = _dot(it["r2s"] + it["pg"][:, :P], st) + it["yg"]
            states[pi] = it["dec_col"] * st + _dot(it["mg"][:, :P], st) + it["g_c"]
            y = y2[0:C, :] + y2[C:2 * C, :]
            mean = head_sum(y) * (1.0 / head)
            yc = y - mean
            var = head_sum(yc * yc) * (1.0 / head)
            yn = yc * lax.rsqrt(var + GN_EPS) * lg_ref[:, ls] + lb_ref[:, ls]
            y_ref[rs, ls] = ((yn + pr["bonus"][rs]) * pr["g"][rs]).astype(y_ref.dtype)
    for pi in range(n_pp):
        st_ref[pi] = states[pi]

    @pl.when(c_idx == n_chunks - 1)
    def _():
        sout_ref[0] = st_ref[...]


def _wkv(z, shift_prev, s0_bd, params, *, batch, seq, row0, head, d_a, lora_sizes, m_out, out_buf):
    pw = 2 * head
    n_pairs = d_a // pw
    n_pp = min(WKV_PAIRS_PER_STEP, n_pairs)
    gw = n_pp * pw
    n_groups = n_pairs // n_pp
    lp = sum(lora_sizes)
    assert (3 * d_a) % lp == 0
    tb = _pick_tile(seq, 256, WKV_CHUNK)
    n_chunks = seq // tb
    assert row0 % tb == 0
    rb0 = row0 // tb
    rpb = seq // tb
    aliased = out_buf is not None

    def rows(b, c):
        return rb0 + b * rpb + c

    in_specs = [
        pl.BlockSpec((tb, gw), lambda b, p, c: (rows(b, c), p)),
        pl.BlockSpec((tb, gw), lambda b, p, c: (rows(b, c), n_groups + p)),
        pl.BlockSpec((tb, gw), lambda b, p, c: (rows(b, c), 2 * n_groups + p)),
        pl.BlockSpec((tb, lp), lambda b, p, c: (rows(b, c), 3 * d_a // lp)),
        pl.BlockSpec((1, 1, SUBLANES, gw), lambda b, p, c: (b, p, 0, 0)),
        pl.BlockSpec((1, 1, lp), lambda b, p, c: (b, 0, 0)),
        pl.BlockSpec((1, n_pp, pw, pw), lambda b, p, c: (b, p, 0, 0)),
        pl.BlockSpec((SUBLANES, gw), lambda b, p, c: (p, 0)),
        pl.BlockSpec((1, lp), lambda b, p, c: (0, 0)),
    ]
    for _ in range(7):
        in_specs.append(pl.BlockSpec((1, gw), lambda b, p, c: (0, p)))
    nw, na, ng = lora_sizes
    in_specs += [pl.BlockSpec((nw, gw), lambda b, p, c: (0, p)),
                 pl.BlockSpec((na, gw), lambda b, p, c: (0, p)),
                 pl.BlockSpec((ng, gw), lambda b, p, c: (0, p))]
    args = [z, z, z, z, shift_prev["rkv"], shift_prev["lora"], s0_bd,
            params["mu_rkv"], params["mu_lora"], params["w0"], params["a0"], params["k_k"],
            params["k_a"], params["r_k"], params["lnx_g"], params["lnx_b"],
            params["w2"], params["a2"], params["g2"]]
    aliases = {}
    if aliased:
        in_specs.append(pl.BlockSpec(memory_space=pl.ANY))
        aliases = {len(args): 0}
        args.append(out_buf)
    out_specs = [pl.BlockSpec((tb, gw), lambda b, p, c: (rows(b, c), p)),
                 pl.BlockSpec((1, n_pp, pw, pw), lambda b, p, c: (b, p, 0, 0))]
    out_shape = [jax.ShapeDtypeStruct((m_out, d_a), _BF16),
                 jax.ShapeDtypeStruct((batch, n_pairs, pw, pw), _F32)]
    kern = functools.partial(_wkv_kernel, n_chunks=n_chunks, head=head, n_pp=n_pp, aliased=int(aliased))
    return pl.pallas_call(
        kern,
        grid=(batch, n_groups, n_chunks),
        in_specs=in_specs,
        out_specs=out_specs,
        out_shape=out_shape,
        scratch_shapes=[pltpu.VMEM((n_pp, pw, pw), _F32),
                        pltpu.VMEM((SUBLANES, gw), _F32),
                        pltpu.VMEM((1, lp), _F32)],
        input_output_aliases=aliases,
        compiler_params=_compiler_params(("parallel", "parallel", "arbitrary")),
        name="wkv7_chunked",
    )(*args)


def _dsa_kernel(*refs, tq, l_ext, l_true, q_pos0, topk, idx_scale, n_kv, dh, d_idx, aliased):
    q_ref, k_ref, v_ref, qi_ref, wi_ref, ki_ref = refs[:6]
    o_ref, kbf, vbf, kibf, wib, isc_scr, key_scr, dm_scr = refs[6 + aliased:]
    qb = pl.program_id(1)
    n_q = q_ref.shape[0]
    n_i = qi_ref.shape[0]
    rep = n_q // n_kv

    @pl.when(qb == 0)
    def _():
        for g in range(n_kv):
            kbf[g] = k_ref[0:l_ext, g * dh:(g + 1) * dh].astype(_BF16)
            vbf[g] = v_ref[0:l_ext, g * dh:(g + 1) * dh].astype(_BF16)
        kibf[...] = ki_ref[0:l_ext, 0:d_idx].astype(_BF16)

    wi = wi_ref[...]
    for hi in range(n_i):
        wib[hi] = jnp.broadcast_to(wi[:, hi:hi + 1], (tq, LANES))
    isc_scr[...] = jnp.zeros((tq, l_ext), _F32)
    ki = kibf[...]

    hs = 4 if n_i % 4 == 0 else 1

    def idx_body(i, carry):
        qs = qi_ref[pl.ds(i * hs, hs)].reshape(hs * tq, d_idx)
        sc = lax.dot_general(qs, ki, (((1,), (1,)), ((), ())), preferred_element_type=_F32)
        acc = isc_scr[...]
        for j in range(hs):
            acc = acc + (jnp.maximum(sc[j * tq:(j + 1) * tq], 0.0)
                         * jnp.tile(wib[i * hs + j], (1, l_ext // LANES)))
        isc_scr[...] = acc
        return carry

    lax.fori_loop(0, n_i // hs, idx_body, 0)

    q_pos = q_pos0 + qb * tq + lax.broadcasted_iota(jnp.int32, (tq, l_ext), 0)
    k_pos = lax.broadcasted_iota(jnp.int32, (tq, l_ext), 1)
    cs = CHUNK.bit_length() - 1
    assert 1 << cs == CHUNK
    allowed = ((k_pos >> cs) <= (q_pos >> cs)) & (k_pos < l_true)
    bits = pltpu.bitcast(isc_scr[...] * idx_scale, jnp.int32)
    int_min = jnp.int32(-2 ** 31)
    key_scr[...] = jnp.where(allowed, jnp.where(bits < 0, bits ^ jnp.int32(0x7FFFFFFF), bits), int_min)

    def count_ge(cand):
        return jnp.sum(jnp.where(key_scr[...] >= cand, 1.0, 0.0), axis=-1, keepdims=True)

    kf = jnp.float32(topk)
    zero = jnp.zeros((tq, 1), jnp.int32)
    thr = jnp.where(count_ge(zero) >= kf, zero, jnp.full((tq, 1), int_min, jnp.int32))

    def thr_body(i, thr):
        cand = thr + (jnp.int32(1) << (jnp.int32(30) - i))
        return jnp.where(count_ge(cand) >= kf, cand, thr)

    thr = lax.fori_loop(0, 31, thr_body, thr)
    key = key_scr[...]
    n_gt = jnp.sum(jnp.where(key > thr, 1.0, 0.0), axis=-1, keepdims=True)
    n_eq = jnp.sum(jnp.where(key == thr, 1.0, 0.0), axis=-1, keepdims=True)
    need = kf - n_gt
    n_bits = max(1, (l_ext - 1).bit_length())
    full = jnp.full((tq, 1), 1 << n_bits, jnp.int32)

    def tie_bound():
        def count_tie_below(bound):
            hit = jnp.where(key_scr[...] == thr, jnp.where(k_pos < bound, 1.0, 0.0), 0.0)
            return jnp.sum(hit, axis=-1, keepdims=True)

        def tbody(i, bound):
            cand = bound + (jnp.int32(1) << (jnp.int32(n_bits - 1) - i))
            return jnp.where(count_tie_below(cand) <= need, cand, bound)

        return lax.fori_loop(0, n_bits, tbody, zero)

    has_excess_ties = jnp.max(n_eq - need) > 0.0
    bound = lax.cond(has_excess_ties, tie_bound, lambda: full)
    sel = allowed & ((key > thr) | ((key == thr) & (k_pos < bound)))
    dm_scr[...] = jnp.where(sel, jnp.abs(q_pos - k_pos).astype(_F32), DIST_EXCLUDED)

    log2 = 0.6931471805599453

    hu = 2 if rep % 2 == 0 else 1

    def head_body(i, carry):
        g = (i * hu) // rep
        kg, vg = kbf[g], vbf[g]
        dm = dm_scr[...]
        ss = [lax.dot_general(q_ref[i * hu + j], kg, (((1,), (1,)), ((), ())),
                              preferred_element_type=_F32) for j in range(hu)]
        for j in range(hu):
            h = i * hu + j
            slope = jnp.exp(jnp.full((1, 1), h + 1, jnp.int32).astype(_F32)
                            * (-8.0 / n_q * log2))
            s = ss[j] - slope * dm
            m = jnp.max(s, axis=-1, keepdims=True)
            p = jnp.exp(s - m)
            l = jnp.sum(p, axis=-1, keepdims=True)
            o = jnp.dot(p.astype(_BF16), vg, preferred_element_type=_F32)
            o_ref[h] = (o / l).astype(o_ref.dtype)
        return carry

    lax.fori_loop(0, n_q // hu, head_body, 0)


def _dsa(q_hm, qi_hm, kv_src, k_col, v_col, ki_src, ki_col, wi_src, wi_col, *, batch, n_qb, tq,
         q_row0, q_stride, kv_rows, kv_stride, l_ext, l_true, q_pos0, topk, n_kv, m_out, out_buf):
    n_q, _, dh = q_hm.shape
    n_i, _, d_idx = qi_hm.shape
    assert q_row0 % tq == 0 and q_stride % tq == 0 and l_ext % LANES == 0 and l_ext <= kv_rows
    qr0, qst = q_row0 // tq, q_stride // tq
    aliased = out_buf is not None
    kern = functools.partial(
        _dsa_kernel, tq=tq, l_ext=l_ext, l_true=l_true, q_pos0=q_pos0, topk=topk,
        idx_scale=float(n_i) ** -0.5 * float(d_idx) ** -0.5, n_kv=n_kv, dh=dh, d_idx=d_idx,
        aliased=int(aliased))
    qrow = lambda b, i: qr0 + b * qst + i
    in_specs = [
        pl.BlockSpec((n_q, tq, dh), lambda b, i: (0, qrow(b, i), 0)),
        pl.BlockSpec((kv_rows, n_kv * dh), lambda b, i: (b * kv_stride, k_col)),
        pl.BlockSpec((kv_rows, n_kv * dh), lambda b, i: (b * kv_stride, v_col)),
        pl.BlockSpec((n_i, tq, d_idx), lambda b, i: (0, qrow(b, i), 0)),
        pl.BlockSpec((tq, LANES), lambda b, i: (qrow(b, i), wi_col)),
        pl.BlockSpec((kv_rows, LANES), lambda b, i: (b * kv_stride, ki_col)),
    ]
    args = [q_hm, kv_src, kv_src, qi_hm, wi_src, ki_src]
    aliases = {}
    if aliased:
        in_specs.append(pl.BlockSpec(memory_space=pl.ANY))
        aliases = {len(args): 0}
        args.append(out_buf)
    return pl.pallas_call(
        kern,
        grid=(batch, n_qb),
        in_specs=in_specs,
        out_specs=pl.BlockSpec((n_q, tq, dh), lambda b, i: (0, qrow(b, i), 0)),
        out_shape=jax.ShapeDtypeStruct((n_q, m_out, dh), _BF16),
        scratch_shapes=[pltpu.VMEM((n_kv, l_ext, dh), _BF16), pltpu.VMEM((n_kv, l_ext, dh), _BF16),
                        pltpu.VMEM((l_ext, d_idx), _BF16), pltpu.VMEM((n_i, tq, LANES), _F32),
                        pltpu.VMEM((tq, l_ext), _F32), pltpu.VMEM((tq, l_ext), jnp.int32),
                        pltpu.VMEM((tq, l_ext), _F32)],
        input_output_aliases=aliases,
        compiler_params=_compiler_params(("parallel", "arbitrary")),
        name="dsa_attention",
    )(*args)


def _pad_rows(w, rows):
    return jnp.pad(w, ((0, rows - w.shape[0]), (0, 0)))


def _pad_cols(w, cols):
    return jnp.pad(w, ((0, 0), (0, cols - w.shape[1])))


def _to_block_diag_t(s, head):
    b, h = s.shape[:2]
    st = jnp.swapaxes(s, -1, -2).reshape(b, h // 2, 2, head, head).astype(_F32)
    z = jnp.zeros_like(st[:, :, 0])
    top = jnp.concatenate([st[:, :, 0], z], axis=-1)
    bot = jnp.concatenate([z, st[:, :, 1]], axis=-1)
    return jnp.concatenate([top, bot], axis=-2)


def _from_block_diag_t(sbd, head):
    b, hp = sbd.shape[:2]
    s0 = sbd[:, :, :head, :head]
    s1 = sbd[:, :, head:, head:]
    st = jnp.stack([s0, s1], axis=2).reshape(b, hp * 2, head, head)
    return jnp.swapaxes(st, -1, -2)


def kernel(x_prompt, x_sample, cache_k, cache_v, cache_kidx, state_shift, state_wkv, norm1_g, w_in, mu_shift, w0, w2, a0, a2, g2, k_k, k_a, r_k, lnx_g, lnx_b, w_br_a, w_br_b, w_out, norm2_g, w_ffn_gate, w_ffn_up, w_ffn_down, norm_f_g):
    depth = w_in.shape[0]
    assert depth == 1
    bp, tp, d_model = x_prompt.shape
    bs, ts, _ = x_sample.shape
    _, _, past, n_kv, dh = cache_k.shape
    d_idx = cache_kidx.shape[-1]
    d_shift = state_shift.shape[-1]
    _, _, h_a, head, _ = state_wkv.shape
    d_a = h_a * head
    r_w, r_a, r_g = w2.shape[1], a2.shape[1], g2.shape[1]
    n_q = w_br_b.shape[1] // dh
    d_in = w_in.shape[-1]
    h_i = (d_in - d_shift - n_q * dh - 2 * n_kv * dh - d_idx - 2 * d_model) // (d_idx + 1)
    d_ff = w_ffn_gate.shape[-1]
    kvw = n_kv * dh
    l = 0
    mp, ms = bp * tp, bs * ts
    m_tot = mp + ms
    xp2, xs2 = x_prompt.reshape(mp, d_model), x_sample.reshape(ms, d_model)
    tm_mm = _pick_tile(m_tot, 1024, 2 * SUBLANES)

    win = w_in[l]
    o = 0
    def take(n):
        nonlocal o
        blk = win[:, o:o + n]
        o += n
        return blk
    nwp, nap, ngp = _round_up(r_w, LANES), _round_up(r_a, LANES), _round_up(r_g, LANES)
    lp = nwp + nap + ngp
    w_rkv, w_lw, w_la, w_lg = take(3 * d_a), take(r_w), take(r_a), take(r_g)
    w_rkvl = jnp.concatenate([w_rkv, _pad_cols(w_lw, nwp), _pad_cols(w_la, nap), _pad_cols(w_lg, ngp)],
                             axis=1).astype(_BF16)
    w_q = take(n_q * dh).astype(_BF16)
    w_k, w_v = take(kvw), take(kvw)
    w_qi = take(h_i * d_idx).astype(_BF16)
    w_ki, w_wi = take(d_idx), take(h_i)
    w_kv = jnp.concatenate([w_k, w_v, _pad_cols(w_ki, LANES), _pad_cols(w_wi, LANES)], axis=1).astype(_BF16)
    w_gates = jnp.concatenate([take(d_model), take(d_model)], axis=1).astype(_BF16)

    h1 = _rmsnorm_cat(xp2, xs2, norm1_g[l], _BF16)
    ident = lambda accs, ex: [accs[0]]
    n_rkvl = 3 * d_a + lp
    (z_a,) = _matmul([h1], [w_rkvl], [], ident, [(n_rkvl, _F32, None)],
                     tm=tm_mm, tn=_pick_tile(n_rkvl, 1024, LANES), name="proj_rkv_lora")
    n_kvz = 2 * kvw + 2 * LANES
    (z_kv,) = _matmul([h1], [w_kv], [], ident, [(n_kvz, _F32, None)],
                      tm=tm_mm, tn=_pick_tile(n_kvz, 1280, LANES), name="proj_kv_idx")
    q_scale = float(dh) ** -0.5
    (q_hm,) = _matmul([h1], [w_q], [], lambda accs, ex: [accs[0] * q_scale],
                      [(n_q * dh, _BF16, dh)], tm=tm_mm, tn=_pick_tile(n_q * dh, 1024, dh), name="proj_q")
    (qi_hm,) = _matmul([h1], [w_qi], [], ident, [(h_i * d_idx, _BF16, d_idx)],
                       tm=tm_mm, tn=_pick_tile(h_i * d_idx, 1024, LANES), name="proj_qi")
    (gates,) = _matmul([h1], [w_gates], [], ident, [(2 * d_model, _BF16, None)],
                       tm=tm_mm, tn=_pick_tile(2 * d_model, 1024, LANES), name="proj_gates")
    kx, vx = z_kv[:, :kvw], z_kv[:, kvw:2 * kvw]
    ki_new = z_kv[:, 2 * kvw:2 * kvw + d_idx]

    mu = mu_shift[l]
    def lora_row(vec):
        return jnp.concatenate([
            jnp.pad(vec[3 * d_a:3 * d_a + r_w], (0, nwp - r_w)),
            jnp.pad(vec[3 * d_a + r_w:3 * d_a + r_w + r_a], (0, nap - r_a)),
            jnp.pad(vec[3 * d_a + r_w + r_a:], (0, ngp - r_g))])
    n_pairs = d_a // (2 * head)
    gw = min(WKV_PAIRS_PER_STEP, n_pairs) * 2 * head
    n_groups = d_a // gw
    def rkv_rows(vec):
        lead = vec.shape[:-1]
        x = vec[..., :3 * d_a].reshape(lead + (3, n_groups, gw))
        x = jnp.moveaxis(x, -3, -2)
        return jnp.pad(x, [(0, 0)] * (len(lead) + 1) + [(0, SUBLANES - 3), (0, 0)])
    row = lambda v: v[l].reshape(1, d_a).astype(_F32)
    params = dict(
        mu_rkv=rkv_rows(mu).reshape(n_groups * SUBLANES, gw),
        mu_lora=lora_row(mu).reshape(1, lp),
        w0=row(w0), a0=row(a0), k_k=row(k_k), k_a=row(k_a),
        r_k=r_k[l].reshape(1, d_a).astype(_F32), lnx_g=row(lnx_g), lnx_b=row(lnx_b),
        w2=_pad_rows(w2[l], nwp).astype(_BF16), a2=_pad_rows(a2[l], nap).astype(_BF16),
        g2=_pad_rows(g2[l], ngp).astype(_BF16))

    def shift_state(s):
        s = s[:, 0]
        return dict(rkv=rkv_rows(s), lora=jax.vmap(lora_row)(s).reshape(-1, 1, lp))

    wkv_kw = dict(head=head, d_a=d_a, lora_sizes=(nwp, nap, ngp), m_out=m_tot)
    y_a, sp_bd = _wkv(z_a, shift_state(jnp.zeros((bp, 1, d_shift), _F32)),
                      jnp.zeros((bp, n_pairs, 2 * head, 2 * head), _F32), params,
                      batch=bp, seq=tp, row0=0, out_buf=None, **wkv_kw)
    y_a, ss_bd = _wkv(z_a, shift_state(state_shift[l]), _to_block_diag_t(state_wkv[l], head), params,
                      batch=bs, seq=ts, row0=mp, out_buf=y_a, **wkv_kw)
    wkv_p = _from_block_diag_t(sp_bd, head)
    wkv_s = _from_block_diag_t(ss_bd, head)

    topk_p = min(TOPK_MAX, tp // 4)
    tq_p = _pick_tile(tp, 128, CHUNK)
    n_grp = max(1, min(DSA_GROUPS, tp // tq_p))
    grp = tp // n_grp
    wi_col = (2 * kvw + LANES) // LANES
    ki_col = 2 * kvw // LANES
    y_b = None
    for gi in range(n_grp):
        l_ext = (gi + 1) * grp
        kv_rows = l_ext
        while tp % kv_rows:
            kv_rows += grp
        y_b = _dsa(q_hm, qi_hm, z_kv, 0, 1, z_kv, ki_col, z_kv, wi_col,
                   batch=bp, n_qb=grp // tq_p, tq=tq_p, q_row0=gi * grp, q_stride=tp,
                   kv_rows=kv_rows, kv_stride=tp // kv_rows, l_ext=l_ext, l_true=l_ext,
                   q_pos0=gi * grp, topk=topk_p, n_kv=n_kv, m_out=m_tot, out_buf=y_b)
    ls = past + ts
    lps = _round_up(ls, LANES)
    def with_cache(cache, new, width):
        x = jnp.concatenate([cache.reshape(bs, past, width), new.reshape(bs, ts, width)], axis=1)
        return jnp.pad(x, ((0, 0), (0, lps - ls), (0, 0))).reshape(bs * lps, width)
    kv_s = jnp.concatenate([with_cache(cache_k[l], kx[mp:], kvw), with_cache(cache_v[l], vx[mp:], kvw)], axis=1)
    ki_s = _pad_cols(with_cache(cache_kidx[l], ki_new[mp:], d_idx), LANES)
    y_b = _dsa(q_hm, qi_hm, kv_s, 0, 1, ki_s, 0, z_kv, wi_col,
               batch=bs, n_qb=1, tq=ts, q_row0=mp, q_stride=ts, kv_rows=lps, kv_stride=1, l_ext=lps, l_true=ls,
               q_pos0=past, topk=min(TOPK_MAX, ls // 4), n_kv=n_kv, m_out=m_tot, out_buf=y_b)

    tn_mg = _pick_tile(d_model, 1024, LANES)
    nj_mg = d_model // tn_mg
    def merge_ep(accs, ex):
        return [_sigmoid(ex[0].astype(_F32)) * accs[0] + _sigmoid(ex[1].astype(_F32)) * accs[1]]
    (merged,) = _matmul([y_a, y_b], [w_br_a[l].astype(_BF16), w_br_b[l].astype(_BF16)],
                        [(gates, tn_mg, lambda i, j: (i, j)), (gates, tn_mg, lambda i, j: (i, nj_mg + j))],
                        merge_ep, [(d_model, _BF16, None)], tm=tm_mm, tn=tn_mg, name="branch_merge")
    tn_res = _pick_tile(d_model, 1024, LANES)
    w_out_b = w_out[l].astype(_BF16)
    resid_ep = lambda accs, ex: [ex[0] + accs[0]]
    tm_p = _pick_tile(mp, 1024, 2 * SUBLANES)
    tm_s = _pick_tile(math.gcd(mp, ms), 1024, 2 * SUBLANES)
    (x1,) = _matmul([merged], [w_out_b], [(xp2, tn_res, lambda i, j: (i, j))], resid_ep,
                    [(d_model, _F32, None)], tm=tm_p, tn=tn_res, row0=0, rows=mp, name="out_proj")
    (x1,) = _matmul([merged], [w_out_b], [(xs2, tn_res, lambda i, j: (i, j))], resid_ep,
                    [(d_model, _F32, None)], tm=tm_s, tn=tn_res, row0=mp, rows=ms, out_buf=x1,
                    name="out_proj")

    h2 = _rmsnorm_rows(x1, norm2_g[l], _BF16, 0, m_tot)
    tn_ff = 512
    d_ffp = _round_up(d_ff, 2 * tn_ff)
    wg = _pad_cols(w_ffn_gate[l].astype(_BF16), d_ffp)
    wu = _pad_cols(w_ffn_up[l].astype(_BF16), d_ffp)
    def swiglu_ep(accs, ex):
        return [accs[0] * _sigmoid(accs[0]) * accs[1]]
    (u,) = _matmul([h2], [wg, wu], [], swiglu_ep, [(d_ffp, _BF16, None)], tm=tm_mm, tn=tn_ff,
                   name="ffn_up")
    wd = jnp.zeros((d_ffp, d_model), _BF16).at[:d_ff].set(w_ffn_down[l].astype(_BF16))
    tn_dn = _pick_tile(d_model, 1024, LANES)
    tk_dn = _pick_tile(d_ffp, 2816, LANES)
    (x2,) = _matmul([u], [wd], [(x1, tn_dn, lambda i, j: (i, j))], resid_ep,
                    [(d_model, _F32, None)], tm=tm_mm, tn=tn_dn, tk=tk_dn, name="ffn_down")
    y_p = _rmsnorm_rows(x2, norm_f_g, _F32, 0, mp).reshape(bp, tp, d_model)
    y_s = _rmsnorm_rows(x2, norm_f_g, _F32, mp, ms).reshape(bs, ts, d_model)

    kx4 = lambda x, b, t: x.reshape(1, b, t, n_kv, dh)
    def zsh_last(rows):
        zl = z_a[rows]
        return jnp.concatenate([zl[:, :3 * d_a + r_w], zl[:, 3 * d_a + nwp:3 * d_a + nwp + r_a],
                                zl[:, 3 * d_a + nwp + nap:3 * d_a + nwp + nap + r_g]], axis=-1)
    last_p = jnp.arange(bp) * tp + (tp - 1)
    last_s = mp + jnp.arange(bs) * ts + (ts - 1)
    return (y_p, y_s,
            kx4(kx[:mp], bp, tp), kx4(vx[:mp], bp, tp), ki_new[:mp].reshape(1, bp, tp, d_idx),
            zsh_last(last_p).reshape(1, bp, 1, d_shift), wkv_p[None],
            kx4(kx[mp:], bs, ts), kx4(vx[mp:], bs, ts), ki_new[mp:].reshape(1, bs, ts, d_idx),
            zsh_last(last_s).reshape(1, bs, 1, d_shift), wkv_s[None])
```

```python
import functools
import math

import jax
import jax.numpy as jnp
from jax import lax
from jax.experimental import pallas as pl
from jax.experimental.pallas import tpu as pltpu

CHUNK = 64
NORM_EPS = 1e-6
GN_EPS = 64e-5
TOPK_MAX = 256

LANES = 128
SUBLANES = 8
V7X_VMEM_LIMIT_BYTES = 56 * 1024 * 1024

WKV_CHUNK = 64
WKV_PAIRS_PER_STEP = 4
DIST_EXCLUDED = 1e30
DSA_GROUPS = 16
LOG2_E = 1.4426950408889634

_F32 = jnp.float32
_BF16 = jnp.bfloat16


def _round_up(n, m):
    return (n + m - 1) // m * m


def _pick_tile(n, pref, align):
    if n <= pref:
        return n
    t = pref // align * align
    while t >= align:
        if n % t == 0:
            return t
        t -= align
    return n


def _compiler_params(semantics):
    return pltpu.CompilerParams(dimension_semantics=semantics,
                                vmem_limit_bytes=V7X_VMEM_LIMIT_BYTES)


def _sigmoid(x):
    return 1.0 / (1.0 + jnp.exp(-x))


def _rms(x, g):
    x = x.astype(_F32)
    ms = jnp.mean(x * x, axis=-1, keepdims=True)
    return x * lax.rsqrt(ms + NORM_EPS) * g


def _rmsnorm_cat_kernel(xa_ref, xb_ref, g_ref, o_ref, *, n_a):
    i = pl.program_id(0)

    @pl.when(i < n_a)
    def _():
        o_ref[...] = _rms(xa_ref[...], g_ref[...]).astype(o_ref.dtype)

    @pl.when(i >= n_a)
    def _():
        o_ref[...] = _rms(xb_ref[...], g_ref[...]).astype(o_ref.dtype)


def _rmsnorm_cat(xa, xb, g, out_dtype):
    (ma, d), mb = xa.shape, xb.shape[0]
    tm = _pick_tile(math.gcd(ma, mb), 512, SUBLANES)
    n_a = ma // tm
    return pl.pallas_call(
        functools.partial(_rmsnorm_cat_kernel, n_a=n_a),
        grid=((ma + mb) // tm,),
        in_specs=[pl.BlockSpec((tm, d), lambda i: (jnp.minimum(i, n_a - 1), 0)),
                  pl.BlockSpec((tm, d), lambda i: (jnp.maximum(i - n_a, 0), 0)),
                  pl.BlockSpec((1, d), lambda i: (0, 0))],
        out_specs=pl.BlockSpec((tm, d), lambda i: (i, 0)),
        out_shape=jax.ShapeDtypeStruct((ma + mb, d), out_dtype),
        compiler_params=_compiler_params(("arbitrary",)),
        name="rmsnorm_cat",
    )(xa, xb, g.reshape(1, d).astype(_F32))


def _rmsnorm_rows_kernel(x_ref, g_ref, o_ref):
    o_ref[...] = _rms(x_ref[...], g_ref[...]).astype(o_ref.dtype)


def _rmsnorm_rows(x, g, out_dtype, row0, rows):
    d = x.shape[1]
    tm = _pick_tile(math.gcd(row0, rows) if row0 else rows, 512, SUBLANES)
    rb0 = row0 // tm
    return pl.pallas_call(
        _rmsnorm_rows_kernel,
        grid=(rows // tm,),
        in_specs=[pl.BlockSpec((tm, d), lambda i: (rb0 + i, 0)),
                  pl.BlockSpec((1, d), lambda i: (0, 0))],
        out_specs=pl.BlockSpec((tm, d), lambda i: (i, 0)),
        out_shape=jax.ShapeDtypeStruct((rows, d), out_dtype),
        compiler_params=_compiler_params(("parallel",)),
        name="rmsnorm_rows",
    )(x, g.reshape(1, d).astype(_F32))


def _matmul_kernel(*refs, n_pairs, n_extras, n_outs, n_k, head_major, out_head_w, aliased, epilogue):
    n_x = len(head_major)
    x_refs = refs[:n_x]
    w_refs = refs[n_x:n_x + n_pairs]
    e_refs = refs[n_x + n_pairs:n_x + n_pairs + n_extras]
    o_refs = refs[n_x + n_pairs + n_extras + aliased:n_x + n_pairs + n_extras + aliased + n_outs]
    acc_refs = refs[n_x + n_pairs + n_extras + aliased + n_outs:]

    def load_x(i):
        if head_major[i]:
            xr = x_refs[i]
            return jnp.concatenate([xr[h] for h in range(xr.shape[0])], axis=1)
        return x_refs[i][...]

    def partial(i):
        return jnp.dot(load_x(i if n_x > 1 else 0), w_refs[i][...], preferred_element_type=_F32)

    def finish(accs):
        outs = epilogue(accs, [e[...] for e in e_refs])
        for o_ref, val, hw in zip(o_refs, outs, out_head_w):
            if hw:
                for h in range(o_ref.shape[0]):
                    o_ref[h] = val[:, h * hw:(h + 1) * hw].astype(o_ref.dtype)
            else:
                o_ref[...] = val.astype(o_ref.dtype)

    if n_k == 1:
        finish([partial(i) for i in range(n_pairs)])
        return

    k = pl.program_id(2)

    @pl.when(k == 0)
    def _():
        for i in range(n_pairs):
            acc_refs[i][...] = partial(i)

    @pl.when(k > 0)
    def _():
        for i in range(n_pairs):
            acc_refs[i][...] += partial(i)

    @pl.when(k == n_k - 1)
    def _():
        finish([a[...] for a in acc_refs])


def _matmul(xs, ws, extras, epilogue, outs, *, tm, tn, tk=None, row0=0, rows=None, out_buf=None,
            name="matmul"):
    n_pairs = len(ws)
    assert len(xs) in (1, n_pairs)
    head_major = tuple(x.ndim == 3 for x in xs)
    m = xs[0].shape[1] if head_major[0] else xs[0].shape[0]
    rows = m if rows is None else rows
    kdim, n = ws[0].shape
    if tk is None:
        tk = kdim
    n_k = kdim // tk
    assert kdim % tk == 0 and rows % tm == 0 and row0 % tm == 0 and n % tn == 0
    n_j = n // tn
    rb0 = row0 // tm
    tks = [tk if n_k > 1 else w.shape[0] for w in ws]
    assert all(w.shape == (kdim, n) for w in ws) or n_k == 1

    in_specs = []
    for x, hm, tki in zip(xs, head_major, tks):
        if hm:
            assert n_k == 1
            in_specs.append(pl.BlockSpec((x.shape[0], tm, LANES), lambda i, j, k: (0, rb0 + i, 0)))
        else:
            in_specs.append(pl.BlockSpec((tm, tki), lambda i, j, k: (rb0 + i, k)))
    for w, tki in zip(ws, tks):
        in_specs.append(pl.BlockSpec((tki, tn), lambda i, j, k: (k, j)))
    for arr, cols, imap in extras:
        in_specs.append(pl.BlockSpec((tm, cols), lambda i, j, k, imap=imap: imap(i, j)))
    args = [*xs, *ws, *[e[0] for e in extras]]
    aliases = {}
    if out_buf is not None:
        assert len(outs) == 1
        in_specs.append(pl.BlockSpec(memory_space=pl.ANY))
        aliases = {len(args): 0}
        args.append(out_buf)

    out_specs, out_shapes, out_head_w = [], [], []
    for n_cols, dtype, hw in outs:
        to = n_cols // n_j
        out_head_w.append(hw)
        if hw:
            assert to % hw == 0
            out_specs.append(pl.BlockSpec((to // hw, tm, hw), lambda i, j, k: (j, rb0 + i, 0)))
            out_shapes.append(jax.ShapeDtypeStruct((n_cols // hw, m, hw), dtype))
        else:
            out_specs.append(pl.BlockSpec((tm, to), lambda i, j, k: (rb0 + i, j)))
            out_shapes.append(jax.ShapeDtypeStruct((m, n_cols), dtype))

    scratch = [pltpu.VMEM((tm, tn), _F32) for _ in range(n_pairs)] if n_k > 1 else []
    kern = functools.partial(
        _matmul_kernel, n_pairs=n_pairs, n_extras=len(extras), n_outs=len(outs), n_k=n_k,
        head_major=head_major, out_head_w=tuple(out_head_w), aliased=len(aliases), epilogue=epilogue)
    return pl.pallas_call(
        kern,
        grid=(rows // tm, n_j, n_k),
        in_specs=in_specs,
        out_specs=out_specs,
        out_shape=out_shapes,
        scratch_shapes=scratch,
        input_output_aliases=aliases,
        compiler_params=_compiler_params(("parallel", "parallel", "arbitrary")),
        name=name,
    )(*args)


def _shift_rows(z, prev_row):
    rolled = pltpu.roll(z, 1, 0)
    row = lax.broadcasted_iota(jnp.int32, z.shape, 0)
    return jnp.where(row == 0, prev_row, rolled)


def _cumsum_rows(x):
    n = x.shape[0]
    row = lax.broadcasted_iota(jnp.int32, x.shape, 0)
    s = 1
    while s < n:
        x = x + jnp.where(row >= s, pltpu.roll(x, s, 0), 0.0)
        s *= 2
    return x


def bf(x):
    return x.astype(_BF16)


def _bdot(a, b):
    assert a.dtype == _BF16 and b.dtype == _BF16
    return jnp.dot(a, b, preferred_element_type=_F32)


def _bdot_nt(a, b):
    assert a.dtype == _BF16 and b.dtype == _BF16
    return lax.dot_general(a, b, (((1,), (1,)), ((), ())), preferred_element_type=_F32)


def _wkv_kernel(*refs, n_chunks, head, n_pp, aliased):
    (zr_ref, zk_ref, zv_ref, zl_ref, sprev_ref, lprev_ref, s0_ref,
     mu_ref, mul_ref, w0_ref, a0_ref, kkp_ref, kap_ref, rk_ref, lg_ref, lb_ref,
     w2_ref, a2_ref, g2_ref) = refs[:19]
    y_ref, sout_ref, st_ref, prev_ref, lprev_scr = refs[19 + aliased:]
    c_idx = pl.program_id(2)
    C = WKV_CHUNK
    P = 2 * head
    assert P == 2 * C
    tb = zr_ref.shape[0]
    n_sub = tb // C

    @pl.when(c_idx == 0)
    def _():
        st_ref[...] = s0_ref[0]
        prev_ref[...] = sprev_ref[0, 0]
        lprev_scr[...] = lprev_ref[0]

    lane = lax.broadcasted_iota(jnp.int32, (1, P), 1)
    m0 = (lane < head).astype(_F32)
    m1 = 1.0 - m0
    r2 = lax.broadcasted_iota(jnp.int32, (P, P), 0)
    c2 = lax.broadcasted_iota(jnp.int32, (P, P), 1)
    same = (r2 // C) == (c2 // C)
    tril_s = jnp.where(same & (c2 < r2), 1.0, 0.0)
    tril_i = jnp.where(same & (c2 <= r2), 1.0, 0.0)
    eye = jnp.where(r2 == c2, 1.0, 0.0)

    def stack(x):
        return jnp.concatenate([x * m0, x * m1], axis=0)

    def head_sum(x):
        s0 = jnp.sum(x * m0, axis=-1, keepdims=True)
        s1 = jnp.sum(x * m1, axis=-1, keepdims=True)
        return s0 * m0 + s1 * m1

    zl = zl_ref[...]
    xl = zl + (_shift_rows(zl, lprev_scr[...]) - zl) * mul_ref[...]
    lprev_scr[...] = zl[tb - 1:tb, :]
    nw = w2_ref.shape[0]
    na = a2_ref.shape[0]
    tw = jnp.tanh(xl[:, 0:nw]).astype(_BF16)
    xa = xl[:, nw:nw + na].astype(_BF16)
    sg = _sigmoid(xl[:, nw + na:]).astype(_BF16)

    pairs = []
    for pi in range(n_pp):
        ls = slice(pi * P, (pi + 1) * P)
        zr, zk, zv = zr_ref[:, ls], zk_ref[:, ls], zv_ref[:, ls]
        r = zr + (_shift_rows(zr, prev_ref[0:1, ls]) - zr) * mu_ref[0:1, ls]
        k = zk + (_shift_rows(zk, prev_ref[1:2, ls]) - zk) * mu_ref[1:2, ls]
        v = zv + (_shift_rows(zv, prev_ref[2:3, ls]) - zv) * mu_ref[2:3, ls]
        prev_ref[0:1, ls] = zr[tb - 1:tb, :]
        prev_ref[1:2, ls] = zk[tb - 1:tb, :]
        prev_ref[2:3, ls] = zv[tb - 1:tb, :]
        u = -(w0_ref[:, ls] + jnp.dot(tw, w2_ref[:, ls], preferred_element_type=_F32))
        softplus = jnp.maximum(u, 0.0) + jnp.log(1.0 + jnp.exp(-jnp.abs(u)))
        logw = -jnp.exp(-softplus - 0.5)
        a = _sigmoid(a0_ref[:, ls] + jnp.dot(xa, a2_ref[:, ls], preferred_element_type=_F32))
        g = jnp.dot(sg, g2_ref[:, ls], preferred_element_type=_F32)
        kk = k * kkp_ref[:, ls]
        kk = kk / jnp.maximum(jnp.sqrt(head_sum(kk * kk)), 1e-12)
        kh = k * (1.0 + (a - 1.0) * kap_ref[:, ls])
        pairs.append(dict(ls=ls, r=r, v=v, logw=logw, kk=kk, kh=kh, beta=kk * a, g=g,
                          bonus=head_sum(r * kh * rk_ref[:, ls]) * v))

    states = [st_ref[pi] for pi in range(n_pp)]
    group = n_sub
    for sc0 in range(0, n_sub, group):
        items = []
        for sc in range(sc0, sc0 + group):
            rs = slice(sc * C, (sc + 1) * C)
            for pi, pr in enumerate(pairs):
                lw = pr["logw"][rs]
                cum = _cumsum_rows(lw)
                cum_c = cum[C - 1:C, :]
                e_neg = jnp.exp(-cum)
                e_end = jnp.exp(cum_c - cum)
                r2s = stack(pr["r"][rs] * jnp.exp(cum))
                items.append(dict(
                    pi=pi, rs=rs, r2s=r2s,
                    ar=bf(jnp.concatenate([stack(-pr["kk"][rs] * jnp.exp(cum - lw)), r2s], axis=0)),
                    bk=bf(jnp.concatenate([stack(pr["beta"][rs] * e_neg), stack(pr["kh"][rs] * e_neg)],
                                          axis=0)),
                    v2s=bf(stack(pr["v"][rs])),
                    bh2t=bf(stack(pr["beta"][rs] * e_end).T),
                    kh2t=bf(stack(pr["kh"][rs] * e_end).T),
                    dec_col=jnp.sum(eye * jnp.exp(cum_c), axis=-1, keepdims=True)))
        for it in items:
            nn = _bdot_nt(it["ar"], it["bk"])
            n_ba = nn[:P, :P] * tril_s
            it["n_ka"] = bf(nn[:P, P:] * tril_s)
            it["p_br"] = bf(nn[P:, :P] * tril_i)
            it["p_kr"] = bf(nn[P:, P:] * tril_i)
            it["t_inv"] = eye + n_ba
            it["pw"] = bf(n_ba)
        for it in items:
            it["pw"] = bf(_bdot(it["pw"], it["pw"]))
        s = 2
        while s < C:
            for it in items:
                sq = _bdot(it["pw"], jnp.concatenate([it["pw"], bf(it["t_inv"])], axis=1))
                it["pw"] = bf(sq[:, :P])
                it["t_inv"] = it["t_inv"] + sq[:, P:]
            s *= 2
        for it in items:
            it["nkv"] = bf(_bdot(it["n_ka"], it["v2s"]))
        for it in items:
            it["tt"] = bf(_bdot(bf(it["t_inv"]), jnp.concatenate([it["ar"][:P], it["nkv"]], axis=1)))
        for it in items:
            it["mg"] = _bdot(it["bh2t"], it["tt"])
        for it in items:
            it["pg"] = _bdot(it["p_br"], it["tt"])
        for it in items:
            it["g_c"] = it["mg"][:, P:] + _bdot(it["kh2t"], it["v2s"])
        for it in items:
            it["yg"] = it["pg"][:, P:] + _bdot(it["p_kr"], it["v2s"])
        for it in items:
            pi, rs, pr = it["pi"], it["rs"], pairs[it["pi"]]
            ls = pr["ls"]
            st = states[pi]
            st_b = bf(st)
            y2 = _bdot(bf(it["r2s"] + it["pg"][:, :P]), st_b) + it["yg"]
            states[pi] = it["dec_col"] * st + _bdot(bf(it["mg"][:, :P]), st_b) + it["g_c"]
            y = y2[0:C, :] + y2[C:2 * C, :]
            mean = head_sum(y) * (1.0 / head)
            yc = y - mean
            var = head_sum(yc * yc) * (1.0 / head)
            yn = yc * lax.rsqrt(var + GN_EPS) * lg_ref[:, ls] + lb_ref[:, ls]
            y_ref[rs, ls] = ((yn + pr["bonus"][rs]) * pr["g"][rs]).astype(y_ref.dtype)
    for pi in range(n_pp):
        st_ref[pi] = states[pi]

    @pl.when(c_idx == n_chunks - 1)
    def _():
        sout_ref[0] = st_ref[...]


def _wkv(z, shift_prev, s0_bd, params, *, batch, seq, row0, head, d_a, lora_sizes, m_out, out_buf):
    pw = 2 * head
    n_pairs = d_a // pw
    n_pp = min(WKV_PAIRS_PER_STEP, n_pairs)
    gw = n_pp * pw
    n_groups = n_pairs // n_pp
    lp = sum(lora_sizes)
    assert (3 * d_a) % lp == 0
    tb = _pick_tile(seq, 256, WKV_CHUNK)
    n_chunks = seq // tb
    assert row0 % tb == 0
    rb0 = row0 // tb
    rpb = seq // tb
    aliased = out_buf is not None

    def rows(b, c):
        return rb0 + b * rpb + c

    in_specs = [
        pl.BlockSpec((tb, gw), lambda b, p, c: (rows(b, c), p)),
        pl.BlockSpec((tb, gw), lambda b, p, c: (rows(b, c), n_groups + p)),
        pl.BlockSpec((tb, gw), lambda b, p, c: (rows(b, c), 2 * n_groups + p)),
        pl.BlockSpec((tb, lp), lambda b, p, c: (rows(b, c), 3 * d_a // lp)),
        pl.BlockSpec((1, 1, SUBLANES, gw), lambda b, p, c: (b, p, 0, 0)),
        pl.BlockSpec((1, 1, lp), lambda b, p, c: (b, 0, 0)),
        pl.BlockSpec((1, n_pp, pw, pw), lambda b, p, c: (b, p, 0, 0)),
        pl.BlockSpec((SUBLANES, gw), lambda b, p, c: (p, 0)),
        pl.BlockSpec((1, lp), lambda b, p, c: (0, 0)),
    ]
    for _ in range(7):
        in_specs.append(pl.BlockSpec((1, gw), lambda b, p, c: (0, p)))
    nw, na, ng = lora_sizes
    in_specs += [pl.BlockSpec((nw, gw), lambda b, p, c: (0, p)),
                 pl.BlockSpec((na, gw), lambda b, p, c: (0, p)),
                 pl.BlockSpec((ng, gw), lambda b, p, c: (0, p))]
    args = [z, z, z, z, shift_prev["rkv"], shift_prev["lora"], s0_bd,
            params["mu_rkv"], params["mu_lora"], params["w0"], params["a0"], params["k_k"],
            params["k_a"], params["r_k"], params["lnx_g"], params["lnx_b"],
            params["w2"], params["a2"], params["g2"]]
    aliases = {}
    if aliased:
        in_specs.append(pl.BlockSpec(memory_space=pl.ANY))
        aliases = {len(args): 0}
        args.append(out_buf)
    out_specs = [pl.BlockSpec((tb, gw), lambda b, p, c: (rows(b, c), p)),
                 pl.BlockSpec((1, n_pp, pw, pw), lambda b, p, c: (b, p, 0, 0))]
    out_shape = [jax.ShapeDtypeStruct((m_out, d_a), _BF16),
                 jax.ShapeDtypeStruct((batch, n_pairs, pw, pw), _F32)]
    kern = functools.partial(_wkv_kernel, n_chunks=n_chunks, head=head, n_pp=n_pp, aliased=int(aliased))
    return pl.pallas_call(
        kern,
        grid=(batch, n_groups, n_chunks),
        in_specs=in_specs,
        out_specs=out_specs,
        out_shape=out_shape,
        scratch_shapes=[pltpu.VMEM((n_pp, pw, pw), _F32),
                        pltpu.VMEM((SUBLANES, gw), _F32),
                        pltpu.VMEM((1, lp), _F32)],
        input_output_aliases=aliases,
        compiler_params=_compiler_params(("parallel", "parallel", "arbitrary")),
        name="wkv7_chunked",
    )(*args)


def _dsa_kernel(*refs, tq, l_ext, l_true, q_pos0, topk, idx_scale, n_kv, dh, d_idx, aliased):
    q_ref, k_ref, v_ref, qi_ref, wi_ref, ki_ref = refs[:6]
    o_ref, kbf, vbf, kibf, wib, isc_scr, key_scr, dm_scr = refs[6 + aliased:]
    qb = pl.program_id(1)
    n_q = q_ref.shape[0]
    n_i = qi_ref.shape[0]
    rep = n_q // n_kv

    @pl.when(qb == 0)
    def _():
        for g in range(n_kv):
            kbf[g] = k_ref[0:l_ext, g * dh:(g + 1) * dh].astype(_BF16)
            vbf[g] = v_ref[0:l_ext, g * dh:(g + 1) * dh].astype(_BF16)
        kibf[...] = ki_ref[0:l_ext, 0:d_idx].astype(_BF16)

    wi = wi_ref[...]
    for hi in range(n_i):
        wib[hi] = jnp.broadcast_to(wi[:, hi:hi + 1], (tq, LANES))
    isc_scr[...] = jnp.zeros((tq, l_ext), _F32)
    ki = kibf[...]

    hs = 4 if n_i % 4 == 0 else 1

    def idx_body(i, carry):
        qs = qi_ref[pl.ds(i * hs, hs)].reshape(hs * tq, d_idx)
        sc = lax.dot_general(qs, ki, (((1,), (1,)), ((), ())), preferred_element_type=_F32)
        acc = isc_scr[...]
        for j in range(hs):
            acc = acc + (jnp.maximum(sc[j * tq:(j + 1) * tq], 0.0)
                         * jnp.tile(wib[i * hs + j], (1, l_ext // LANES)))
        isc_scr[...] = acc
        return carry

    lax.fori_loop(0, n_i // hs, idx_body, 0)

    q_pos = q_pos0 + qb * tq + lax.broadcasted_iota(jnp.int32, (tq, l_ext), 0)
    k_pos = lax.broadcasted_iota(jnp.int32, (tq, l_ext), 1)
    cs = CHUNK.bit_length() - 1
    assert 1 << cs == CHUNK
    allowed = ((k_pos >> cs) <= (q_pos >> cs)) & (k_pos < l_true)
    bits = pltpu.bitcast(isc_scr[...] * idx_scale, jnp.int32)
    int_min = jnp.int32(-2 ** 31)
    key_scr[...] = jnp.where(allowed, jnp.where(bits < 0, bits ^ jnp.int32(0x7FFFFFFF), bits), int_min)

    def count_ge(cand):
        return jnp.sum(jnp.where(key_scr[...] >= cand, 1.0, 0.0), axis=-1, keepdims=True)

    kf = jnp.float32(topk)
    zero = jnp.zeros((tq, 1), jnp.int32)
    thr = jnp.where(count_ge(zero) >= kf, zero, jnp.full((tq, 1), int_min, jnp.int32))

    def thr_body(i, thr):
        cand = thr + (jnp.int32(1) << (jnp.int32(30) - i))
        return jnp.where(count_ge(cand) >= kf, cand, thr)

    thr = lax.fori_loop(0, 31, thr_body, thr)
    key = key_scr[...]
    n_gt = jnp.sum(jnp.where(key > thr, 1.0, 0.0), axis=-1, keepdims=True)
    n_eq = jnp.sum(jnp.where(key == thr, 1.0, 0.0), axis=-1, keepdims=True)
    need = kf - n_gt
    n_bits = max(1, (l_ext - 1).bit_length())
    full = jnp.full((tq, 1), 1 << n_bits, jnp.int32)

    def tie_bound():
        def count_tie_below(bound):
            hit = jnp.where(key_scr[...] == thr, jnp.where(k_pos < bound, 1.0, 0.0), 0.0)
            return jnp.sum(hit, axis=-1, keepdims=True)

        def tbody(i, bound):
            cand = bound + (jnp.int32(1) << (jnp.int32(n_bits - 1) - i))
            return jnp.where(count_tie_below(cand) <= need, cand, bound)

        return lax.fori_loop(0, n_bits, tbody, zero)

    has_excess_ties = jnp.max(n_eq - need) > 0.0
    bound = lax.cond(has_excess_ties, tie_bound, lambda: full)
    sel = allowed & ((key > thr) | ((key == thr) & (k_pos < bound)))
    dm_scr[...] = jnp.where(sel, jnp.abs(q_pos - k_pos).astype(_F32), DIST_EXCLUDED)

    def qk(h, kg):
        return lax.dot_general(q_ref[h], kg, (((1,), (1,)), ((), ())), preferred_element_type=_F32)

    def head_body(g, carry):
        kg, vg = kbf[g], vbf[g]
        dm = dm_scr[...]
        s_next = qk(g * rep, kg)
        for j in range(rep):
            h = g * rep + j
            s_cur = s_next
            if j + 1 < rep:
                s_next = qk(h + 1, kg)
            slope2 = jnp.exp2(jnp.full((1, 1), h + 1, jnp.int32).astype(_F32) * (-8.0 / n_q)) * LOG2_E
            s = s_cur - slope2 * dm
            m = jnp.max(s, axis=-1, keepdims=True)
            p = jnp.exp2(s - m)
            l = jnp.sum(p, axis=-1, keepdims=True)
            o = jnp.dot(p.astype(_BF16), vg, preferred_element_type=_F32)
            o_ref[h] = (o / l).astype(o_ref.dtype)
        return carry

    lax.fori_loop(0, n_kv, head_body, 0)


def _dsa(q_hm, qi_hm, kv_src, k_col, v_col, ki_src, ki_col, wi_src, wi_col, *, batch, n_qb, tq,
         q_row0, q_stride, kv_rows, kv_stride, l_ext, l_true, q_pos0, topk, n_kv, m_out, out_buf):
    n_q, _, dh = q_hm.shape
    n_i, _, d_idx = qi_hm.shape
    assert q_row0 % tq == 0 and q_stride % tq == 0 and l_ext % LANES == 0 and l_ext <= kv_rows
    qr0, qst = q_row0 // tq, q_stride // tq
    aliased = out_buf is not None
    kern = functools.partial(
        _dsa_kernel, tq=tq, l_ext=l_ext, l_true=l_true, q_pos0=q_pos0, topk=topk,
        idx_scale=float(n_i) ** -0.5 * float(d_idx) ** -0.5, n_kv=n_kv, dh=dh, d_idx=d_idx,
        aliased=int(aliased))
    qrow = lambda b, i: qr0 + b * qst + i
    in_specs = [
        pl.BlockSpec((n_q, tq, dh), lambda b, i: (0, qrow(b, i), 0)),
        pl.BlockSpec((kv_rows, n_kv * dh), lambda b, i: (b * kv_stride, k_col)),
        pl.BlockSpec((kv_rows, n_kv * dh), lambda b, i: (b * kv_stride, v_col)),
        pl.BlockSpec((n_i, tq, d_idx), lambda b, i: (0, qrow(b, i), 0)),
        pl.BlockSpec((tq, LANES), lambda b, i: (qrow(b, i), wi_col)),
        pl.BlockSpec((kv_rows, LANES), lambda b, i: (b * kv_stride, ki_col)),
    ]
    args = [q_hm, kv_src, kv_src, qi_hm, wi_src, ki_src]
    aliases = {}
    if aliased:
        in_specs.append(pl.BlockSpec(memory_space=pl.ANY))
        aliases = {len(args): 0}
        args.append(out_buf)
    return pl.pallas_call(
        kern,
        grid=(batch, n_qb),
        in_specs=in_specs,
        out_specs=pl.BlockSpec((n_q, tq, dh), lambda b, i: (0, qrow(b, i), 0)),
        out_shape=jax.ShapeDtypeStruct((n_q, m_out, dh), _BF16),
        scratch_shapes=[pltpu.VMEM((n_kv, l_ext, dh), _BF16), pltpu.VMEM((n_kv, l_ext, dh), _BF16),
                        pltpu.VMEM((l_ext, d_idx), _BF16), pltpu.VMEM((n_i, tq, LANES), _F32),
                        pltpu.VMEM((tq, l_ext), _F32), pltpu.VMEM((tq, l_ext), jnp.int32),
                        pltpu.VMEM((tq, l_ext), _F32)],
        input_output_aliases=aliases,
        compiler_params=_compiler_params(("parallel", "arbitrary")),
        name="dsa_attention",
    )(*args)


def _pad_rows(w, rows):
    return jnp.pad(w, ((0, rows - w.shape[0]), (0, 0)))


def _pad_cols(w, cols):
    return jnp.pad(w, ((0, 0), (0, cols - w.shape[1])))


def _to_block_diag_t(s, head):
    b, h = s.shape[:2]
    st = jnp.swapaxes(s, -1, -2).reshape(b, h // 2, 2, head, head).astype(_F32)
    z = jnp.zeros_like(st[:, :, 0])
    top = jnp.concatenate([st[:, :, 0], z], axis=-1)
    bot = jnp.concatenate([z, st[:, :, 1]], axis=-1)
    return jnp.concatenate([top, bot], axis=-2)


def _from_block_diag_t(sbd, head):
    b, hp = sbd.shape[:2]
    s0 = sbd[:, :, :head, :head]
    s1 = sbd[:, :, head:, head:]
    st = jnp.stack([s0, s1], axis=2).reshape(b, hp * 2, head, head)
    return jnp.swapaxes(st, -1, -2)


def kernel(x_prompt, x_sample, cache_k, cache_v, cache_kidx, state_shift, state_wkv, norm1_g, w_in, mu_shift, w0, w2, a0, a2, g2, k_k, k_a, r_k, lnx_g, lnx_b, w_br_a, w_br_b, w_out, norm2_g, w_ffn_gate, w_ffn_up, w_ffn_down, norm_f_g):
    depth = w_in.shape[0]
    assert depth == 1
    bp, tp, d_model = x_prompt.shape
    bs, ts, _ = x_sample.shape
    _, _, past, n_kv, dh = cache_k.shape
    d_idx = cache_kidx.shape[-1]
    d_shift = state_shift.shape[-1]
    _, _, h_a, head, _ = state_wkv.shape
    d_a = h_a * head
    r_w, r_a, r_g = w2.shape[1], a2.shape[1], g2.shape[1]
    n_q = w_br_b.shape[1] // dh
    d_in = w_in.shape[-1]
    h_i = (d_in - d_shift - n_q * dh - 2 * n_kv * dh - d_idx - 2 * d_model) // (d_idx + 1)
    d_ff = w_ffn_gate.shape[-1]
    kvw = n_kv * dh
    l = 0
    mp, ms = bp * tp, bs * ts
    m_tot = mp + ms
    xp2, xs2 = x_prompt.reshape(mp, d_model), x_sample.reshape(ms, d_model)
    tm_mm = _pick_tile(m_tot, 1024, 2 * SUBLANES)

    win = w_in[l]
    o = 0
    def take(n):
        nonlocal o
        blk = win[:, o:o + n]
        o += n
        return blk
    nwp, nap, ngp = _round_up(r_w, LANES), _round_up(r_a, LANES), _round_up(r_g, LANES)
    lp = nwp + nap + ngp
    w_rkv, w_lw, w_la, w_lg = take(3 * d_a), take(r_w), take(r_a), take(r_g)
    w_rkvl = jnp.concatenate([w_rkv, _pad_cols(w_lw, nwp), _pad_cols(w_la, nap), _pad_cols(w_lg, ngp)],
                             axis=1).astype(_BF16)
    w_q = take(n_q * dh).astype(_BF16)
    w_k, w_v = take(kvw), take(kvw)
    w_qi = take(h_i * d_idx).astype(_BF16)
    w_ki, w_wi = take(d_idx), take(h_i)
    w_kv = jnp.concatenate([w_k, w_v, _pad_cols(w_ki, LANES), _pad_cols(w_wi, LANES)], axis=1).astype(_BF16)
    w_gates = jnp.concatenate([take(d_model), take(d_model)], axis=1).astype(_BF16)

    h1 = _rmsnorm_cat(xp2, xs2, norm1_g[l], _BF16)
    ident = lambda accs, ex: [accs[0]]
    n_rkvl = 3 * d_a + lp
    (z_a,) = _matmul([h1], [w_rkvl], [], ident, [(n_rkvl, _F32, None)],
                     tm=tm_mm, tn=_pick_tile(n_rkvl, 1024, LANES), name="proj_rkv_lora")
    n_kvz = 2 * kvw + 2 * LANES
    (z_kv,) = _matmul([h1], [w_kv], [], ident, [(n_kvz, _F32, None)],
                      tm=tm_mm, tn=_pick_tile(n_kvz, 1280, LANES), name="proj_kv_idx")
    q_scale = float(dh) ** -0.5 * LOG2_E
    (q_hm,) = _matmul([h1], [w_q], [], lambda accs, ex: [accs[0] * q_scale],
                      [(n_q * dh, _BF16, dh)], tm=tm_mm, tn=_pick_tile(n_q * dh, 1024, dh), name="proj_q")
    (qi_hm,) = _matmul([h1], [w_qi], [], ident, [(h_i * d_idx, _BF16, d_idx)],
                       tm=tm_mm, tn=_pick_tile(h_i * d_idx, 1024, LANES), name="proj_qi")
    (gates,) = _matmul([h1], [w_gates], [], ident, [(2 * d_model, _BF16, None)],
                       tm=tm_mm, tn=_pick_tile(2 * d_model, 1024, LANES), name="proj_gates")
    kx, vx = z_kv[:, :kvw], z_kv[:, kvw:2 * kvw]
    ki_new = z_kv[:, 2 * kvw:2 * kvw + d_idx]

    mu = mu_shift[l]
    def lora_row(vec):
        return jnp.concatenate([
            jnp.pad(vec[3 * d_a:3 * d_a + r_w], (0, nwp - r_w)),
            jnp.pad(vec[3 * d_a + r_w:3 * d_a + r_w + r_a], (0, nap - r_a)),
            jnp.pad(vec[3 * d_a + r_w + r_a:], (0, ngp - r_g))])
    n_pairs = d_a // (2 * head)
    gw = min(WKV_PAIRS_PER_STEP, n_pairs) * 2 * head
    n_groups = d_a // gw
    def rkv_rows(vec):
        lead = vec.shape[:-1]
        x = vec[..., :3 * d_a].reshape(lead + (3, n_groups, gw))
        x = jnp.moveaxis(x, -3, -2)
        return jnp.pad(x, [(0, 0)] * (len(lead) + 1) + [(0, SUBLANES - 3), (0, 0)])
    row = lambda v: v[l].reshape(1, d_a).astype(_F32)
    params = dict(
        mu_rkv=rkv_rows(mu).reshape(n_groups * SUBLANES, gw),
        mu_lora=lora_row(mu).reshape(1, lp),
        w0=row(w0), a0=row(a0), k_k=row(k_k), k_a=row(k_a),
        r_k=r_k[l].reshape(1, d_a).astype(_F32), lnx_g=row(lnx_g), lnx_b=row(lnx_b),
        w2=_pad_rows(w2[l], nwp).astype(_BF16), a2=_pad_rows(a2[l], nap).astype(_BF16),
        g2=_pad_rows(g2[l], ngp).astype(_BF16))

    def shift_state(s):
        s = s[:, 0]
        return dict(rkv=rkv_rows(s), lora=jax.vmap(lora_row)(s).reshape(-1, 1, lp))

    wkv_kw = dict(head=head, d_a=d_a, lora_sizes=(nwp, nap, ngp), m_out=m_tot)
    y_a, sp_bd = _wkv(z_a, shift_state(jnp.zeros((bp, 1, d_shift), _F32)),
                      jnp.zeros((bp, n_pairs, 2 * head, 2 * head), _F32), params,
                      batch=bp, seq=tp, row0=0, out_buf=None, **wkv_kw)
    y_a, ss_bd = _wkv(z_a, shift_state(state_shift[l]), _to_block_diag_t(state_wkv[l], head), params,
                      batch=bs, seq=ts, row0=mp, out_buf=y_a, **wkv_kw)
    wkv_p = _from_block_diag_t(sp_bd, head)
    wkv_s = _from_block_diag_t(ss_bd, head)

    topk_p = min(TOPK_MAX, tp // 4)
    tq_p = _pick_tile(tp, 128, CHUNK)
    n_grp = max(1, min(DSA_GROUPS, tp // tq_p))
    grp = tp // n_grp
    wi_col = (2 * kvw + LANES) // LANES
    ki_col = 2 * kvw // LANES
    y_b = None
    for gi in range(n_grp):
        l_ext = (gi + 1) * grp
        kv_rows = l_ext
        while tp % kv_rows:
            kv_rows += grp
        y_b = _dsa(q_hm, qi_hm, z_kv, 0, 1, z_kv, ki_col, z_kv, wi_col,
                   batch=bp, n_qb=grp // tq_p, tq=tq_p, q_row0=gi * grp, q_stride=tp,
                   kv_rows=kv_rows, kv_stride=tp // kv_rows, l_ext=l_ext, l_true=l_ext,
                   q_pos0=gi * grp, topk=topk_p, n_kv=n_kv, m_out=m_tot, out_buf=y_b)
    ls = past + ts
    lps = _round_up(ls, LANES)
    def with_cache(cache, new, width):
        x = jnp.concatenate([cache.reshape(bs, past, width), new.reshape(bs, ts, width)], axis=1)
        return jnp.pad(x, ((0, 0), (0, lps - ls), (0, 0))).reshape(bs * lps, width)
    kv_s = jnp.concatenate([with_cache(cache_k[l], kx[mp:], kvw), with_cache(cache_v[l], vx[mp:], kvw)], axis=1)
    ki_s = _pad_cols(with_cache(cache_kidx[l], ki_new[mp:], d_idx), LANES)
    y_b = _dsa(q_hm, qi_hm, kv_s, 0, 1, ki_s, 0, z_kv, wi_col,
               batch=bs, n_qb=1, tq=ts, q_row0=mp, q_stride=ts, kv_rows=lps, kv_stride=1, l_ext=lps, l_true=ls,
               q_pos0=past, topk=min(TOPK_MAX, ls // 4), n_kv=n_kv, m_out=m_tot, out_buf=y_b)

    tn_mg = _pick_tile(d_model, 1024, LANES)
    nj_mg = d_model // tn_mg
    def merge_ep(accs, ex):
        return [_sigmoid(ex[0].astype(_F32)) * accs[0] + _sigmoid(ex[1].astype(_F32)) * accs[1]]
    (merged,) = _matmul([y_a, y_b], [w_br_a[l].astype(_BF16), w_br_b[l].astype(_BF16)],
                        [(gates, tn_mg, lambda i, j: (i, j)), (gates, tn_mg, lambda i, j: (i, nj_mg + j))],
                        merge_ep, [(d_model, _BF16, None)], tm=tm_mm, tn=tn_mg, name="branch_merge")
    tn_res = _pick_tile(d_model, 1024, LANES)
    w_out_b = w_out[l].astype(_BF16)
    resid_ep = lambda accs, ex: [ex[0] + accs[0]]
    tm_p = _pick_tile(mp, 1024, 2 * SUBLANES)
    tm_s = _pick_tile(math.gcd(mp, ms), 1024, 2 * SUBLANES)
    (x1,) = _matmul([merged], [w_out_b], [(xp2, tn_res, lambda i, j: (i, j))], resid_ep,
                    [(d_model, _F32, None)], tm=tm_p, tn=tn_res, row0=0, rows=mp, name="out_proj")
    (x1,) = _matmul([merged], [w_out_b], [(xs2, tn_res, lambda i, j: (i, j))], resid_ep,
                    [(d_model, _F32, None)], tm=tm_s, tn=tn_res, row0=mp, rows=ms, out_buf=x1,
                    name="out_proj")

    h2 = _rmsnorm_rows(x1, norm2_g[l], _BF16, 0, m_tot)
    tn_ff = 512
    d_ffp = _round_up(d_ff, 2 * tn_ff)
    ff_pad_cols = jnp.zeros((d_model, d_ffp - d_ff), _BF16)
    wg = jnp.concatenate([w_ffn_gate[l].astype(_BF16), ff_pad_cols], axis=1)
    wu = jnp.concatenate([w_ffn_up[l].astype(_BF16), ff_pad_cols], axis=1)
    def swiglu_ep(accs, ex):
        return [accs[0] * _sigmoid(accs[0]) * accs[1]]
    (u,) = _matmul([h2], [wg, wu], [], swiglu_ep, [(d_ffp, _BF16, None)], tm=tm_mm, tn=tn_ff,
                   name="ffn_up")
    wd = jnp.concatenate([w_ffn_down[l].astype(_BF16), jnp.zeros((d_ffp - d_ff, d_model), _BF16)], axis=0)
    tn_dn = _pick_tile(d_model, 1024, LANES)
    tk_dn = _pick_tile(d_ffp, 2816, LANES)
    (x2,) = _matmul([u], [wd], [(x1, tn_dn, lambda i, j: (i, j))], resid_ep,
                    [(d_model, _F32, None)], tm=tm_mm, tn=tn_dn, tk=tk_dn, name="ffn_down")
    y_p = _rmsnorm_rows(x2, norm_f_g, _F32, 0, mp).reshape(bp, tp, d_model)
    y_s = _rmsnorm_rows(x2, norm_f_g, _F32, mp, ms).reshape(bs, ts, d_model)

    kx4 = lambda x, b, t: x.reshape(1, b, t, n_kv, dh)
    def zsh_last(rows):
        zl = z_a[rows]
        return jnp.concatenate([zl[:, :3 * d_a + r_w], zl[:, 3 * d_a + nwp:3 * d_a + nwp + r_a],
                                zl[:, 3 * d_a + nwp + nap:3 * d_a + nwp + nap + r_g]], axis=-1)
    last_p = jnp.arange(bp) * tp + (tp - 1)
    last_s = mp + jnp.arange(bs) * ts + (ts - 1)
    return (y_p, y_s,
            kx4(kx[:mp], bp, tp), kx4(vx[:mp], bp, tp), ki_new[:mp].reshape(1, bp, tp, d_idx),
            zsh_last(last_p).reshape(1, bp, 1, d_shift), wkv_p[None],
            kx4(kx[mp:], bs, ts), kx4(vx[mp:], bs, ts), ki_new[mp:].reshape(1, bs, ts, d_idx),
            zsh_last(last_s).reshape(1, bs, 1, d_shift), wkv_s[None])
```

```python
import functools
import math

import jax
import jax.numpy as jnp
from jax import lax
from jax.experimental import pallas as pl
from jax.experimental.pallas import tpu as pltpu

CHUNK = 64
NORM_EPS = 1e-6
GN_EPS = 64e-5
TOPK_MAX = 256

LANES = 128
SUBLANES = 8
V7X_VMEM_LIMIT_BYTES = 56 * 1024 * 1024

WKV_CHUNK = 64
WKV_PAIRS_PER_STEP = 4
DIST_EXCLUDED = 1e30
DSA_GROUPS = 8
LOG2_E = 1.4426950408889634

_F32 = jnp.float32
_BF16 = jnp.bfloat16


def _round_up(n, m):
    return (n + m - 1) // m * m


def _pick_tile(n, pref, align):
    if n <= pref:
        return n
    t = pref // align * align
    while t >= align:
        if n % t == 0:
            return t
        t -= align
    return n


def _compiler_params(semantics):
    return pltpu.CompilerParams(dimension_semantics=semantics,
                                vmem_limit_bytes=V7X_VMEM_LIMIT_BYTES)


def _sigmoid(x):
    return 1.0 / (1.0 + jnp.exp(-x))


def _rms(x, g):
    x = x.astype(_F32)
    ms = jnp.mean(x * x, axis=-1, keepdims=True)
    return x * lax.rsqrt(ms + NORM_EPS) * g


def _rmsnorm_cat_kernel(xa_ref, xb_ref, g_ref, o_ref, *, n_a):
    i = pl.program_id(0)

    @pl.when(i < n_a)
    def _():
        o_ref[...] = _rms(xa_ref[...], g_ref[...]).astype(o_ref.dtype)

    @pl.when(i >= n_a)
    def _():
        o_ref[...] = _rms(xb_ref[...], g_ref[...]).astype(o_ref.dtype)


def _rmsnorm_cat(xa, xb, g, out_dtype):
    (ma, d), mb = xa.shape, xb.shape[0]
    tm = _pick_tile(math.gcd(ma, mb), 512, SUBLANES)
    n_a = ma // tm
    return pl.pallas_call(
        functools.partial(_rmsnorm_cat_kernel, n_a=n_a),
        grid=((ma + mb) // tm,),
        in_specs=[pl.BlockSpec((tm, d), lambda i: (jnp.minimum(i, n_a - 1), 0)),
                  pl.BlockSpec((tm, d), lambda i: (jnp.maximum(i - n_a, 0), 0)),
                  pl.BlockSpec((1, d), lambda i: (0, 0))],
        out_specs=pl.BlockSpec((tm, d), lambda i: (i, 0)),
        out_shape=jax.ShapeDtypeStruct((ma + mb, d), out_dtype),
        compiler_params=_compiler_params(("arbitrary",)),
        name="rmsnorm_cat",
    )(xa, xb, g.reshape(1, d).astype(_F32))


def _rmsnorm_rows_kernel(x_ref, g_ref, o_ref):
    o_ref[...] = _rms(x_ref[...], g_ref[...]).astype(o_ref.dtype)


def _rmsnorm_rows(x, g, out_dtype, row0, rows):
    d = x.shape[1]
    tm = _pick_tile(math.gcd(row0, rows) if row0 else rows, 512, SUBLANES)
    rb0 = row0 // tm
    return pl.pallas_call(
        _rmsnorm_rows_kernel,
        grid=(rows // tm,),
        in_specs=[pl.BlockSpec((tm, d), lambda i: (rb0 + i, 0)),
                  pl.BlockSpec((1, d), lambda i: (0, 0))],
        out_specs=pl.BlockSpec((tm, d), lambda i: (i, 0)),
        out_shape=jax.ShapeDtypeStruct((rows, d), out_dtype),
        compiler_params=_compiler_params(("parallel",)),
        name="rmsnorm_rows",
    )(x, g.reshape(1, d).astype(_F32))


def _matmul_kernel(*refs, n_pairs, n_extras, n_outs, n_k, head_major, out_head_w, aliased, epilogue):
    n_x = len(head_major)
    x_refs = refs[:n_x]
    w_refs = refs[n_x:n_x + n_pairs]
    e_refs = refs[n_x + n_pairs:n_x + n_pairs + n_extras]
    o_refs = refs[n_x + n_pairs + n_extras + aliased:n_x + n_pairs + n_extras + aliased + n_outs]
    acc_refs = refs[n_x + n_pairs + n_extras + aliased + n_outs:]

    def load_x(i):
        if head_major[i]:
            xr = x_refs[i]
            return jnp.concatenate([xr[h] for h in range(xr.shape[0])], axis=1)
        return x_refs[i][...]

    def partial(i):
        return jnp.dot(load_x(i if n_x > 1 else 0), w_refs[i][...], preferred_element_type=_F32)

    def finish(accs):
        outs = epilogue(accs, [e[...] for e in e_refs])
        for o_ref, val, hw in zip(o_refs, outs, out_head_w):
            if hw:
                for h in range(o_ref.shape[0]):
                    o_ref[h] = val[:, h * hw:(h + 1) * hw].astype(o_ref.dtype)
            else:
                o_ref[...] = val.astype(o_ref.dtype)

    if n_k == 1:
        finish([partial(i) for i in range(n_pairs)])
        return

    k = pl.program_id(2)

    @pl.when(k == 0)
    def _():
        for i in range(n_pairs):
            acc_refs[i][...] = partial(i)

    @pl.when(k > 0)
    def _():
        for i in range(n_pairs):
            acc_refs[i][...] += partial(i)

    @pl.when(k == n_k - 1)
    def _():
        finish([a[...] for a in acc_refs])


def _matmul(xs, ws, extras, epilogue, outs, *, tm, tn, tk=None, row0=0, rows=None, out_buf=None,
            name="matmul"):
    n_pairs = len(ws)
    assert len(xs) in (1, n_pairs)
    head_major = tuple(x.ndim == 3 for x in xs)
    m = xs[0].shape[1] if head_major[0] else xs[0].shape[0]
    rows = m if rows is None else rows
    kdim, n = ws[0].shape
    if tk is None:
        tk = kdim
    n_k = kdim // tk
    assert kdim % tk == 0 and rows % tm == 0 and row0 % tm == 0 and n % tn == 0
    n_j = n // tn
    rb0 = row0 // tm
    tks = [tk if n_k > 1 else w.shape[0] for w in ws]
    assert all(w.shape == (kdim, n) for w in ws) or n_k == 1

    in_specs = []
    for x, hm, tki in zip(xs, head_major, tks):
        if hm:
            assert n_k == 1
            in_specs.append(pl.BlockSpec((x.shape[0], tm, LANES), lambda i, j, k: (0, rb0 + i, 0)))
        else:
            in_specs.append(pl.BlockSpec((tm, tki), lambda i, j, k: (rb0 + i, k)))
    for w, tki in zip(ws, tks):
        in_specs.append(pl.BlockSpec((tki, tn), lambda i, j, k: (k, j)))
    for arr, cols, imap in extras:
        in_specs.append(pl.BlockSpec((tm, cols), lambda i, j, k, imap=imap: imap(i, j)))
    args = [*xs, *ws, *[e[0] for e in extras]]
    aliases = {}
    if out_buf is not None:
        assert len(outs) == 1
        in_specs.append(pl.BlockSpec(memory_space=pl.ANY))
        aliases = {len(args): 0}
        args.append(out_buf)

    out_specs, out_shapes, out_head_w = [], [], []
    for n_cols, dtype, hw in outs:
        to = n_cols // n_j
        out_head_w.append(hw)
        if hw:
            assert to % hw == 0
            out_specs.append(pl.BlockSpec((to // hw, tm, hw), lambda i, j, k: (j, rb0 + i, 0)))
            out_shapes.append(jax.ShapeDtypeStruct((n_cols // hw, m, hw), dtype))
        else:
            out_specs.append(pl.BlockSpec((tm, to), lambda i, j, k: (rb0 + i, j)))
            out_shapes.append(jax.ShapeDtypeStruct((m, n_cols), dtype))

    scratch = [pltpu.VMEM((tm, tn), _F32) for _ in range(n_pairs)] if n_k > 1 else []
    kern = functools.partial(
        _matmul_kernel, n_pairs=n_pairs, n_extras=len(extras), n_outs=len(outs), n_k=n_k,
        head_major=head_major, out_head_w=tuple(out_head_w), aliased=len(aliases), epilogue=epilogue)
    return pl.pallas_call(
        kern,
        grid=(rows // tm, n_j, n_k),
        in_specs=in_specs,
        out_specs=out_specs,
        out_shape=out_shapes,
        scratch_shapes=scratch,
        input_output_aliases=aliases,
        compiler_params=_compiler_params(("parallel", "parallel", "arbitrary")),
        name=name,
    )(*args)


def _matmul_resid_kernel(x_ref, w_ref, r_ref, o_ref, acc_ref, *, n_k):
    k = pl.program_id(1)
    j = pl.program_id(2)
    part = jnp.dot(x_ref[...], w_ref[...], preferred_element_type=_F32)
    if n_k == 1:
        o_ref[...] = r_ref[...] + part
        return

    @pl.when(k == 0)
    def _():
        acc_ref[j] = part

    @pl.when((k > 0) & (k < n_k - 1))
    def _():
        acc_ref[j] += part

    @pl.when(k == n_k - 1)
    def _():
        o_ref[...] = r_ref[...] + (acc_ref[j] + part)


def _matmul_resid(x, w, resid, *, tm, tn, tk, name):
    m, kdim = x.shape
    n = w.shape[1]
    assert m % tm == 0 and n % tn == 0 and kdim % tk == 0
    n_j, n_k = n // tn, kdim // tk
    last = lambda i, k, j: (i, jnp.where(k == n_k - 1, j, 0))
    return pl.pallas_call(
        functools.partial(_matmul_resid_kernel, n_k=n_k),
        grid=(m // tm, n_k, n_j),
        in_specs=[pl.BlockSpec((tm, tk), lambda i, k, j: (i, k)),
                  pl.BlockSpec((tk, tn), lambda i, k, j: (k, j)),
                  pl.BlockSpec((tm, tn), last)],
        out_specs=pl.BlockSpec((tm, tn), last),
        out_shape=jax.ShapeDtypeStruct((m, n), _F32),
        scratch_shapes=[pltpu.VMEM((n_j, tm, tn), _F32)],
        compiler_params=_compiler_params(("parallel", "arbitrary", "arbitrary")),
        name=name,
    )(x, w, resid)


def _shift_rows(z, prev_row):
    rolled = pltpu.roll(z, 1, 0)
    row = lax.broadcasted_iota(jnp.int32, z.shape, 0)
    return jnp.where(row == 0, prev_row, rolled)


def _cumsum_rows(x):
    n = x.shape[0]
    row = lax.broadcasted_iota(jnp.int32, x.shape, 0)
    s = 1
    while s < n:
        x = x + jnp.where(row >= s, pltpu.roll(x, s, 0), 0.0)
        s *= 2
    return x


def bf(x):
    return x.astype(_BF16)


def _bdot(a, b):
    assert a.dtype == _BF16 and b.dtype == _BF16
    return jnp.dot(a, b, preferred_element_type=_F32)


def _bdot_nt(a, b):
    assert a.dtype == _BF16 and b.dtype == _BF16
    return lax.dot_general(a, b, (((1,), (1,)), ((), ())), preferred_element_type=_F32)


def _wkv_kernel(*refs, n_chunks, head, n_pp, aliased):
    (zr_ref, zk_ref, zv_ref, zl_ref, sprev_ref, lprev_ref, s0_ref,
     mu_ref, mul_ref, w0_ref, a0_ref, kkp_ref, kap_ref, rk_ref, lg_ref, lb_ref,
     w2_ref, a2_ref, g2_ref) = refs[:19]
    y_ref, sout_ref, st_ref, prev_ref, lprev_scr = refs[19 + aliased:]
    c_idx = pl.program_id(2)
    C = WKV_CHUNK
    P = 2 * head
    assert P == 2 * C
    tb = zr_ref.shape[0]
    n_sub = tb // C

    @pl.when(c_idx == 0)
    def _():
        st_ref[...] = s0_ref[0]
        prev_ref[...] = sprev_ref[0, 0]
        lprev_scr[...] = lprev_ref[0]

    lane = lax.broadcasted_iota(jnp.int32, (1, P), 1)
    m0 = (lane < head).astype(_F32)
    m1 = 1.0 - m0
    r2 = lax.broadcasted_iota(jnp.int32, (P, P), 0)
    c2 = lax.broadcasted_iota(jnp.int32, (P, P), 1)
    same = (r2 // C) == (c2 // C)
    tril_s = jnp.where(same & (c2 < r2), 1.0, 0.0)
    tril_i = jnp.where(same & (c2 <= r2), 1.0, 0.0)
    eye = jnp.where(r2 == c2, 1.0, 0.0)

    def stack(x):
        return jnp.concatenate([x * m0, x * m1], axis=0)

    def head_sum(x):
        s0 = jnp.sum(x * m0, axis=-1, keepdims=True)
        s1 = jnp.sum(x * m1, axis=-1, keepdims=True)
        return s0 * m0 + s1 * m1

    zl = zl_ref[...]
    xl = zl + (_shift_rows(zl, lprev_scr[...]) - zl) * mul_ref[...]
    lprev_scr[...] = zl[tb - 1:tb, :]
    nw = w2_ref.shape[0]
    na = a2_ref.shape[0]
    tw = jnp.tanh(xl[:, 0:nw]).astype(_BF16)
    xa = xl[:, nw:nw + na].astype(_BF16)
    sg = _sigmoid(xl[:, nw + na:]).astype(_BF16)

    pairs = []
    for pi in range(n_pp):
        ls = slice(pi * P, (pi + 1) * P)
        zr, zk, zv = zr_ref[:, ls], zk_ref[:, ls], zv_ref[:, ls]
        r = zr + (_shift_rows(zr, prev_ref[0:1, ls]) - zr) * mu_ref[0:1, ls]
        k = zk + (_shift_rows(zk, prev_ref[1:2, ls]) - zk) * mu_ref[1:2, ls]
        v = zv + (_shift_rows(zv, prev_ref[2:3, ls]) - zv) * mu_ref[2:3, ls]
        prev_ref[0:1, ls] = zr[tb - 1:tb, :]
        prev_ref[1:2, ls] = zk[tb - 1:tb, :]
        prev_ref[2:3, ls] = zv[tb - 1:tb, :]
        u = -(w0_ref[:, ls] + jnp.dot(tw, w2_ref[:, ls], preferred_element_type=_F32))
        softplus = jnp.maximum(u, 0.0) + jnp.log(1.0 + jnp.exp(-jnp.abs(u)))
        logw = -jnp.exp(-softplus - 0.5)
        a = _sigmoid(a0_ref[:, ls] + jnp.dot(xa, a2_ref[:, ls], preferred_element_type=_F32))
        g = jnp.dot(sg, g2_ref[:, ls], preferred_element_type=_F32)
        kk = k * kkp_ref[:, ls]
        kk = kk / jnp.maximum(jnp.sqrt(head_sum(kk * kk)), 1e-12)
        kh = k * (1.0 + (a - 1.0) * kap_ref[:, ls])
        pairs.append(dict(ls=ls, r=r, v=v, logw=logw, kk=kk, kh=kh, beta=kk * a, g=g,
                          bonus=head_sum(r * kh * rk_ref[:, ls]) * v))

    states = [st_ref[pi] for pi in range(n_pp)]
    group = n_sub
    for sc0 in range(0, n_sub, group):
        items = []
        for sc in range(sc0, sc0 + group):
            rs = slice(sc * C, (sc + 1) * C)
            for pi, pr in enumerate(pairs):
                lw = pr["logw"][rs]
                cum = _cumsum_rows(lw)
                cum_c = cum[C - 1:C, :]
                e_neg = jnp.exp(-cum)
                e_end = jnp.exp(cum_c - cum)
                r2s = stack(pr["r"][rs] * jnp.exp(cum))
                items.append(dict(
                    pi=pi, rs=rs, r2s=r2s,
                    ar=bf(jnp.concatenate([stack(-pr["kk"][rs] * jnp.exp(cum - lw)), r2s], axis=0)),
                    bk=bf(jnp.concatenate([stack(pr["beta"][rs] * e_neg), stack(pr["kh"][rs] * e_neg)],
                                          axis=0)),
                    v2s=bf(stack(pr["v"][rs])),
                    bh2t=bf(stack(pr["beta"][rs] * e_end).T),
                    kh2t=bf(stack(pr["kh"][rs] * e_end).T),
                    dec_col=jnp.sum(eye * jnp.exp(cum_c), axis=-1, keepdims=True)))
        for it in items:
            nn = _bdot_nt(it["ar"], it["bk"])
            n_ba = nn[:P, :P] * tril_s
            it["n_ka"] = bf(nn[:P, P:] * tril_s)
            it["p_br"] = bf(nn[P:, :P] * tril_i)
            it["p_kr"] = bf(nn[P:, P:] * tril_i)
            it["t_inv"] = eye + n_ba
            it["pw"] = bf(n_ba)
        for it in items:
            it["pw"] = bf(_bdot(it["pw"], it["pw"]))
        s = 2
        while s < C:
            for it in items:
                sq = _bdot(it["pw"], jnp.concatenate([it["pw"], bf(it["t_inv"])], axis=1))
                it["pw"] = bf(sq[:, :P])
                it["t_inv"] = it["t_inv"] + sq[:, P:]
            s *= 2
        for it in items:
            it["nkv"] = bf(_bdot(it["n_ka"], it["v2s"]))
        for it in items:
            it["tt"] = bf(_bdot(bf(it["t_inv"]), jnp.concatenate([it["ar"][:P], it["nkv"]], axis=1)))
        for it in items:
            it["mg"] = _bdot(it["bh2t"], it["tt"])
        for it in items:
            it["pg"] = _bdot(it["p_br"], it["tt"])
        for it in items:
            it["g_c"] = it["mg"][:, P:] + _bdot(it["kh2t"], it["v2s"])
        for it in items:
            it["yg"] = it["pg"][:, P:] + _bdot(it["p_kr"], it["v2s"])
        for it in items:
            pi, rs, pr = it["pi"], it["rs"], pairs[it["pi"]]
            ls = pr["ls"]
            st = states[pi]
            st_b = bf(st)
            y2 = _bdot(bf(it["r2s"] + it["pg"][:, :P]), st_b) + it["yg"]
            states[pi] = it["dec_col"] * st + _bdot(bf(it["mg"][:, :P]), st_b) + it["g_c"]
            y = y2[0:C, :] + y2[C:2 * C, :]
            mean = head_sum(y) * (1.0 / head)
            yc = y - mean
            var = head_sum(yc * yc) * (1.0 / head)
            yn = yc * lax.rsqrt(var + GN_EPS) * lg_ref[:, ls] + lb_ref[:, ls]
            y_ref[rs, ls] = ((yn + pr["bonus"][rs]) * pr["g"][rs]).astype(y_ref.dtype)
    for pi in range(n_pp):
        st_ref[pi] = states[pi]

    @pl.when(c_idx == n_chunks - 1)
    def _():
        sout_ref[0] = st_ref[...]


def _wkv(z, shift_prev, s0_bd, params, *, batch, seq, row0, head, d_a, lora_sizes, m_out, out_buf):
    pw = 2 * head
    n_pairs = d_a // pw
    n_pp = min(WKV_PAIRS_PER_STEP, n_pairs)
    gw = n_pp * pw
    n_groups = n_pairs // n_pp
    lp = sum(lora_sizes)
    assert (3 * d_a) % lp == 0
    tb = _pick_tile(seq, 256, WKV_CHUNK)
    n_chunks = seq // tb
    assert row0 % tb == 0
    rb0 = row0 // tb
    rpb = seq // tb
    aliased = out_buf is not None

    def rows(b, c):
        return rb0 + b * rpb + c

    in_specs = [
        pl.BlockSpec((tb, gw), lambda b, p, c: (rows(b, c), p)),
        pl.BlockSpec((tb, gw), lambda b, p, c: (rows(b, c), n_groups + p)),
        pl.BlockSpec((tb, gw), lambda b, p, c: (rows(b, c), 2 * n_groups + p)),
        pl.BlockSpec((tb, lp), lambda b, p, c: (rows(b, c), 3 * d_a // lp)),
        pl.BlockSpec((1, 1, SUBLANES, gw), lambda b, p, c: (b, p, 0, 0)),
        pl.BlockSpec((1, 1, lp), lambda b, p, c: (b, 0, 0)),
        pl.BlockSpec((1, n_pp, pw, pw), lambda b, p, c: (b, p, 0, 0)),
        pl.BlockSpec((SUBLANES, gw), lambda b, p, c: (p, 0)),
        pl.BlockSpec((1, lp), lambda b, p, c: (0, 0)),
    ]
    for _ in range(7):
        in_specs.append(pl.BlockSpec((1, gw), lambda b, p, c: (0, p)))
    nw, na, ng = lora_sizes
    in_specs += [pl.BlockSpec((nw, gw), lambda b, p, c: (0, p)),
                 pl.BlockSpec((na, gw), lambda b, p, c: (0, p)),
                 pl.BlockSpec((ng, gw), lambda b, p, c: (0, p))]
    args = [z, z, z, z, shift_prev["rkv"], shift_prev["lora"], s0_bd,
            params["mu_rkv"], params["mu_lora"], params["w0"], params["a0"], params["k_k"],
            params["k_a"], params["r_k"], params["lnx_g"], params["lnx_b"],
            params["w2"], params["a2"], params["g2"]]
    aliases = {}
    if aliased:
        in_specs.append(pl.BlockSpec(memory_space=pl.ANY))
        aliases = {len(args): 0}
        args.append(out_buf)
    out_specs = [pl.BlockSpec((tb, gw), lambda b, p, c: (rows(b, c), p)),
                 pl.BlockSpec((1, n_pp, pw, pw), lambda b, p, c: (b, p, 0, 0))]
    out_shape = [jax.ShapeDtypeStruct((m_out, d_a), _BF16),
                 jax.ShapeDtypeStruct((batch, n_pairs, pw, pw), _F32)]
    kern = functools.partial(_wkv_kernel, n_chunks=n_chunks, head=head, n_pp=n_pp, aliased=int(aliased))
    return pl.pallas_call(
        kern,
        grid=(batch, n_groups, n_chunks),
        in_specs=in_specs,
        out_specs=out_specs,
        out_shape=out_shape,
        scratch_shapes=[pltpu.VMEM((n_pp, pw, pw), _F32),
                        pltpu.VMEM((SUBLANES, gw), _F32),
                        pltpu.VMEM((1, lp), _F32)],
        input_output_aliases=aliases,
        compiler_params=_compiler_params(("parallel", "parallel", "arbitrary")),
        name="wkv7_chunked",
    )(*args)


def _dsa_kernel(*refs, tq, l_ext, l_true, q_pos0, topk, idx_scale, n_kv, dh, d_idx, aliased):
    q_ref, k_ref, v_ref, qi_ref, wi_ref, ki_ref = refs[:6]
    o_ref, kbf, vbf, kibf, wib, isc_scr, key_scr, dm_scr = refs[6 + aliased:]
    qb = pl.program_id(1)
    n_q = q_ref.shape[0]
    n_i = qi_ref.shape[0]
    rep = n_q // n_kv

    @pl.when(qb == 0)
    def _():
        for g in range(n_kv):
            kbf[g] = k_ref[0:l_ext, g * dh:(g + 1) * dh].astype(_BF16)
            vbf[g] = v_ref[0:l_ext, g * dh:(g + 1) * dh].astype(_BF16)
        kibf[...] = ki_ref[0:l_ext, 0:d_idx].astype(_BF16)

    wi = wi_ref[...]
    for hi in range(n_i):
        wib[hi] = jnp.broadcast_to(wi[:, hi:hi + 1], (tq, LANES))
    isc_scr[...] = jnp.zeros((tq, l_ext), _F32)
    ki = kibf[...]

    hs = max(1, min(4, 512 // tq))
    assert n_i % hs == 0

    def idx_body(i, carry):
        qs = qi_ref[pl.ds(i * hs, hs)].reshape(hs * tq, d_idx)
        sc = lax.dot_general(qs, ki, (((1,), (1,)), ((), ())), preferred_element_type=_F32)
        acc = isc_scr[...]
        for j in range(hs):
            acc = acc + (jnp.maximum(sc[j * tq:(j + 1) * tq], 0.0)
                         * jnp.tile(wib[i * hs + j], (1, l_ext // LANES)))
        isc_scr[...] = acc
        return carry

    lax.fori_loop(0, n_i // hs, idx_body, 0)

    q_pos = q_pos0 + qb * tq + lax.broadcasted_iota(jnp.int32, (tq, l_ext), 0)
    k_pos = lax.broadcasted_iota(jnp.int32, (tq, l_ext), 1)
    cs = CHUNK.bit_length() - 1
    assert 1 << cs == CHUNK
    allowed = ((k_pos >> cs) <= (q_pos >> cs)) & (k_pos < l_true)
    bits = pltpu.bitcast(isc_scr[...] * idx_scale, jnp.int32)
    int_min = jnp.int32(-2 ** 31)
    key_scr[...] = jnp.where(allowed, jnp.where(bits < 0, bits ^ jnp.int32(0x7FFFFFFF), bits), int_min)

    def count_ge(cand):
        return jnp.sum(jnp.where(key_scr[...] >= cand, 1.0, 0.0), axis=-1, keepdims=True)

    kf = jnp.float32(topk)
    zero = jnp.zeros((tq, 1), jnp.int32)
    thr = jnp.where(count_ge(zero) >= kf, zero, jnp.full((tq, 1), int_min, jnp.int32))

    def thr_body(i, thr):
        cand = thr + (jnp.int32(1) << (jnp.int32(30) - i))
        return jnp.where(count_ge(cand) >= kf, cand, thr)

    thr = lax.fori_loop(0, 31, thr_body, thr)
    key = key_scr[...]
    n_gt = jnp.sum(jnp.where(key > thr, 1.0, 0.0), axis=-1, keepdims=True)
    n_eq = jnp.sum(jnp.where(key == thr, 1.0, 0.0), axis=-1, keepdims=True)
    need = kf - n_gt
    n_bits = max(1, (l_ext - 1).bit_length())
    full = jnp.full((tq, 1), 1 << n_bits, jnp.int32)

    def tie_bound():
        def count_tie_below(bound):
            hit = jnp.where(key_scr[...] == thr, jnp.where(k_pos < bound, 1.0, 0.0), 0.0)
            return jnp.sum(hit, axis=-1, keepdims=True)

        def tbody(i, bound):
            cand = bound + (jnp.int32(1) << (jnp.int32(n_bits - 1) - i))
            return jnp.where(count_tie_below(cand) <= need, cand, bound)

        return lax.fori_loop(0, n_bits, tbody, zero)

    has_excess_ties = jnp.max(n_eq - need) > 0.0
    bound = lax.cond(has_excess_ties, tie_bound, lambda: full)
    sel = allowed & ((key > thr) | ((key == thr) & (k_pos < bound)))
    dm_scr[...] = jnp.where(sel, jnp.abs(q_pos - k_pos).astype(_F32), DIST_EXCLUDED)

    def qk(h, kg):
        return lax.dot_general(q_ref[h], kg, (((1,), (1,)), ((), ())), preferred_element_type=_F32)

    def head_body(g, carry):
        kg, vg = kbf[g], vbf[g]
        dm = dm_scr[...]
        s_next = qk(g * rep, kg)
        for j in range(rep):
            h = g * rep + j
            s_cur = s_next
            if j + 1 < rep:
                s_next = qk(h + 1, kg)
            slope2 = jnp.exp2(jnp.full((1, 1), h + 1, jnp.int32).astype(_F32) * (-8.0 / n_q)) * LOG2_E
            s = s_cur - slope2 * dm
            m = jnp.max(s, axis=-1, keepdims=True)
            p = jnp.exp2(s - m)
            l = jnp.sum(p, axis=-1, keepdims=True)
            o = jnp.dot(p.astype(_BF16), vg, preferred_element_type=_F32)
            o_ref[h] = (o / l).astype(o_ref.dtype)
        return carry

    lax.fori_loop(0, n_kv, head_body, 0)


def _dsa(q_hm, qi_hm, kv_src, k_col, v_col, ki_src, ki_col, wi_src, wi_col, *, batch, n_qb, tq,
         q_row0, q_stride, kv_rows, kv_stride, l_ext, l_true, q_pos0, topk, n_kv, m_out, out_buf):
    n_q, _, dh = q_hm.shape
    n_i, _, d_idx = qi_hm.shape
    assert q_row0 % tq == 0 and q_stride % tq == 0 and l_ext % LANES == 0 and l_ext <= kv_rows
    qr0, qst = q_row0 // tq, q_stride // tq
    aliased = out_buf is not None
    kern = functools.partial(
        _dsa_kernel, tq=tq, l_ext=l_ext, l_true=l_true, q_pos0=q_pos0, topk=topk,
        idx_scale=float(n_i) ** -0.5 * float(d_idx) ** -0.5, n_kv=n_kv, dh=dh, d_idx=d_idx,
        aliased=int(aliased))
    qrow = lambda b, i: qr0 + b * qst + i
    in_specs = [
        pl.BlockSpec((n_q, tq, dh), lambda b, i: (0, qrow(b, i), 0)),
        pl.BlockSpec((kv_rows, n_kv * dh), lambda b, i: (b * kv_stride, k_col)),
        pl.BlockSpec((kv_rows, n_kv * dh), lambda b, i: (b * kv_stride, v_col)),
        pl.BlockSpec((n_i, tq, d_idx), lambda b, i: (0, qrow(b, i), 0)),
        pl.BlockSpec((tq, LANES), lambda b, i: (qrow(b, i), wi_col)),
        pl.BlockSpec((kv_rows, LANES), lambda b, i: (b * kv_stride, ki_col)),
    ]
    args = [q_hm, kv_src, kv_src, qi_hm, wi_src, ki_src]
    aliases = {}
    if aliased:
        in_specs.append(pl.BlockSpec(memory_space=pl.ANY))
        aliases = {len(args): 0}
        args.append(out_buf)
    return pl.pallas_call(
        kern,
        grid=(batch, n_qb),
        in_specs=in_specs,
        out_specs=pl.BlockSpec((n_q, tq, dh), lambda b, i: (0, qrow(b, i), 0)),
        out_shape=jax.ShapeDtypeStruct((n_q, m_out, dh), _BF16),
        scratch_shapes=[pltpu.VMEM((n_kv, l_ext, dh), _BF16), pltpu.VMEM((n_kv, l_ext, dh), _BF16),
                        pltpu.VMEM((l_ext, d_idx), _BF16), pltpu.VMEM((n_i, tq, LANES), _F32),
                        pltpu.VMEM((tq, l_ext), _F32), pltpu.VMEM((tq, l_ext), jnp.int32),
                        pltpu.VMEM((tq, l_ext), _F32)],
        input_output_aliases=aliases,
        compiler_params=_compiler_params(("parallel", "arbitrary")),
        name="dsa_attention",
    )(*args)


def _pad_rows(w, rows):
    return jnp.pad(w, ((0, rows - w.shape[0]), (0, 0)))


def _pad_cols(w, cols):
    return jnp.pad(w, ((0, 0), (0, cols - w.shape[1])))


def _to_block_diag_t(s, head):
    b, h = s.shape[:2]
    st = jnp.swapaxes(s, -1, -2).reshape(b, h // 2, 2, head, head).astype(_F32)
    z = jnp.zeros_like(st[:, :, 0])
    top = jnp.concatenate([st[:, :, 0], z], axis=-1)
    bot = jnp.concatenate([z, st[:, :, 1]], axis=-1)
    return jnp.concatenate([top, bot], axis=-2)


def _from_block_diag_t(sbd, head):
    b, hp = sbd.shape[:2]
    s0 = sbd[:, :, :head, :head]
    s1 = sbd[:, :, head:, head:]
    st = jnp.stack([s0, s1], axis=2).reshape(b, hp * 2, head, head)
    return jnp.swapaxes(st, -1, -2)


def kernel(x_prompt, x_sample, cache_k, cache_v, cache_kidx, state_shift, state_wkv, norm1_g, w_in, mu_shift, w0, w2, a0, a2, g2, k_k, k_a, r_k, lnx_g, lnx_b, w_br_a, w_br_b, w_out, norm2_g, w_ffn_gate, w_ffn_up, w_ffn_down, norm_f_g):
    depth = w_in.shape[0]
    assert depth == 1
    bp, tp, d_model = x_prompt.shape
    bs, ts, _ = x_sample.shape
    _, _, past, n_kv, dh = cache_k.shape
    d_idx = cache_kidx.shape[-1]
    d_shift = state_shift.shape[-1]
    _, _, h_a, head, _ = state_wkv.shape
    d_a = h_a * head
    r_w, r_a, r_g = w2.shape[1], a2.shape[1], g2.shape[1]
    n_q = w_br_b.shape[1] // dh
    d_in = w_in.shape[-1]
    h_i = (d_in - d_shift - n_q * dh - 2 * n_kv * dh - d_idx - 2 * d_model) // (d_idx + 1)
    d_ff = w_ffn_gate.shape[-1]
    kvw = n_kv * dh
    l = 0
    mp, ms = bp * tp, bs * ts
    m_tot = mp + ms
    xp2, xs2 = x_prompt.reshape(mp, d_model), x_sample.reshape(ms, d_model)
    tm_mm = _pick_tile(m_tot, 1024, 2 * SUBLANES)

    win = w_in[l]
    o = 0
    def take(n):
        nonlocal o
        blk = win[:, o:o + n]
        o += n
        return blk
    nwp, nap, ngp = _round_up(r_w, LANES), _round_up(r_a, LANES), _round_up(r_g, LANES)
    lp = nwp + nap + ngp
    w_rkv, w_lw, w_la, w_lg = take(3 * d_a), take(r_w), take(r_a), take(r_g)
    w_rkvl = jnp.concatenate([w_rkv, _pad_cols(w_lw, nwp), _pad_cols(w_la, nap), _pad_cols(w_lg, ngp)],
                             axis=1).astype(_BF16)
    w_q = take(n_q * dh).astype(_BF16)
    w_k, w_v = take(kvw), take(kvw)
    w_qi = take(h_i * d_idx).astype(_BF16)
    w_ki, w_wi = take(d_idx), take(h_i)
    w_kv = jnp.concatenate([w_k, w_v, _pad_cols(w_ki, LANES), _pad_cols(w_wi, LANES)], axis=1).astype(_BF16)
    w_gates = jnp.concatenate([take(d_model), take(d_model)], axis=1).astype(_BF16)

    h1 = _rmsnorm_cat(xp2, xs2, norm1_g[l], _BF16)
    ident = lambda accs, ex: [accs[0]]
    n_rkvl = 3 * d_a + lp
    (z_a,) = _matmul([h1], [w_rkvl], [], ident, [(n_rkvl, _F32, None)],
                     tm=tm_mm, tn=_pick_tile(n_rkvl, 1024, LANES), name="proj_rkv_lora")
    n_kvz = 2 * kvw + 2 * LANES
    (z_kv,) = _matmul([h1], [w_kv], [], ident, [(n_kvz, _F32, None)],
                      tm=tm_mm, tn=_pick_tile(n_kvz, 1280, LANES), name="proj_kv_idx")
    q_scale = float(dh) ** -0.5 * LOG2_E
    (q_hm,) = _matmul([h1], [w_q], [], lambda accs, ex: [accs[0] * q_scale],
                      [(n_q * dh, _BF16, dh)], tm=tm_mm, tn=_pick_tile(n_q * dh, 1024, dh), name="proj_q")
    (qi_hm,) = _matmul([h1], [w_qi], [], ident, [(h_i * d_idx, _BF16, d_idx)],
                       tm=tm_mm, tn=_pick_tile(h_i * d_idx, 1024, LANES), name="proj_qi")
    (gates,) = _matmul([h1], [w_gates], [], ident, [(2 * d_model, _BF16, None)],
                       tm=tm_mm, tn=_pick_tile(2 * d_model, 1024, LANES), name="proj_gates")
    kx, vx = z_kv[:, :kvw], z_kv[:, kvw:2 * kvw]
    ki_new = z_kv[:, 2 * kvw:2 * kvw + d_idx]

    mu = mu_shift[l]
    def lora_row(vec):
        return jnp.concatenate([
            jnp.pad(vec[3 * d_a:3 * d_a + r_w], (0, nwp - r_w)),
            jnp.pad(vec[3 * d_a + r_w:3 * d_a + r_w + r_a], (0, nap - r_a)),
            jnp.pad(vec[3 * d_a + r_w + r_a:], (0, ngp - r_g))])
    n_pairs = d_a // (2 * head)
    gw = min(WKV_PAIRS_PER_STEP, n_pairs) * 2 * head
    n_groups = d_a // gw
    def rkv_rows(vec):
        lead = vec.shape[:-1]
        x = vec[..., :3 * d_a].reshape(lead + (3, n_groups, gw))
        x = jnp.moveaxis(x, -3, -2)
        return jnp.pad(x, [(0, 0)] * (len(lead) + 1) + [(0, SUBLANES - 3), (0, 0)])
    row = lambda v: v[l].reshape(1, d_a).astype(_F32)
    params = dict(
        mu_rkv=rkv_rows(mu).reshape(n_groups * SUBLANES, gw),
        mu_lora=lora_row(mu).reshape(1, lp),
        w0=row(w0), a0=row(a0), k_k=row(k_k), k_a=row(k_a),
        r_k=r_k[l].reshape(1, d_a).astype(_F32), lnx_g=row(lnx_g), lnx_b=row(lnx_b),
        w2=_pad_rows(w2[l], nwp).astype(_BF16), a2=_pad_rows(a2[l], nap).astype(_BF16),
        g2=_pad_rows(g2[l], ngp).astype(_BF16))

    def shift_state(s):
        s = s[:, 0]
        return dict(rkv=rkv_rows(s), lora=jax.vmap(lora_row)(s).reshape(-1, 1, lp))

    wkv_kw = dict(head=head, d_a=d_a, lora_sizes=(nwp, nap, ngp), m_out=m_tot)
    y_a, sp_bd = _wkv(z_a, shift_state(jnp.zeros((bp, 1, d_shift), _F32)),
                      jnp.zeros((bp, n_pairs, 2 * head, 2 * head), _F32), params,
                      batch=bp, seq=tp, row0=0, out_buf=None, **wkv_kw)
    y_a, ss_bd = _wkv(z_a, shift_state(state_shift[l]), _to_block_diag_t(state_wkv[l], head), params,
                      batch=bs, seq=ts, row0=mp, out_buf=y_a, **wkv_kw)
    wkv_p = _from_block_diag_t(sp_bd, head)
    wkv_s = _from_block_diag_t(ss_bd, head)

    topk_p = min(TOPK_MAX, tp // 4)
    tq_p = _pick_tile(tp, 256, CHUNK)
    n_grp = max(1, min(DSA_GROUPS, tp // tq_p))
    grp = tp // n_grp
    wi_col = (2 * kvw + LANES) // LANES
    ki_col = 2 * kvw // LANES
    y_b = None
    for gi in range(n_grp):
        l_ext = (gi + 1) * grp
        kv_rows = l_ext
        while tp % kv_rows:
            kv_rows += grp
        y_b = _dsa(q_hm, qi_hm, z_kv, 0, 1, z_kv, ki_col, z_kv, wi_col,
                   batch=bp, n_qb=grp // tq_p, tq=tq_p, q_row0=gi * grp, q_stride=tp,
                   kv_rows=kv_rows, kv_stride=tp // kv_rows, l_ext=l_ext, l_true=l_ext,
                   q_pos0=gi * grp, topk=topk_p, n_kv=n_kv, m_out=m_tot, out_buf=y_b)
    ls = past + ts
    lps = _round_up(ls, LANES)
    def with_cache(cache, new, width):
        x = jnp.concatenate([cache.reshape(bs, past, width), new.reshape(bs, ts, width)], axis=1)
        return jnp.pad(x, ((0, 0), (0, lps - ls), (0, 0))).reshape(bs * lps, width)
    kv_s = jnp.concatenate([with_cache(cache_k[l], kx[mp:], kvw), with_cache(cache_v[l], vx[mp:], kvw)], axis=1)
    ki_s = _pad_cols(with_cache(cache_kidx[l], ki_new[mp:], d_idx), LANES)
    y_b = _dsa(q_hm, qi_hm, kv_s, 0, 1, ki_s, 0, z_kv, wi_col,
               batch=bs, n_qb=1, tq=ts, q_row0=mp, q_stride=ts, kv_rows=lps, kv_stride=1, l_ext=lps, l_true=ls,
               q_pos0=past, topk=min(TOPK_MAX, ls // 4), n_kv=n_kv, m_out=m_tot, out_buf=y_b)

    tn_mg = _pick_tile(d_model, 1024, LANES)
    nj_mg = d_model // tn_mg
    def merge_ep(accs, ex):
        return [_sigmoid(ex[0].astype(_F32)) * accs[0] + _sigmoid(ex[1].astype(_F32)) * accs[1]]
    (merged,) = _matmul([y_a, y_b], [w_br_a[l].astype(_BF16), w_br_b[l].astype(_BF16)],
                        [(gates, tn_mg, lambda i, j: (i, j)), (gates, tn_mg, lambda i, j: (i, nj_mg + j))],
                        merge_ep, [(d_model, _BF16, None)], tm=tm_mm, tn=tn_mg, name="branch_merge")
    tn_res = _pick_tile(d_model, 1024, LANES)
    w_out_b = w_out[l].astype(_BF16)
    resid_ep = lambda accs, ex: [ex[0] + accs[0]]
    tm_p = _pick_tile(mp, 1024, 2 * SUBLANES)
    tm_s = _pick_tile(math.gcd(mp, ms), 1024, 2 * SUBLANES)
    (x1,) = _matmul([merged], [w_out_b], [(xp2, tn_res, lambda i, j: (i, j))], resid_ep,
                    [(d_model, _F32, None)], tm=tm_p, tn=tn_res, row0=0, rows=mp, name="out_proj")
    (x1,) = _matmul([merged], [w_out_b], [(xs2, tn_res, lambda i, j: (i, j))], resid_ep,
                    [(d_model, _F32, None)], tm=tm_s, tn=tn_res, row0=mp, rows=ms, out_buf=x1,
                    name="out_proj")

    h2 = _rmsnorm_rows(x1, norm2_g[l], _BF16, 0, m_tot)
    tn_ff = 2 * LANES
    d_ffp = _round_up(d_ff, tn_ff)
    ff_pad_cols = jnp.zeros((d_model, d_ffp - d_ff), _BF16)
    wg = jnp.concatenate([w_ffn_gate[l].astype(_BF16), ff_pad_cols], axis=1)
    wu = jnp.concatenate([w_ffn_up[l].astype(_BF16), ff_pad_cols], axis=1)
    def swiglu_ep(accs, ex):
        return [accs[0] * _sigmoid(accs[0]) * accs[1]]
    (u,) = _matmul([h2], [wg, wu], [], swiglu_ep, [(d_ffp, _BF16, None)],
                   tm=_pick_tile(m_tot, 1536, 2 * SUBLANES), tn=tn_ff, name="ffn_up")
    wd = jnp.concatenate([w_ffn_down[l].astype(_BF16), jnp.zeros((d_ffp - d_ff, d_model), _BF16)], axis=0)
    tk_dn = d_ffp // 2 if (d_ffp // 2) % LANES == 0 else d_ffp
    x2 = _matmul_resid(u, wd, x1, tm=tm_mm, tn=_pick_tile(d_model, 512, LANES), tk=tk_dn, name="ffn_down")
    y_p = _rmsnorm_rows(x2, norm_f_g, _F32, 0, mp).reshape(bp, tp, d_model)
    y_s = _rmsnorm_rows(x2, norm_f_g, _F32, mp, ms).reshape(bs, ts, d_model)

    kx4 = lambda x, b, t: x.reshape(1, b, t, n_kv, dh)
    def zsh_last(rows):
        zl = z_a[rows]
        return jnp.concatenate([zl[:, :3 * d_a + r_w], zl[:, 3 * d_a + nwp:3 * d_a + nwp + r_a],
                                zl[:, 3 * d_a + nwp + nap:3 * d_a + nwp + nap + r_g]], axis=-1)
    last_p = jnp.arange(bp) * tp + (tp - 1)
    last_s = mp + jnp.arange(bs) * ts + (ts - 1)
    return (y_p, y_s,
            kx4(kx[:mp], bp, tp), kx4(vx[:mp], bp, tp), ki_new[:mp].reshape(1, bp, tp, d_idx),
            zsh_last(last_p).reshape(1, bp, 1, d_shift), wkv_p[None],
            kx4(kx[mp:], bs, ts), kx4(vx[mp:], bs, ts), ki_new[mp:].reshape(1, bs, ts, d_idx),
            zsh_last(last_s).reshape(1, bs, 1, d_shift), wkv_s[None])
```

```python
import functools
import math

import jax
import jax.numpy as jnp
from jax import lax
from jax.experimental import pallas as pl
from jax.experimental.pallas import tpu as pltpu

CHUNK = 64
NORM_EPS = 1e-6
GN_EPS = 64e-5
TOPK_MAX = 256

LANES = 128
SUBLANES = 8
V7X_VMEM_LIMIT_BYTES = 56 * 1024 * 1024

WKV_CHUNK = 64
WKV_PAIRS_PER_STEP = 4
DIST_EXCLUDED = 1e30
DSA_GROUPS = 8
LOG2_E = 1.4426950408889634

_F32 = jnp.float32
_BF16 = jnp.bfloat16


def _round_up(n, m):
    return (n + m - 1) // m * m


def _pick_tile(n, pref, align):
    if n <= pref:
        return n
    t = pref // align * align
    while t >= align:
        if n % t == 0:
            return t
        t -= align
    return n


def _compiler_params(semantics):
    return pltpu.CompilerParams(dimension_semantics=semantics,
                                vmem_limit_bytes=V7X_VMEM_LIMIT_BYTES)


def _sigmoid(x):
    return 1.0 / (1.0 + jnp.exp(-x))


def _rms(x, g):
    x = x.astype(_F32)
    ms = jnp.mean(x * x, axis=-1, keepdims=True)
    return x * lax.rsqrt(ms + NORM_EPS) * g


def _rmsnorm_cat_kernel(xa_ref, xb_ref, g_ref, o_ref, *, n_a):
    i = pl.program_id(0)

    @pl.when(i < n_a)
    def _():
        o_ref[...] = _rms(xa_ref[...], g_ref[...]).astype(o_ref.dtype)

    @pl.when(i >= n_a)
    def _():
        o_ref[...] = _rms(xb_ref[...], g_ref[...]).astype(o_ref.dtype)


def _rmsnorm_cat(xa, xb, g, out_dtype):
    (ma, d), mb = xa.shape, xb.shape[0]
    tm = _pick_tile(math.gcd(ma, mb), 512, SUBLANES)
    n_a = ma // tm
    return pl.pallas_call(
        functools.partial(_rmsnorm_cat_kernel, n_a=n_a),
        grid=((ma + mb) // tm,),
        in_specs=[pl.BlockSpec((tm, d), lambda i: (jnp.minimum(i, n_a - 1), 0)),
                  pl.BlockSpec((tm, d), lambda i: (jnp.maximum(i - n_a, 0), 0)),
                  pl.BlockSpec((1, d), lambda i: (0, 0))],
        out_specs=pl.BlockSpec((tm, d), lambda i: (i, 0)),
        out_shape=jax.ShapeDtypeStruct((ma + mb, d), out_dtype),
        compiler_params=_compiler_params(("arbitrary",)),
        name="rmsnorm_cat",
    )(xa, xb, g.reshape(1, d).astype(_F32))


def _rmsnorm_rows_kernel(x_ref, g_ref, o_ref):
    o_ref[...] = _rms(x_ref[...], g_ref[...]).astype(o_ref.dtype)


def _rmsnorm_rows(x, g, out_dtype, row0, rows):
    d = x.shape[1]
    tm = _pick_tile(math.gcd(row0, rows) if row0 else rows, 512, SUBLANES)
    rb0 = row0 // tm
    return pl.pallas_call(
        _rmsnorm_rows_kernel,
        grid=(rows // tm,),
        in_specs=[pl.BlockSpec((tm, d), lambda i: (rb0 + i, 0)),
                  pl.BlockSpec((1, d), lambda i: (0, 0))],
        out_specs=pl.BlockSpec((tm, d), lambda i: (i, 0)),
        out_shape=jax.ShapeDtypeStruct((rows, d), out_dtype),
        compiler_params=_compiler_params(("parallel",)),
        name="rmsnorm_rows",
    )(x, g.reshape(1, d).astype(_F32))


def _matmul_kernel(*refs, n_pairs, n_extras, n_outs, n_k, head_major, out_head_w, epilogue):
    n_x = len(head_major)
    x_refs = refs[:n_x]
    w_refs = refs[n_x:n_x + n_pairs]
    e_refs = refs[n_x + n_pairs:n_x + n_pairs + n_extras]
    o_refs = refs[n_x + n_pairs + n_extras:n_x + n_pairs + n_extras + n_outs]
    acc_refs = refs[n_x + n_pairs + n_extras + n_outs:]

    def load_x(i):
        if head_major[i]:
            xr = x_refs[i]
            return jnp.concatenate([xr[h] for h in range(xr.shape[0])], axis=1)
        return x_refs[i][...]

    def partial(i):
        return jnp.dot(load_x(i if n_x > 1 else 0), w_refs[i][...], preferred_element_type=_F32)

    def finish(accs):
        outs = epilogue(accs, [e[...] for e in e_refs])
        for o_ref, val, hw in zip(o_refs, outs, out_head_w):
            if hw:
                for h in range(o_ref.shape[0]):
                    o_ref[h] = val[:, h * hw:(h + 1) * hw].astype(o_ref.dtype)
            else:
                o_ref[...] = val.astype(o_ref.dtype)

    if n_k == 1:
        finish([partial(i) for i in range(n_pairs)])
        return

    k = pl.program_id(2)

    @pl.when(k == 0)
    def _():
        for i in range(n_pairs):
            acc_refs[i][...] = partial(i)

    @pl.when(k > 0)
    def _():
        for i in range(n_pairs):
            acc_refs[i][...] += partial(i)

    @pl.when(k == n_k - 1)
    def _():
        finish([a[...] for a in acc_refs])


def _matmul(xs, ws, extras, epilogue, outs, *, tm, tn, tk=None, col_major=False, name="matmul"):
    n_pairs = len(ws)
    assert len(xs) in (1, n_pairs)
    head_major = tuple(x.ndim == 3 for x in xs)
    m = xs[0].shape[1] if head_major[0] else xs[0].shape[0]
    kdim, n = ws[0].shape
    if tk is None:
        tk = kdim
    n_k = kdim // tk
    assert kdim % tk == 0 and m % tm == 0 and n % tn == 0
    n_j = n // tn
    tks = [tk if n_k > 1 else w.shape[0] for w in ws]
    assert all(w.shape == (kdim, n) for w in ws) or n_k == 1

    def order(f):
        return (lambda j, i, k: f(i, j, k)) if col_major else f

    in_specs = []
    for x, hm, tki in zip(xs, head_major, tks):
        if hm:
            assert n_k == 1
            in_specs.append(pl.BlockSpec((x.shape[0], tm, LANES), order(lambda i, j, k: (0, i, 0))))
        else:
            in_specs.append(pl.BlockSpec((tm, tki), order(lambda i, j, k: (i, k))))
    for w, tki in zip(ws, tks):
        in_specs.append(pl.BlockSpec((tki, tn), order(lambda i, j, k: (k, j))))
    for arr, cols, imap in extras:
        in_specs.append(pl.BlockSpec((tm, cols), order(lambda i, j, k, imap=imap: imap(i, j))))

    out_specs, out_shapes, out_head_w = [], [], []
    for n_cols, dtype, hw in outs:
        to = n_cols // n_j
        out_head_w.append(hw)
        if hw:
            assert to % hw == 0
            out_specs.append(pl.BlockSpec((to // hw, tm, hw), order(lambda i, j, k: (j, i, 0))))
            out_shapes.append(jax.ShapeDtypeStruct((n_cols // hw, m, hw), dtype))
        else:
            out_specs.append(pl.BlockSpec((tm, to), order(lambda i, j, k: (i, j))))
            out_shapes.append(jax.ShapeDtypeStruct((m, n_cols), dtype))

    scratch = [pltpu.VMEM((tm, tn), _F32) for _ in range(n_pairs)] if n_k > 1 else []
    kern = functools.partial(
        _matmul_kernel, n_pairs=n_pairs, n_extras=len(extras), n_outs=len(outs), n_k=n_k,
        head_major=head_major, out_head_w=tuple(out_head_w), epilogue=epilogue)
    return pl.pallas_call(
        kern,
        grid=(n_j, m // tm, n_k) if col_major else (m // tm, n_j, n_k),
        in_specs=in_specs,
        out_specs=out_specs,
        out_shape=out_shapes,
        scratch_shapes=scratch,
        compiler_params=_compiler_params(("parallel", "parallel", "arbitrary")),
        name=name,
    )(*xs, *ws, *[e[0] for e in extras])


def _matmul_resid_kernel(x_ref, w_ref, r_ref, o_ref, acc_ref, *, n_k):
    k = pl.program_id(1)
    j = pl.program_id(2)
    part = jnp.dot(x_ref[...], w_ref[...], preferred_element_type=_F32)
    if n_k == 1:
        o_ref[...] = r_ref[...] + part
        return

    @pl.when(k == 0)
    def _():
        acc_ref[j] = part

    @pl.when((k > 0) & (k < n_k - 1))
    def _():
        acc_ref[j] += part

    @pl.when(k == n_k - 1)
    def _():
        o_ref[...] = r_ref[...] + (acc_ref[j] + part)


def _matmul_resid(x, w, resid, *, tm, tn, tk, name):
    m, kdim = x.shape
    n = w.shape[1]
    assert m % tm == 0 and n % tn == 0 and kdim % tk == 0
    n_j, n_k = n // tn, kdim // tk
    last = lambda i, k, j: (i, jnp.where(k == n_k - 1, j, 0))
    return pl.pallas_call(
        functools.partial(_matmul_resid_kernel, n_k=n_k),
        grid=(m // tm, n_k, n_j),
        in_specs=[pl.BlockSpec((tm, tk), lambda i, k, j: (i, k)),
                  pl.BlockSpec((tk, tn), lambda i, k, j: (k, j)),
                  pl.BlockSpec((tm, tn), last)],
        out_specs=pl.BlockSpec((tm, tn), last),
        out_shape=jax.ShapeDtypeStruct((m, n), _F32),
        scratch_shapes=[pltpu.VMEM((n_j, tm, tn), _F32)],
        compiler_params=_compiler_params(("parallel", "arbitrary", "arbitrary")),
        name=name,
    )(x, w, resid)


def _shift_rows(z, prev_row):
    rolled = pltpu.roll(z, 1, 0)
    row = lax.broadcasted_iota(jnp.int32, z.shape, 0)
    return jnp.where(row == 0, prev_row, rolled)


def _cumsum_rows(x):
    n = x.shape[0]
    row = lax.broadcasted_iota(jnp.int32, x.shape, 0)
    s = 1
    while s < n:
        x = x + jnp.where(row >= s, pltpu.roll(x, s, 0), 0.0)
        s *= 2
    return x


def bf(x):
    return x.astype(_BF16)


def _bdot(a, b):
    assert a.dtype == _BF16 and b.dtype == _BF16
    return jnp.dot(a, b, preferred_element_type=_F32)


def _bdot_nt(a, b):
    assert a.dtype == _BF16 and b.dtype == _BF16
    return lax.dot_general(a, b, (((1,), (1,)), ((), ())), preferred_element_type=_F32)


def _wkv_kernel(*refs, n_chunks, head, n_pp, aliased):
    (zr_ref, zk_ref, zv_ref, zl_ref, sprev_ref, lprev_ref, s0_ref,
     mu_ref, mul_ref, w0_ref, a0_ref, kkp_ref, kap_ref, rk_ref, lg_ref, lb_ref,
     w2_ref, a2_ref, g2_ref) = refs[:19]
    y_ref, sout_ref, st_ref, prev_ref, lprev_scr = refs[19 + aliased:]
    c_idx = pl.program_id(2)
    C = WKV_CHUNK
    P = 2 * head
    assert P == 2 * C
    tb = zr_ref.shape[0]
    n_sub = tb // C

    @pl.when(c_idx == 0)
    def _():
        st_ref[...] = s0_ref[0]
        prev_ref[...] = sprev_ref[0, 0]
        lprev_scr[...] = lprev_ref[0]

    lane = lax.broadcasted_iota(jnp.int32, (1, P), 1)
    m0 = (lane < head).astype(_F32)
    m1 = 1.0 - m0
    r2 = lax.broadcasted_iota(jnp.int32, (P, P), 0)
    c2 = lax.broadcasted_iota(jnp.int32, (P, P), 1)
    same = (r2 // C) == (c2 // C)
    tril_s = jnp.where(same & (c2 < r2), 1.0, 0.0)
    tril_i = jnp.where(same & (c2 <= r2), 1.0, 0.0)
    eye = jnp.where(r2 == c2, 1.0, 0.0)

    def stack(x):
        return jnp.concatenate([x * m0, x * m1], axis=0)

    def head_sum(x):
        s0 = jnp.sum(x * m0, axis=-1, keepdims=True)
        s1 = jnp.sum(x * m1, axis=-1, keepdims=True)
        return s0 * m0 + s1 * m1

    zl = zl_ref[...]
    xl = zl + (_shift_rows(zl, lprev_scr[...]) - zl) * mul_ref[...]
    lprev_scr[...] = zl[tb - 1:tb, :]
    nw = w2_ref.shape[0]
    na = a2_ref.shape[0]
    tw = jnp.tanh(xl[:, 0:nw]).astype(_BF16)
    xa = xl[:, nw:nw + na].astype(_BF16)
    sg = _sigmoid(xl[:, nw + na:]).astype(_BF16)

    pairs = []
    for pi in range(n_pp):
        ls = slice(pi * P, (pi + 1) * P)
        zr, zk, zv = zr_ref[:, ls], zk_ref[:, ls], zv_ref[:, ls]
        r = zr + (_shift_rows(zr, prev_ref[0:1, ls]) - zr) * mu_ref[0:1, ls]
        k = zk + (_shift_rows(zk, prev_ref[1:2, ls]) - zk) * mu_ref[1:2, ls]
        v = zv + (_shift_rows(zv, prev_ref[2:3, ls]) - zv) * mu_ref[2:3, ls]
        prev_ref[0:1, ls] = zr[tb - 1:tb, :]
        prev_ref[1:2, ls] = zk[tb - 1:tb, :]
        prev_ref[2:3, ls] = zv[tb - 1:tb, :]
        u = -(w0_ref[:, ls] + jnp.dot(tw, w2_ref[:, ls], preferred_element_type=_F32))
        softplus = jnp.maximum(u, 0.0) + jnp.log(1.0 + jnp.exp(-jnp.abs(u)))
        logw = -jnp.exp(-softplus - 0.5)
        a = _sigmoid(a0_ref[:, ls] + jnp.dot(xa, a2_ref[:, ls], preferred_element_type=_F32))
        g = jnp.dot(sg, g2_ref[:, ls], preferred_element_type=_F32)
        kk = k * kkp_ref[:, ls]
        kk = kk / jnp.maximum(jnp.sqrt(head_sum(kk * kk)), 1e-12)
        kh = k * (1.0 + (a - 1.0) * kap_ref[:, ls])
        pairs.append(dict(ls=ls, r=r, v=v, logw=logw, kk=kk, kh=kh, beta=kk * a, g=g,
                          bonus=head_sum(r * kh * rk_ref[:, ls]) * v))

    states = [st_ref[pi] for pi in range(n_pp)]
    group = n_sub
    for sc0 in range(0, n_sub, group):
        items = []
        for sc in range(sc0, sc0 + group):
            rs = slice(sc * C, (sc + 1) * C)
            for pi, pr in enumerate(pairs):
                lw = pr["logw"][rs]
                cum = _cumsum_rows(lw)
                cum_c = cum[C - 1:C, :]
                e_neg = jnp.exp(-cum)
                e_end = jnp.exp(cum_c - cum)
                r2s = stack(pr["r"][rs] * jnp.exp(cum))
                items.append(dict(
                    pi=pi, rs=rs, r2s=r2s,
                    ar=bf(jnp.concatenate([stack(-pr["kk"][rs] * jnp.exp(cum - lw)), r2s], axis=0)),
                    bk=bf(jnp.concatenate([stack(pr["beta"][rs] * e_neg), stack(pr["kh"][rs] * e_neg)],
                                          axis=0)),
                    v2s=bf(stack(pr["v"][rs])),
                    bh2t=bf(stack(pr["beta"][rs] * e_end).T),
                    kh2t=bf(stack(pr["kh"][rs] * e_end).T),
                    dec_col=jnp.sum(eye * jnp.exp(cum_c), axis=-1, keepdims=True)))
        for it in items:
            nn = _bdot_nt(it["ar"], it["bk"])
            n_ba = nn[:P, :P] * tril_s
            it["n_ka"] = bf(nn[:P, P:] * tril_s)
            it["p_br"] = bf(nn[P:, :P] * tril_i)
            it["p_kr"] = bf(nn[P:, P:] * tril_i)
            it["t_inv"] = eye + n_ba
            it["pw"] = bf(n_ba)
        for it in items:
            it["pw"] = bf(_bdot(it["pw"], it["pw"]))
        s = 2
        while s < C:
            for it in items:
                sq = _bdot(it["pw"], jnp.concatenate([it["pw"], bf(it["t_inv"])], axis=1))
                it["pw"] = bf(sq[:, :P])
                it["t_inv"] = it["t_inv"] + sq[:, P:]
            s *= 2
        for it in items:
            it["nkv"] = bf(_bdot(it["n_ka"], it["v2s"]))
        for it in items:
            it["tt"] = bf(_bdot(bf(it["t_inv"]), jnp.concatenate([it["ar"][:P], it["nkv"]], axis=1)))
        for it in items:
            it["mg"] = _bdot(it["bh2t"], it["tt"])
        for it in items:
            it["pg"] = _bdot(it["p_br"], it["tt"])
        for it in items:
            it["g_c"] = it["mg"][:, P:] + _bdot(it["kh2t"], it["v2s"])
        for it in items:
            it["yg"] = it["pg"][:, P:] + _bdot(it["p_kr"], it["v2s"])
        for it in items:
            pi, rs, pr = it["pi"], it["rs"], pairs[it["pi"]]
            ls = pr["ls"]
            st = states[pi]
            st_b = bf(st)
            y2 = _bdot(bf(it["r2s"] + it["pg"][:, :P]), st_b) + it["yg"]
            states[pi] = it["dec_col"] * st + _bdot(bf(it["mg"][:, :P]), st_b) + it["g_c"]
            y = y2[0:C, :] + y2[C:2 * C, :]
            mean = head_sum(y) * (1.0 / head)
            yc = y - mean
            var = head_sum(yc * yc) * (1.0 / head)
            yn = yc * lax.rsqrt(var + GN_EPS) * lg_ref[:, ls] + lb_ref[:, ls]
            y_ref[rs, ls] = ((yn + pr["bonus"][rs]) * pr["g"][rs]).astype(y_ref.dtype)
    for pi in range(n_pp):
        st_ref[pi] = states[pi]

    @pl.when(c_idx == n_chunks - 1)
    def _():
        sout_ref[0] = st_ref[...]


def _wkv(z, shift_prev, s0_bd, params, *, batch, seq, row0, head, d_a, lora_sizes, m_out, out_buf):
    pw = 2 * head
    n_pairs = d_a // pw
    n_pp = min(WKV_PAIRS_PER_STEP, n_pairs)
    gw = n_pp * pw
    n_groups = n_pairs // n_pp
    lp = sum(lora_sizes)
    assert (3 * d_a) % lp == 0
    tb = _pick_tile(seq, 256, WKV_CHUNK)
    n_chunks = seq // tb
    assert row0 % tb == 0
    rb0 = row0 // tb
    rpb = seq // tb
    aliased = out_buf is not None

    def rows(b, c):
        return rb0 + b * rpb + c

    in_specs = [
        pl.BlockSpec((tb, gw), lambda b, p, c: (rows(b, c), p)),
        pl.BlockSpec((tb, gw), lambda b, p, c: (rows(b, c), n_groups + p)),
        pl.BlockSpec((tb, gw), lambda b, p, c: (rows(b, c), 2 * n_groups + p)),
        pl.BlockSpec((tb, lp), lambda b, p, c: (rows(b, c), 3 * d_a // lp)),
        pl.BlockSpec((1, 1, SUBLANES, gw), lambda b, p, c: (b, p, 0, 0)),
        pl.BlockSpec((1, 1, lp), lambda b, p, c: (b, 0, 0)),
        pl.BlockSpec((1, n_pp, pw, pw), lambda b, p, c: (b, p, 0, 0)),
        pl.BlockSpec((SUBLANES, gw), lambda b, p, c: (p, 0)),
        pl.BlockSpec((1, lp), lambda b, p, c: (0, 0)),
    ]
    for _ in range(7):
        in_specs.append(pl.BlockSpec((1, gw), lambda b, p, c: (0, p)))
    nw, na, ng = lora_sizes
    in_specs += [pl.BlockSpec((nw, gw), lambda b, p, c: (0, p)),
                 pl.BlockSpec((na, gw), lambda b, p, c: (0, p)),
                 pl.BlockSpec((ng, gw), lambda b, p, c: (0, p))]
    args = [z, z, z, z, shift_prev["rkv"], shift_prev["lora"], s0_bd,
            params["mu_rkv"], params["mu_lora"], params["w0"], params["a0"], params["k_k"],
            params["k_a"], params["r_k"], params["lnx_g"], params["lnx_b"],
            params["w2"], params["a2"], params["g2"]]
    aliases = {}
    if aliased:
        in_specs.append(pl.BlockSpec(memory_space=pl.ANY))
        aliases = {len(args): 0}
        args.append(out_buf)
    out_specs = [pl.BlockSpec((tb, gw), lambda b, p, c: (rows(b, c), p)),
                 pl.BlockSpec((1, n_pp, pw, pw), lambda b, p, c: (b, p, 0, 0))]
    out_shape = [jax.ShapeDtypeStruct((m_out, d_a), _BF16),
                 jax.ShapeDtypeStruct((batch, n_pairs, pw, pw), _F32)]
    kern = functools.partial(_wkv_kernel, n_chunks=n_chunks, head=head, n_pp=n_pp, aliased=int(aliased))
    return pl.pallas_call(
        kern,
        grid=(batch, n_groups, n_chunks),
        in_specs=in_specs,
        out_specs=out_specs,
        out_shape=out_shape,
        scratch_shapes=[pltpu.VMEM((n_pp, pw, pw), _F32),
                        pltpu.VMEM((SUBLANES, gw), _F32),
                        pltpu.VMEM((1, lp), _F32)],
        input_output_aliases=aliases,
        compiler_params=_compiler_params(("parallel", "parallel", "arbitrary")),
        name="wkv7_chunked",
    )(*args)


def _dsa_kernel(*refs, tq, l_ext, l_true, q_pos0, topk, idx_scale, n_kv, dh, d_idx, aliased):
    q_ref, k_ref, v_ref, qi_ref, wi_ref, ki_ref = refs[:6]
    o_ref, kbf, vbf, kibf, wib, isc_scr, key_scr, dm_scr = refs[6 + aliased:]
    qb = pl.program_id(1)
    n_q = q_ref.shape[0]
    n_i = qi_ref.shape[0]
    rep = n_q // n_kv

    @pl.when(qb == 0)
    def _():
        for g in range(n_kv):
            kbf[g] = k_ref[0:l_ext, g * dh:(g + 1) * dh].astype(_BF16)
            vbf[g] = v_ref[0:l_ext, g * dh:(g + 1) * dh].astype(_BF16)
        kibf[...] = ki_ref[0:l_ext, 0:d_idx].astype(_BF16)

    wi = wi_ref[...]
    for hi in range(n_i):
        wib[hi] = jnp.broadcast_to(wi[:, hi:hi + 1], (tq, LANES))
    isc_scr[...] = jnp.zeros((tq, l_ext), _F32)
    ki = kibf[...]

    hs = max(1, min(4, 512 // tq))
    assert n_i % hs == 0

    def idx_body(i, carry):
        qs = qi_ref[pl.ds(i * hs, hs)].reshape(hs * tq, d_idx)
        sc = lax.dot_general(qs, ki, (((1,), (1,)), ((), ())), preferred_element_type=_F32)
        acc = isc_scr[...]
        for j in range(hs):
            acc = acc + (jnp.maximum(sc[j * tq:(j + 1) * tq], 0.0)
                         * jnp.tile(wib[i * hs + j], (1, l_ext // LANES)))
        isc_scr[...] = acc
        return carry

    lax.fori_loop(0, n_i // hs, idx_body, 0)

    q_pos = q_pos0 + qb * tq + lax.broadcasted_iota(jnp.int32, (tq, l_ext), 0)
    k_pos = lax.broadcasted_iota(jnp.int32, (tq, l_ext), 1)
    cs = CHUNK.bit_length() - 1
    assert 1 << cs == CHUNK
    allowed = ((k_pos >> cs) <= (q_pos >> cs)) & (k_pos < l_true)
    bits = pltpu.bitcast(isc_scr[...] * idx_scale, jnp.int32)
    int_min = jnp.int32(-2 ** 31)
    key_scr[...] = jnp.where(allowed, jnp.where(bits < 0, bits ^ jnp.int32(0x7FFFFFFF), bits), int_min)

    def count_ge(cand):
        return jnp.sum(jnp.where(key_scr[...] >= cand, 1.0, 0.0), axis=-1, keepdims=True)

    kf = jnp.float32(topk)
    zero = jnp.zeros((tq, 1), jnp.int32)
    thr = jnp.where(count_ge(zero) >= kf, zero, jnp.full((tq, 1), int_min, jnp.int32))

    def thr_body(i, thr):
        cand = thr + (jnp.int32(1) << (jnp.int32(30) - i))
        return jnp.where(count_ge(cand) >= kf, cand, thr)

    thr = lax.fori_loop(0, 31, thr_body, thr)
    key = key_scr[...]
    n_gt = jnp.sum(jnp.where(key > thr, 1.0, 0.0), axis=-1, keepdims=True)
    n_eq = jnp.sum(jnp.where(key == thr, 1.0, 0.0), axis=-1, keepdims=True)
    need = kf - n_gt
    n_bits = max(1, (l_ext - 1).bit_length())
    full = jnp.full((tq, 1), 1 << n_bits, jnp.int32)

    def tie_bound():
        def count_tie_below(bound):
            hit = jnp.where(key_scr[...] == thr, jnp.where(k_pos < bound, 1.0, 0.0), 0.0)
            return jnp.sum(hit, axis=-1, keepdims=True)

        def tbody(i, bound):
            cand = bound + (jnp.int32(1) << (jnp.int32(n_bits - 1) - i))
            return jnp.where(count_tie_below(cand) <= need, cand, bound)

        return lax.fori_loop(0, n_bits, tbody, zero)

    has_excess_ties = jnp.max(n_eq - need) > 0.0
    bound = lax.cond(has_excess_ties, tie_bound, lambda: full)
    sel = allowed & ((key > thr) | ((key == thr) & (k_pos < bound)))
    dm_scr[...] = jnp.where(sel, jnp.abs(q_pos - k_pos).astype(_F32), DIST_EXCLUDED)

    def qk(h, kg):
        return lax.dot_general(q_ref[h], kg, (((1,), (1,)), ((), ())), preferred_element_type=_F32)

    def head_body(g, carry):
        kg, vg = kbf[g], vbf[g]
        dm = dm_scr[...]
        s_next = qk(g * rep, kg)
        for j in range(rep):
            h = g * rep + j
            s_cur = s_next
            if j + 1 < rep:
                s_next = qk(h + 1, kg)
            slope2 = jnp.exp2(jnp.full((1, 1), h + 1, jnp.int32).astype(_F32) * (-8.0 / n_q)) * LOG2_E
            s = s_cur - slope2 * dm
            m = jnp.max(s, axis=-1, keepdims=True)
            p = jnp.exp2(s - m)
            l = jnp.sum(p, axis=-1, keepdims=True)
            o = jnp.dot(p.astype(_BF16), vg, preferred_element_type=_F32)
            o_ref[h] = (o / l).astype(o_ref.dtype)
        return carry

    lax.fori_loop(0, n_kv, head_body, 0)


def _dsa(q_hm, qi_hm, kv_src, k_col, v_col, ki_src, ki_col, wi_src, wi_col, *, batch, n_qb, tq,
         q_row0, q_stride, kv_rows, kv_stride, l_ext, l_true, q_pos0, topk, n_kv, m_out, out_buf):
    n_q, _, dh = q_hm.shape
    n_i, _, d_idx = qi_hm.shape
    assert q_row0 % tq == 0 and q_stride % tq == 0 and l_ext % LANES == 0 and l_ext <= kv_rows
    qr0, qst = q_row0 // tq, q_stride // tq
    aliased = out_buf is not None
    kern = functools.partial(
        _dsa_kernel, tq=tq, l_ext=l_ext, l_true=l_true, q_pos0=q_pos0, topk=topk,
        idx_scale=float(n_i) ** -0.5 * float(d_idx) ** -0.5, n_kv=n_kv, dh=dh, d_idx=d_idx,
        aliased=int(aliased))
    qrow = lambda b, i: qr0 + b * qst + i
    in_specs = [
        pl.BlockSpec((n_q, tq, dh), lambda b, i: (0, qrow(b, i), 0)),
        pl.BlockSpec((kv_rows, n_kv * dh), lambda b, i: (b * kv_stride, k_col)),
        pl.BlockSpec((kv_rows, n_kv * dh), lambda b, i: (b * kv_stride, v_col)),
        pl.BlockSpec((n_i, tq, d_idx), lambda b, i: (0, qrow(b, i), 0)),
        pl.BlockSpec((tq, LANES), lambda b, i: (qrow(b, i), wi_col)),
        pl.BlockSpec((kv_rows, LANES), lambda b, i: (b * kv_stride, ki_col)),
    ]
    args = [q_hm, kv_src, kv_src, qi_hm, wi_src, ki_src]
    aliases = {}
    if aliased:
        in_specs.append(pl.BlockSpec(memory_space=pl.ANY))
        aliases = {len(args): 0}
        args.append(out_buf)
    return pl.pallas_call(
        kern,
        grid=(batch, n_qb),
        in_specs=in_specs,
        out_specs=pl.BlockSpec((n_q, tq, dh), lambda b, i: (0, qrow(b, i), 0)),
        out_shape=jax.ShapeDtypeStruct((n_q, m_out, dh), _BF16),
        scratch_shapes=[pltpu.VMEM((n_kv, l_ext, dh), _BF16), pltpu.VMEM((n_kv, l_ext, dh), _BF16),
                        pltpu.VMEM((l_ext, d_idx), _BF16), pltpu.VMEM((n_i, tq, LANES), _F32),
                        pltpu.VMEM((tq, l_ext), _F32), pltpu.VMEM((tq, l_ext), jnp.int32),
                        pltpu.VMEM((tq, l_ext), _F32)],
        input_output_aliases=aliases,
        compiler_params=_compiler_params(("parallel", "arbitrary")),
        name="dsa_attention",
    )(*args)


def _pad_rows(w, rows):
    return jnp.pad(w, ((0, rows - w.shape[0]), (0, 0)))


def _pad_cols(w, cols):
    return jnp.pad(w, ((0, 0), (0, cols - w.shape[1])))


def _to_block_diag_t(s, head):
    b, h = s.shape[:2]
    st = jnp.swapaxes(s, -1, -2).reshape(b, h // 2, 2, head, head).astype(_F32)
    z = jnp.zeros_like(st[:, :, 0])
    top = jnp.concatenate([st[:, :, 0], z], axis=-1)
    bot = jnp.concatenate([z, st[:, :, 1]], axis=-1)
    return jnp.concatenate([top, bot], axis=-2)


def _from_block_diag_t(sbd, head):
    b, hp = sbd.shape[:2]
    s0 = sbd[:, :, :head, :head]
    s1 = sbd[:, :, head:, head:]
    st = jnp.stack([s0, s1], axis=2).reshape(b, hp * 2, head, head)
    return jnp.swapaxes(st, -1, -2)


def kernel(x_prompt, x_sample, cache_k, cache_v, cache_kidx, state_shift, state_wkv, norm1_g, w_in, mu_shift, w0, w2, a0, a2, g2, k_k, k_a, r_k, lnx_g, lnx_b, w_br_a, w_br_b, w_out, norm2_g, w_ffn_gate, w_ffn_up, w_ffn_down, norm_f_g):
    depth = w_in.shape[0]
    assert depth == 1
    bp, tp, d_model = x_prompt.shape
    bs, ts, _ = x_sample.shape
    _, _, past, n_kv, dh = cache_k.shape
    d_idx = cache_kidx.shape[-1]
    d_shift = state_shift.shape[-1]
    _, _, h_a, head, _ = state_wkv.shape
    d_a = h_a * head
    r_w, r_a, r_g = w2.shape[1], a2.shape[1], g2.shape[1]
    n_q = w_br_b.shape[1] // dh
    d_in = w_in.shape[-1]
    h_i = (d_in - d_shift - n_q * dh - 2 * n_kv * dh - d_idx - 2 * d_model) // (d_idx + 1)
    d_ff = w_ffn_gate.shape[-1]
    kvw = n_kv * dh
    l = 0
    mp, ms = bp * tp, bs * ts
    m_tot = mp + ms
    xp2, xs2 = x_prompt.reshape(mp, d_model), x_sample.reshape(ms, d_model)
    tm_mm = _pick_tile(m_tot, 1024, 2 * SUBLANES)

    win = w_in[l]
    o = 0
    def take(n):
        nonlocal o
        blk = win[:, o:o + n]
        o += n
        return blk
    nwp, nap, ngp = _round_up(r_w, LANES), _round_up(r_a, LANES), _round_up(r_g, LANES)
    lp = nwp + nap + ngp
    w_rkv, w_lw, w_la, w_lg = take(3 * d_a), take(r_w), take(r_a), take(r_g)
    w_rkvl = jnp.concatenate([w_rkv, _pad_cols(w_lw, nwp), _pad_cols(w_la, nap), _pad_cols(w_lg, ngp)],
                             axis=1).astype(_BF16)
    w_q = take(n_q * dh).astype(_BF16)
    w_k, w_v = take(kvw), take(kvw)
    w_qi = take(h_i * d_idx).astype(_BF16)
    w_ki, w_wi = take(d_idx), take(h_i)
    w_kv = jnp.concatenate([w_k, w_v, _pad_cols(w_ki, LANES), _pad_cols(w_wi, LANES)], axis=1).astype(_BF16)
    w_gates = jnp.concatenate([take(d_model), take(d_model)], axis=1).astype(_BF16)

    h1 = _rmsnorm_cat(xp2, xs2, norm1_g[l], _BF16)
    ident = lambda accs, ex: [accs[0]]
    n_rkvl = 3 * d_a + lp
    (z_a,) = _matmul([h1], [w_rkvl], [], ident, [(n_rkvl, _F32, None)],
                     tm=tm_mm, tn=_pick_tile(n_rkvl, 1024, LANES), name="proj_rkv_lora")
    n_kvz = 2 * kvw + 2 * LANES
    (z_kv,) = _matmul([h1], [w_kv], [], ident, [(n_kvz, _F32, None)],
                      tm=tm_mm, tn=_pick_tile(n_kvz, 1280, LANES), name="proj_kv_idx")
    q_scale = float(dh) ** -0.5 * LOG2_E
    (q_hm,) = _matmul([h1], [w_q], [], lambda accs, ex: [accs[0] * q_scale],
                      [(n_q * dh, _BF16, dh)], tm=tm_mm, tn=_pick_tile(n_q * dh, 1024, dh), name="proj_q")
    (qi_hm,) = _matmul([h1], [w_qi], [], ident, [(h_i * d_idx, _BF16, d_idx)],
                       tm=tm_mm, tn=_pick_tile(h_i * d_idx, 1024, LANES), name="proj_qi")
    (gates,) = _matmul([h1], [w_gates], [], ident, [(2 * d_model, _BF16, None)],
                       tm=tm_mm, tn=_pick_tile(2 * d_model, 1024, LANES), name="proj_gates")
    kx, vx = z_kv[:, :kvw], z_kv[:, kvw:2 * kvw]
    ki_new = z_kv[:, 2 * kvw:2 * kvw + d_idx]

    mu = mu_shift[l]
    def lora_row(vec):
        return jnp.concatenate([
            jnp.pad(vec[3 * d_a:3 * d_a + r_w], (0, nwp - r_w)),
            jnp.pad(vec[3 * d_a + r_w:3 * d_a + r_w + r_a], (0, nap - r_a)),
            jnp.pad(vec[3 * d_a + r_w + r_a:], (0, ngp - r_g))])
    n_pairs = d_a // (2 * head)
    gw = min(WKV_PAIRS_PER_STEP, n_pairs) * 2 * head
    n_groups = d_a // gw
    def rkv_rows(vec):
        lead = vec.shape[:-1]
        x = vec[..., :3 * d_a].reshape(lead + (3, n_groups, gw))
        x = jnp.moveaxis(x, -3, -2)
        return jnp.pad(x, [(0, 0)] * (len(lead) + 1) + [(0, SUBLANES - 3), (0, 0)])
    row = lambda v: v[l].reshape(1, d_a).astype(_F32)
    params = dict(
        mu_rkv=rkv_rows(mu).reshape(n_groups * SUBLANES, gw),
        mu_lora=lora_row(mu).reshape(1, lp),
        w0=row(w0), a0=row(a0), k_k=row(k_k), k_a=row(k_a),
        r_k=r_k[l].reshape(1, d_a).astype(_F32), lnx_g=row(lnx_g), lnx_b=row(lnx_b),
        w2=_pad_rows(w2[l], nwp).astype(_BF16), a2=_pad_rows(a2[l], nap).astype(_BF16),
        g2=_pad_rows(g2[l], ngp).astype(_BF16))

    def shift_state(s):
        s = s[:, 0]
        return dict(rkv=rkv_rows(s), lora=jax.vmap(lora_row)(s).reshape(-1, 1, lp))

    wkv_kw = dict(head=head, d_a=d_a, lora_sizes=(nwp, nap, ngp), m_out=m_tot)
    y_a, sp_bd = _wkv(z_a, shift_state(jnp.zeros((bp, 1, d_shift), _F32)),
                      jnp.zeros((bp, n_pairs, 2 * head, 2 * head), _F32), params,
                      batch=bp, seq=tp, row0=0, out_buf=jnp.zeros((m_tot, d_a), _BF16), **wkv_kw)
    y_a, ss_bd = _wkv(z_a, shift_state(state_shift[l]), _to_block_diag_t(state_wkv[l], head), params,
                      batch=bs, seq=ts, row0=mp, out_buf=y_a, **wkv_kw)
    wkv_p = _from_block_diag_t(sp_bd, head)
    wkv_s = _from_block_diag_t(ss_bd, head)

    topk_p = min(TOPK_MAX, tp // 4)
    tq_p = _pick_tile(tp, 256, CHUNK)
    n_grp = max(1, min(DSA_GROUPS, tp // tq_p))
    grp = tp // n_grp
    wi_col = (2 * kvw + LANES) // LANES
    ki_col = 2 * kvw // LANES
    y_b = jnp.zeros((n_q, m_tot, dh), _BF16)
    for gi in range(n_grp):
        l_ext = (gi + 1) * grp
        kv_rows = l_ext
        while tp % kv_rows:
            kv_rows += grp
        y_b = _dsa(q_hm, qi_hm, z_kv, 0, 1, z_kv, ki_col, z_kv, wi_col,
                   batch=bp, n_qb=grp // tq_p, tq=tq_p, q_row0=gi * grp, q_stride=tp,
                   kv_rows=kv_rows, kv_stride=tp // kv_rows, l_ext=l_ext, l_true=l_ext,
                   q_pos0=gi * grp, topk=topk_p, n_kv=n_kv, m_out=m_tot, out_buf=y_b)
    ls = past + ts
    lps = _round_up(ls, LANES)
    def with_cache(cache, new, width):
        x = jnp.concatenate([cache.reshape(bs, past, width), new.reshape(bs, ts, width)], axis=1)
        return jnp.pad(x, ((0, 0), (0, lps - ls), (0, 0))).reshape(bs * lps, width)
    kv_s = jnp.concatenate([with_cache(cache_k[l], kx[mp:], kvw), with_cache(cache_v[l], vx[mp:], kvw)], axis=1)
    ki_s = _pad_cols(with_cache(cache_kidx[l], ki_new[mp:], d_idx), LANES)
    y_b = _dsa(q_hm, qi_hm, kv_s, 0, 1, ki_s, 0, z_kv, wi_col,
               batch=bs, n_qb=1, tq=ts, q_row0=mp, q_stride=ts, kv_rows=lps, kv_stride=1, l_ext=lps, l_true=ls,
               q_pos0=past, topk=min(TOPK_MAX, ls // 4), n_kv=n_kv, m_out=m_tot, out_buf=y_b)

    tn_mg = _pick_tile(d_model, 1024, LANES)
    nj_mg = d_model // tn_mg
    def merge_ep(accs, ex):
        return [_sigmoid(ex[0].astype(_F32)) * accs[0] + _sigmoid(ex[1].astype(_F32)) * accs[1]]
    (merged,) = _matmul([y_a, y_b], [w_br_a[l].astype(_BF16), w_br_b[l].astype(_BF16)],
                        [(gates, tn_mg, lambda i, j: (i, j)), (gates, tn_mg, lambda i, j: (i, nj_mg + j))],
                        merge_ep, [(d_model, _BF16, None)], tm=tm_mm, tn=tn_mg, name="branch_merge")
    tn_res = _pick_tile(d_model, 1024, LANES)
    tm_io = _pick_tile(math.gcd(mp, ms), 512, 2 * SUBLANES)
    n_io_p = mp // tm_io
    def resid2_ep(accs, ex):
        return [jnp.where(pl.program_id(1) < n_io_p, ex[0], ex[1]) + accs[0]]
    (x1,) = _matmul([merged], [w_out[l].astype(_BF16)],
                    [(xp2, tn_res, lambda i, j: (jnp.minimum(i, n_io_p - 1), j)),
                     (xs2, tn_res, lambda i, j: (jnp.maximum(i - n_io_p, 0), j))],
                    resid2_ep, [(d_model, _F32, None)], tm=tm_io, tn=tn_res, col_major=True,
                    name="out_proj")

    h2 = _rmsnorm_rows(x1, norm2_g[l], _BF16, 0, m_tot)
    tn_ff = 2 * LANES
    d_ffp = _round_up(d_ff, tn_ff)
    ff_pad_cols = jnp.zeros((d_model, d_ffp - d_ff), _BF16)
    wg = jnp.concatenate([w_ffn_gate[l].astype(_BF16), ff_pad_cols], axis=1)
    wu = jnp.concatenate([w_ffn_up[l].astype(_BF16), ff_pad_cols], axis=1)
    def swiglu_ep(accs, ex):
        return [accs[0] * _sigmoid(accs[0]) * accs[1]]
    (u,) = _matmul([h2], [wg, wu], [], swiglu_ep, [(d_ffp, _BF16, None)],
                   tm=_pick_tile(m_tot, 1536, 2 * SUBLANES), tn=tn_ff, name="ffn_up")
    wd = jnp.concatenate([w_ffn_down[l].astype(_BF16), jnp.zeros((d_ffp - d_ff, d_model), _BF16)], axis=0)
    tk_dn = d_ffp // 2 if (d_ffp // 2) % LANES == 0 else d_ffp
    x2 = _matmul_resid(u, wd, x1, tm=tm_mm, tn=_pick_tile(d_model, 512, LANES), tk=tk_dn, name="ffn_down")
    y_p = _rmsnorm_rows(x2, norm_f_g, _F32, 0, mp).reshape(bp, tp, d_model)
    y_s = _rmsnorm_rows(x2, norm_f_g, _F32, mp, ms).reshape(bs, ts, d_model)

    kx4 = lambda x, b, t: x.reshape(1, b, t, n_kv, dh)
    def zsh_last(rows):
        zl = z_a[rows]
        return jnp.concatenate([zl[:, :3 * d_a + r_w], zl[:, 3 * d_a + nwp:3 * d_a + nwp + r_a],
                                zl[:, 3 * d_a + nwp + nap:3 * d_a + nwp + nap + r_g]], axis=-1)
    last_p = jnp.arange(bp) * tp + (tp - 1)
    last_s = mp + jnp.arange(bs) * ts + (ts - 1)
    return (y_p, y_s,
            kx4(kx[:mp], bp, tp), kx4(vx[:mp], bp, tp), ki_new[:mp].reshape(1, bp, tp, d_idx),
            zsh_last(last_p).reshape(1, bp, 1, d_shift), wkv_p[None],
            kx4(kx[mp:], bs, ts), kx4(vx[mp:], bs, ts), ki_new[mp:].reshape(1, bs, ts, d_idx),
            zsh_last(last_s).reshape(1, bs, 1, d_shift), wkv_s[None])
```

```python
import functools
import math

import jax
import jax.numpy as jnp
from jax import lax
from jax.experimental import pallas as pl
from jax.experimental.pallas import tpu as pltpu

CHUNK = 64
NORM_EPS = 1e-6
GN_EPS = 64e-5
TOPK_MAX = 256

LANES = 128
SUBLANES = 8
V7X_VMEM_LIMIT_BYTES = 56 * 1024 * 1024

WKV_CHUNK = 64
WKV_PAIRS_PER_STEP = 8
WKV_ROWS_PER_STEP = 256
DIST_EXCLUDED = 1e30
DSA_GROUPS = 8
DSA_QUERY_TILE = 256
LOG2_E = 1.4426950408889634

ROW_TILE_NORM = 512
ROW_TILE_MM = 1024
COL_TILE_MM = 1024
COL_TILE_KV = 1280
ROW_TILE_FFN_UP = 1536
COL_TILE_FFN_DOWN = 512

_F32 = jnp.float32
_BF16 = jnp.bfloat16


def _round_up(n, m):
    return (n + m - 1) // m * m


def _pick_tile(n, pref, align):
    if n <= pref:
        return n
    t = pref // align * align
    while t >= align:
        if n % t == 0:
            return t
        t -= align
    return n


def _compiler_params(semantics):
    return pltpu.CompilerParams(dimension_semantics=semantics,
                                vmem_limit_bytes=V7X_VMEM_LIMIT_BYTES)


def _sigmoid(x):
    return 1.0 / (1.0 + jnp.exp(-x))


def _rms(x, g):
    x = x.astype(_F32)
    ms = jnp.mean(x * x, axis=-1, keepdims=True)
    return x * lax.rsqrt(ms + NORM_EPS) * g


def _rmsnorm_cat_kernel(xa_ref, xb_ref, g_ref, o_ref, *, n_a):
    i = pl.program_id(0)

    @pl.when(i < n_a)
    def _():
        o_ref[...] = _rms(xa_ref[...], g_ref[...]).astype(o_ref.dtype)

    @pl.when(i >= n_a)
    def _():
        o_ref[...] = _rms(xb_ref[...], g_ref[...]).astype(o_ref.dtype)


def _rmsnorm_cat(xa, xb, g, out_dtype):
    (ma, d), mb = xa.shape, xb.shape[0]
    tm = _pick_tile(math.gcd(ma, mb), ROW_TILE_NORM, SUBLANES)
    n_a = ma // tm
    return pl.pallas_call(
        functools.partial(_rmsnorm_cat_kernel, n_a=n_a),
        grid=((ma + mb) // tm,),
        in_specs=[pl.BlockSpec((tm, d), lambda i: (jnp.minimum(i, n_a - 1), 0)),
                  pl.BlockSpec((tm, d), lambda i: (jnp.maximum(i - n_a, 0), 0)),
                  pl.BlockSpec((1, d), lambda i: (0, 0))],
        out_specs=pl.BlockSpec((tm, d), lambda i: (i, 0)),
        out_shape=jax.ShapeDtypeStruct((ma + mb, d), out_dtype),
        compiler_params=_compiler_params(("arbitrary",)),
        name="rmsnorm_cat",
    )(xa, xb, g.reshape(1, d).astype(_F32))


def _rmsnorm_rows_kernel(x_ref, g_ref, o_ref):
    o_ref[...] = _rms(x_ref[...], g_ref[...]).astype(o_ref.dtype)


def _rmsnorm_rows(x, g, out_dtype, row0, rows):
    d = x.shape[1]
    tm = _pick_tile(math.gcd(row0, rows) if row0 else rows, ROW_TILE_NORM, SUBLANES)
    rb0 = row0 // tm
    return pl.pallas_call(
        _rmsnorm_rows_kernel,
        grid=(rows // tm,),
        in_specs=[pl.BlockSpec((tm, d), lambda i: (rb0 + i, 0)),
                  pl.BlockSpec((1, d), lambda i: (0, 0))],
        out_specs=pl.BlockSpec((tm, d), lambda i: (i, 0)),
        out_shape=jax.ShapeDtypeStruct((rows, d), out_dtype),
        compiler_params=_compiler_params(("parallel",)),
        name="rmsnorm_rows",
    )(x, g.reshape(1, d).astype(_F32))


def _matmul_kernel(*refs, n_pairs, n_extras, n_outs, n_k, head_major, out_head_w, epilogue):
    n_x = len(head_major)
    x_refs = refs[:n_x]
    w_refs = refs[n_x:n_x + n_pairs]
    e_refs = refs[n_x + n_pairs:n_x + n_pairs + n_extras]
    o_refs = refs[n_x + n_pairs + n_extras:n_x + n_pairs + n_extras + n_outs]
    acc_refs = refs[n_x + n_pairs + n_extras + n_outs:]

    def load_x(i):
        if head_major[i]:
            xr = x_refs[i]
            return jnp.concatenate([xr[h] for h in range(xr.shape[0])], axis=1)
        return x_refs[i][...]

    def partial(i):
        return jnp.dot(load_x(i if n_x > 1 else 0), w_refs[i][...], preferred_element_type=_F32)

    def finish(accs):
        outs = epilogue(accs, [e[...] for e in e_refs])
        for o_ref, val, hw in zip(o_refs, outs, out_head_w):
            if hw:
                for h in range(o_ref.shape[0]):
                    o_ref[h] = val[:, h * hw:(h + 1) * hw].astype(o_ref.dtype)
            else:
                o_ref[...] = val.astype(o_ref.dtype)

    if n_k == 1:
        finish([partial(i) for i in range(n_pairs)])
        return

    k = pl.program_id(2)

    @pl.when(k == 0)
    def _():
        for i in range(n_pairs):
            acc_refs[i][...] = partial(i)

    @pl.when(k > 0)
    def _():
        for i in range(n_pairs):
            acc_refs[i][...] += partial(i)

    @pl.when(k == n_k - 1)
    def _():
        finish([a[...] for a in acc_refs])


def _matmul(xs, ws, extras, epilogue, outs, *, tm, tn, tk=None, col_major=False, name="matmul"):
    n_pairs = len(ws)
    assert len(xs) in (1, n_pairs)
    head_major = tuple(x.ndim == 3 for x in xs)
    m = xs[0].shape[1] if head_major[0] else xs[0].shape[0]
    kdim, n = ws[0].shape
    if tk is None:
        tk = kdim
    n_k = kdim // tk
    assert kdim % tk == 0 and m % tm == 0 and n % tn == 0
    n_j = n // tn
    tks = [tk if n_k > 1 else w.shape[0] for w in ws]
    assert all(w.shape == (kdim, n) for w in ws) or n_k == 1

    def order(f):
        return (lambda j, i, k: f(i, j, k)) if col_major else f

    in_specs = []
    for x, hm, tki in zip(xs, head_major, tks):
        if hm:
            assert n_k == 1
            in_specs.append(pl.BlockSpec((x.shape[0], tm, LANES), order(lambda i, j, k: (0, i, 0))))
        else:
            in_specs.append(pl.BlockSpec((tm, tki), order(lambda i, j, k: (i, k))))
    for w, tki in zip(ws, tks):
        in_specs.append(pl.BlockSpec((tki, tn), order(lambda i, j, k: (k, j))))
    for arr, cols, imap in extras:
        in_specs.append(pl.BlockSpec((tm, cols), order(lambda i, j, k, imap=imap: imap(i, j))))

    out_specs, out_shapes, out_head_w = [], [], []
    for n_cols, dtype, hw in outs:
        to = n_cols // n_j
        out_head_w.append(hw)
        if hw:
            assert to % hw == 0
            out_specs.append(pl.BlockSpec((to // hw, tm, hw), order(lambda i, j, k: (j, i, 0))))
            out_shapes.append(jax.ShapeDtypeStruct((n_cols // hw, m, hw), dtype))
        else:
            out_specs.append(pl.BlockSpec((tm, to), order(lambda i, j, k: (i, j))))
            out_shapes.append(jax.ShapeDtypeStruct((m, n_cols), dtype))

    scratch = [pltpu.VMEM((tm, tn), _F32) for _ in range(n_pairs)] if n_k > 1 else []
    kern = functools.partial(
        _matmul_kernel, n_pairs=n_pairs, n_extras=len(extras), n_outs=len(outs), n_k=n_k,
        head_major=head_major, out_head_w=tuple(out_head_w), epilogue=epilogue)
    return pl.pallas_call(
        kern,
        grid=(n_j, m // tm, n_k) if col_major else (m // tm, n_j, n_k),
        in_specs=in_specs,
        out_specs=out_specs,
        out_shape=out_shapes,
        scratch_shapes=scratch,
        compiler_params=_compiler_params(("parallel", "parallel", "arbitrary")),
        name=name,
    )(*xs, *ws, *[e[0] for e in extras])


def _matmul_resid_kernel(x_ref, w_ref, r_ref, o_ref, acc_ref, *, n_k):
    k = pl.program_id(1)
    j = pl.program_id(2)
    part = jnp.dot(x_ref[...], w_ref[...], preferred_element_type=_F32)
    if n_k == 1:
        o_ref[...] = r_ref[...] + part
        return

    @pl.when(k == 0)
    def _():
        acc_ref[j] = part

    @pl.when((k > 0) & (k < n_k - 1))
    def _():
        acc_ref[j] += part

    @pl.when(k == n_k - 1)
    def _():
        o_ref[...] = r_ref[...] + (acc_ref[j] + part)


def _matmul_resid(x, w, resid, *, tm, tn, tk, name):
    m, kdim = x.shape
    n = w.shape[1]
    assert m % tm == 0 and n % tn == 0 and kdim % tk == 0
    n_j, n_k = n // tn, kdim // tk
    last = lambda i, k, j: (i, jnp.where(k == n_k - 1, j, 0))
    return pl.pallas_call(
        functools.partial(_matmul_resid_kernel, n_k=n_k),
        grid=(m // tm, n_k, n_j),
        in_specs=[pl.BlockSpec((tm, tk), lambda i, k, j: (i, k)),
                  pl.BlockSpec((tk, tn), lambda i, k, j: (k, j)),
                  pl.BlockSpec((tm, tn), last)],
        out_specs=pl.BlockSpec((tm, tn), last),
        out_shape=jax.ShapeDtypeStruct((m, n), _F32),
        scratch_shapes=[pltpu.VMEM((n_j, tm, tn), _F32)],
        compiler_params=_compiler_params(("parallel", "arbitrary", "arbitrary")),
        name=name,
    )(x, w, resid)


def _shift_rows(z, prev_row):
    rolled = pltpu.roll(z, 1, 0)
    row = lax.broadcasted_iota(jnp.int32, z.shape, 0)
    return jnp.where(row == 0, prev_row, rolled)


def _cumsum_rows(x):
    n = x.shape[0]
    row = lax.broadcasted_iota(jnp.int32, x.shape, 0)
    s = 1
    while s < n:
        x = x + jnp.where(row >= s, pltpu.roll(x, s, 0), 0.0)
        s *= 2
    return x


def bf(x):
    return x.astype(_BF16)


def _bdot(a, b):
    assert a.dtype == _BF16 and b.dtype == _BF16
    return jnp.dot(a, b, preferred_element_type=_F32)


def _bdot_nt(a, b):
    assert a.dtype == _BF16 and b.dtype == _BF16
    return lax.dot_general(a, b, (((1,), (1,)), ((), ())), preferred_element_type=_F32)


def _wkv_kernel(*refs, n_chunks, head, n_pp, aliased):
    (zr_ref, zk_ref, zv_ref, zl_ref, sprev_ref, lprev_ref, s0_ref,
     mu_ref, mul_ref, w0_ref, a0_ref, kkp_ref, kap_ref, rk_ref, lg_ref, lb_ref,
     w2_ref, a2_ref, g2_ref) = refs[:19]
    y_ref, sout_ref, st_ref, prev_ref, lprev_scr = refs[19 + aliased:]
    c_idx = pl.program_id(2)
    C = WKV_CHUNK
    P = 2 * head
    assert P == 2 * C
    tb = zr_ref.shape[0]
    n_sub = tb // C

    @pl.when(c_idx == 0)
    def _():
        st_ref[...] = s0_ref[0]
        prev_ref[...] = sprev_ref[0, 0]
        lprev_scr[...] = lprev_ref[0]

    lane = lax.broadcasted_iota(jnp.int32, (1, P), 1)
    m0 = (lane < head).astype(_F32)
    m1 = 1.0 - m0
    r2 = lax.broadcasted_iota(jnp.int32, (P, P), 0)
    c2 = lax.broadcasted_iota(jnp.int32, (P, P), 1)
    same = (r2 // C) == (c2 // C)
    tril_s = jnp.where(same & (c2 < r2), 1.0, 0.0)
    tril_i = jnp.where(same & (c2 <= r2), 1.0, 0.0)
    eye = jnp.where(r2 == c2, 1.0, 0.0)

    def stack(x):
        return jnp.concatenate([x * m0, x * m1], axis=0)

    def head_sum(x):
        s0 = jnp.sum(x * m0, axis=-1, keepdims=True)
        s1 = jnp.sum(x * m1, axis=-1, keepdims=True)
        return s0 * m0 + s1 * m1

    zl = zl_ref[...]
    xl = zl + (_shift_rows(zl, lprev_scr[...]) - zl) * mul_ref[...]
    lprev_scr[...] = zl[tb - 1:tb, :]
    nw = w2_ref.shape[0]
    na = a2_ref.shape[0]
    tw = jnp.tanh(xl[:, 0:nw]).astype(_BF16)
    xa = xl[:, nw:nw + na].astype(_BF16)
    sg = _sigmoid(xl[:, nw + na:]).astype(_BF16)

    pairs = []
    for pi in range(n_pp):
        ls = slice(pi * P, (pi + 1) * P)
        zr, zk, zv = zr_ref[:, ls], zk_ref[:, ls], zv_ref[:, ls]
        r = zr + (_shift_rows(zr, prev_ref[0:1, ls]) - zr) * mu_ref[0:1, ls]
        k = zk + (_shift_rows(zk, prev_ref[1:2, ls]) - zk) * mu_ref[1:2, ls]
        v = zv + (_shift_rows(zv, prev_ref[2:3, ls]) - zv) * mu_ref[2:3, ls]
        prev_ref[0:1, ls] = zr[tb - 1:tb, :]
        prev_ref[1:2, ls] = zk[tb - 1:tb, :]
        prev_ref[2:3, ls] = zv[tb - 1:tb, :]
        u = -(w0_ref[:, ls] + jnp.dot(tw, w2_ref[:, ls], preferred_element_type=_F32))
        softplus = jnp.maximum(u, 0.0) + jnp.log(1.0 + jnp.exp(-jnp.abs(u)))
        logw = -jnp.exp(-softplus - 0.5)
        a = _sigmoid(a0_ref[:, ls] + jnp.dot(xa, a2_ref[:, ls], preferred_element_type=_F32))
        g = jnp.dot(sg, g2_ref[:, ls], preferred_element_type=_F32)
        kk = k * kkp_ref[:, ls]
        kk = kk / jnp.maximum(jnp.sqrt(head_sum(kk * kk)), 1e-12)
        kh = k * (1.0 + (a - 1.0) * kap_ref[:, ls])
        pairs.append(dict(ls=ls, r=r, v=v, logw=logw, kk=kk, kh=kh, beta=kk * a, g=g,
                          bonus=head_sum(r * kh * rk_ref[:, ls]) * v))

    states = [st_ref[pi] for pi in range(n_pp)]
    group = n_sub
    for sc0 in range(0, n_sub, group):
        items = []
        for sc in range(sc0, sc0 + group):
            rs = slice(sc * C, (sc + 1) * C)
            for pi, pr in enumerate(pairs):
                lw = pr["logw"][rs]
                cum = _cumsum_rows(lw)
                cum_c = cum[C - 1:C, :]
                e_neg = jnp.exp(-cum)
                e_end = jnp.exp(cum_c - cum)
                r2s = stack(pr["r"][rs] * jnp.exp(cum))
                items.append(dict(
                    pi=pi, rs=rs, r2s=r2s,
                    ar=bf(jnp.concatenate([stack(-pr["kk"][rs] * jnp.exp(cum - lw)), r2s], axis=0)),
                    bk=bf(jnp.concatenate([stack(pr["beta"][rs] * e_neg), stack(pr["kh"][rs] * e_neg)],
                                          axis=0)),
                    v2s=bf(stack(pr["v"][rs])),
                    bh2t=bf(stack(pr["beta"][rs] * e_end).T),
                    kh2t=bf(stack(pr["kh"][rs] * e_end).T),
                    dec_col=jnp.sum(eye * jnp.exp(cum_c), axis=-1, keepdims=True)))
        for it in items:
            nn = _bdot_nt(it["ar"], it["bk"])
            n_ba = nn[:P, :P] * tril_s
            it["n_ka"] = bf(nn[:P, P:] * tril_s)
            it["p_br"] = bf(nn[P:, :P] * tril_i)
            it["p_kr"] = bf(nn[P:, P:] * tril_i)
            it["t_inv"] = eye + n_ba
            it["pw"] = bf(n_ba)
        for it in items:
            it["pw"] = bf(_bdot(it["pw"], it["pw"]))
        s = 2
        while s < C:
            for it in items:
                sq = _bdot(it["pw"], jnp.concatenate([it["pw"], bf(it["t_inv"])], axis=1))
                it["pw"] = bf(sq[:, :P])
                it["t_inv"] = it["t_inv"] + sq[:, P:]
            s *= 2
        for it in items:
            it["nkv"] = bf(_bdot(it["n_ka"], it["v2s"]))
        for it in items:
            it["tt"] = bf(_bdot(bf(it["t_inv"]), jnp.concatenate([it["ar"][:P], it["nkv"]], axis=1)))
        for it in items:
            it["mg"] = _bdot(it["bh2t"], it["tt"])
        for it in items:
            it["pg"] = _bdot(it["p_br"], it["tt"])
        for it in items:
            it["g_c"] = it["mg"][:, P:] + _bdot(it["kh2t"], it["v2s"])
        for it in items:
            it["yg"] = it["pg"][:, P:] + _bdot(it["p_kr"], it["v2s"])
        for it in items:
            pi, rs, pr = it["pi"], it["rs"], pairs[it["pi"]]
            ls = pr["ls"]
            st = states[pi]
            st_b = bf(st)
            y2 = _bdot(bf(it["r2s"] + it["pg"][:, :P]), st_b) + it["yg"]
            states[pi] = it["dec_col"] * st + _bdot(bf(it["mg"][:, :P]), st_b) + it["g_c"]
            y = y2[0:C, :] + y2[C:2 * C, :]
            mean = head_sum(y) * (1.0 / head)
            yc = y - mean
            var = head_sum(yc * yc) * (1.0 / head)
            yn = yc * lax.rsqrt(var + GN_EPS) * lg_ref[:, ls] + lb_ref[:, ls]
            y_ref[rs, ls] = ((yn + pr["bonus"][rs]) * pr["g"][rs]).astype(y_ref.dtype)
    for pi in range(n_pp):
        st_ref[pi] = states[pi]

    @pl.when(c_idx == n_chunks - 1)
    def _():
        sout_ref[0] = st_ref[...]


def _wkv(z, shift_prev, s0_bd, params, *, batch, seq, row0, head, d_a, lora_sizes, m_out, out_buf):
    pw = 2 * head
    n_pairs = d_a // pw
    n_pp = min(WKV_PAIRS_PER_STEP, n_pairs)
    gw = n_pp * pw
    n_groups = n_pairs // n_pp
    lp = sum(lora_sizes)
    assert (3 * d_a) % lp == 0
    tb = _pick_tile(seq, WKV_ROWS_PER_STEP, WKV_CHUNK)
    n_chunks = seq // tb
    assert row0 % tb == 0
    rb0 = row0 // tb
    rpb = seq // tb
    aliased = out_buf is not None

    def rows(b, c):
        return rb0 + b * rpb + c

    in_specs = [
        pl.BlockSpec((tb, gw), lambda b, p, c: (rows(b, c), p)),
        pl.BlockSpec((tb, gw), lambda b, p, c: (rows(b, c), n_groups + p)),
        pl.BlockSpec((tb, gw), lambda b, p, c: (rows(b, c), 2 * n_groups + p)),
        pl.BlockSpec((tb, lp), lambda b, p, c: (rows(b, c), 3 * d_a // lp)),
        pl.BlockSpec((1, 1, SUBLANES, gw), lambda b, p, c: (b, p, 0, 0)),
        pl.BlockSpec((1, 1, lp), lambda b, p, c: (b, 0, 0)),
        pl.BlockSpec((1, n_pp, pw, pw), lambda b, p, c: (b, p, 0, 0)),
        pl.BlockSpec((SUBLANES, gw), lambda b, p, c: (p, 0)),
        pl.BlockSpec((1, lp), lambda b, p, c: (0, 0)),
    ]
    for _ in range(7):
        in_specs.append(pl.BlockSpec((1, gw), lambda b, p, c: (0, p)))
    nw, na, ng = lora_sizes
    in_specs += [pl.BlockSpec((nw, gw), lambda b, p, c: (0, p)),
                 pl.BlockSpec((na, gw), lambda b, p, c: (0, p)),
                 pl.BlockSpec((ng, gw), lambda b, p, c: (0, p))]
    args = [z, z, z, z, shift_prev["rkv"], shift_prev["lora"], s0_bd,
            params["mu_rkv"], params["mu_lora"], params["w0"], params["a0"], params["k_k"],
            params["k_a"], params["r_k"], params["lnx_g"], params["lnx_b"],
            params["w2"], params["a2"], params["g2"]]
    aliases = {}
    if aliased:
        in_specs.append(pl.BlockSpec(memory_space=pl.ANY))
        aliases = {len(args): 0}
        args.append(out_buf)
    out_specs = [pl.BlockSpec((tb, gw), lambda b, p, c: (rows(b, c), p)),
                 pl.BlockSpec((1, n_pp, pw, pw), lambda b, p, c: (b, p, 0, 0))]
    out_shape = [jax.ShapeDtypeStruct((m_out, d_a), _BF16),
                 jax.ShapeDtypeStruct((batch, n_pairs, pw, pw), _F32)]
    kern = functools.partial(_wkv_kernel, n_chunks=n_chunks, head=head, n_pp=n_pp, aliased=int(aliased))
    return pl.pallas_call(
        kern,
        grid=(batch, n_groups, n_chunks),
        in_specs=in_specs,
        out_specs=out_specs,
        out_shape=out_shape,
        scratch_shapes=[pltpu.VMEM((n_pp, pw, pw), _F32),
                        pltpu.VMEM((SUBLANES, gw), _F32),
                        pltpu.VMEM((1, lp), _F32)],
        input_output_aliases=aliases,
        compiler_params=_compiler_params(("parallel", "parallel", "arbitrary")),
        name="wkv7_chunked",
    )(*args)


def _dsa_kernel(*refs, tq, l_ext, l_true, q_pos0, topk, idx_scale, n_kv, dh, d_idx, aliased):
    q_ref, k_ref, v_ref, qi_ref, wi_ref, ki_ref = refs[:6]
    o_ref, kbf, vbf, kibf, wib, isc_scr, key_scr, dm_scr = refs[6 + aliased:]
    qb = pl.program_id(1)
    n_q = q_ref.shape[0]
    n_i = qi_ref.shape[0]
    rep = n_q // n_kv

    @pl.when(qb == 0)
    def _():
        for g in range(n_kv):
            kbf[g] = k_ref[0:l_ext, g * dh:(g + 1) * dh].astype(_BF16)
            vbf[g] = v_ref[0:l_ext, g * dh:(g + 1) * dh].astype(_BF16)
        kibf[...] = ki_ref[0:l_ext, 0:d_idx].astype(_BF16)

    wi = wi_ref[...]
    for hi in range(n_i):
        wib[hi] = jnp.broadcast_to(wi[:, hi:hi + 1], (tq, LANES))
    isc_scr[...] = jnp.zeros((tq, l_ext), _F32)
    ki = kibf[...]

    hs = max(1, min(4, 512 // tq))
    assert n_i % hs == 0

    def idx_body(i, carry):
        qs = qi_ref[pl.ds(i * hs, hs)].reshape(hs * tq, d_idx)
        sc = lax.dot_general(qs, ki, (((1,), (1,)), ((), ())), preferred_element_type=_F32)
        acc = isc_scr[...]
        for j in range(hs):
            acc = acc + (jnp.maximum(sc[j * tq:(j + 1) * tq], 0.0)
                         * jnp.tile(wib[i * hs + j], (1, l_ext // LANES)))
        isc_scr[...] = acc
        return carry

    lax.fori_loop(0, n_i // hs, idx_body, 0)

    q_pos = q_pos0 + qb * tq + lax.broadcasted_iota(jnp.int32, (tq, l_ext), 0)
    k_pos = lax.broadcasted_iota(jnp.int32, (tq, l_ext), 1)
    cs = CHUNK.bit_length() - 1
    assert 1 << cs == CHUNK
    allowed = ((k_pos >> cs) <= (q_pos >> cs)) & (k_pos < l_true)
    bits = pltpu.bitcast(isc_scr[...] * idx_scale, jnp.int32)
    int_min = jnp.int32(-2 ** 31)
    key_scr[...] = jnp.where(allowed, jnp.where(bits < 0, bits ^ jnp.int32(0x7FFFFFFF), bits), int_min)

    def count_ge(cand):
        return jnp.sum(jnp.where(key_scr[...] >= cand, 1.0, 0.0), axis=-1, keepdims=True)

    kf = jnp.float32(topk)
    zero = jnp.zeros((tq, 1), jnp.int32)
    thr = jnp.where(count_ge(zero) >= kf, zero, jnp.full((tq, 1), int_min, jnp.int32))

    def thr_body(i, thr):
        cand = thr + (jnp.int32(1) << (jnp.int32(30) - i))
        return jnp.where(count_ge(cand) >= kf, cand, thr)

    thr = lax.fori_loop(0, 31, thr_body, thr)
    key = key_scr[...]
    n_gt = jnp.sum(jnp.where(key > thr, 1.0, 0.0), axis=-1, keepdims=True)
    n_eq = jnp.sum(jnp.where(key == thr, 1.0, 0.0), axis=-1, keepdims=True)
    need = kf - n_gt
    n_bits = max(1, (l_ext - 1).bit_length())
    full = jnp.full((tq, 1), 1 << n_bits, jnp.int32)

    def tie_bound():
        def count_tie_below(bound):
            hit = jnp.where(key_scr[...] == thr, jnp.where(k_pos < bound, 1.0, 0.0), 0.0)
            return jnp.sum(hit, axis=-1, keepdims=True)

        def tbody(i, bound):
            cand = bound + (jnp.int32(1) << (jnp.int32(n_bits - 1) - i))
            return jnp.where(count_tie_below(cand) <= need, cand, bound)

        return lax.fori_loop(0, n_bits, tbody, zero)

    has_excess_ties = jnp.max(n_eq - need) > 0.0
    bound = lax.cond(has_excess_ties, tie_bound, lambda: full)
    sel = allowed & ((key > thr) | ((key == thr) & (k_pos < bound)))
    dm_scr[...] = jnp.where(sel, jnp.abs(q_pos - k_pos).astype(_F32), DIST_EXCLUDED)

    def qk(h, kg):
        return lax.dot_general(q_ref[h], kg, (((1,), (1,)), ((), ())), preferred_element_type=_F32)

    def head_body(g, carry):
        kg, vg = kbf[g], vbf[g]
        dm = dm_scr[...]
        s_next = qk(g * rep, kg)
        for j in range(rep):
            h = g * rep + j
            s_cur = s_next
            if j + 1 < rep:
                s_next = qk(h + 1, kg)
            slope2 = jnp.exp2(jnp.full((1, 1), h + 1, jnp.int32).astype(_F32) * (-8.0 / n_q)) * LOG2_E
            s = s_cur - slope2 * dm
            m = jnp.max(s, axis=-1, keepdims=True)
            p = jnp.exp2(s - m)
            l = jnp.sum(p, axis=-1, keepdims=True)
            o = jnp.dot(p.astype(_BF16), vg, preferred_element_type=_F32)
            o_ref[h] = (o / l).astype(o_ref.dtype)
        return carry

    lax.fori_loop(0, n_kv, head_body, 0)


def _dsa(q_hm, qi_hm, kv_src, k_col, v_col, ki_src, ki_col, wi_src, wi_col, *, batch, n_qb, tq,
         q_row0, q_stride, kv_rows, kv_stride, l_ext, l_true, q_pos0, topk, n_kv, m_out, out_buf):
    n_q, _, dh = q_hm.shape
    n_i, _, d_idx = qi_hm.shape
    assert q_row0 % tq == 0 and q_stride % tq == 0 and l_ext % LANES == 0 and l_ext <= kv_rows
    qr0, qst = q_row0 // tq, q_stride // tq
    aliased = out_buf is not None
    kern = functools.partial(
        _dsa_kernel, tq=tq, l_ext=l_ext, l_true=l_true, q_pos0=q_pos0, topk=topk,
        idx_scale=float(n_i) ** -0.5 * float(d_idx) ** -0.5, n_kv=n_kv, dh=dh, d_idx=d_idx,
        aliased=int(aliased))
    qrow = lambda b, i: qr0 + b * qst + i
    in_specs = [
        pl.BlockSpec((n_q, tq, dh), lambda b, i: (0, qrow(b, i), 0)),
        pl.BlockSpec((kv_rows, n_kv * dh), lambda b, i: (b * kv_stride, k_col)),
        pl.BlockSpec((kv_rows, n_kv * dh), lambda b, i: (b * kv_stride, v_col)),
        pl.BlockSpec((n_i, tq, d_idx), lambda b, i: (0, qrow(b, i), 0)),
        pl.BlockSpec((tq, LANES), lambda b, i: (qrow(b, i), wi_col)),
        pl.BlockSpec((kv_rows, LANES), lambda b, i: (b * kv_stride, ki_col)),
    ]
    args = [q_hm, kv_src, kv_src, qi_hm, wi_src, ki_src]
    aliases = {}
    if aliased:
        in_specs.append(pl.BlockSpec(memory_space=pl.ANY))
        aliases = {len(args): 0}
        args.append(out_buf)
    return pl.pallas_call(
        kern,
        grid=(batch, n_qb),
        in_specs=in_specs,
        out_specs=pl.BlockSpec((n_q, tq, dh), lambda b, i: (0, qrow(b, i), 0)),
        out_shape=jax.ShapeDtypeStruct((n_q, m_out, dh), _BF16),
        scratch_shapes=[pltpu.VMEM((n_kv, l_ext, dh), _BF16), pltpu.VMEM((n_kv, l_ext, dh), _BF16),
                        pltpu.VMEM((l_ext, d_idx), _BF16), pltpu.VMEM((n_i, tq, LANES), _F32),
                        pltpu.VMEM((tq, l_ext), _F32), pltpu.VMEM((tq, l_ext), jnp.int32),
                        pltpu.VMEM((tq, l_ext), _F32)],
        input_output_aliases=aliases,
        compiler_params=_compiler_params(("parallel", "arbitrary")),
        name="dsa_attention",
    )(*args)


def _pad_rows(w, rows):
    return jnp.pad(w, ((0, rows - w.shape[0]), (0, 0)))


def _pad_cols(w, cols):
    return jnp.pad(w, ((0, 0), (0, cols - w.shape[1])))


def _to_block_diag_t(s, head):
    b, h = s.shape[:2]
    st = jnp.swapaxes(s, -1, -2).reshape(b, h // 2, 2, head, head).astype(_F32)
    z = jnp.zeros_like(st[:, :, 0])
    top = jnp.concatenate([st[:, :, 0], z], axis=-1)
    bot = jnp.concatenate([z, st[:, :, 1]], axis=-1)
    return jnp.concatenate([top, bot], axis=-2)


def _from_block_diag_t(sbd, head):
    b, hp = sbd.shape[:2]
    s0 = sbd[:, :, :head, :head]
    s1 = sbd[:, :, head:, head:]
    st = jnp.stack([s0, s1], axis=2).reshape(b, hp * 2, head, head)
    return jnp.swapaxes(st, -1, -2)


def kernel(x_prompt, x_sample, cache_k, cache_v, cache_kidx, state_shift, state_wkv, norm1_g, w_in, mu_shift, w0, w2, a0, a2, g2, k_k, k_a, r_k, lnx_g, lnx_b, w_br_a, w_br_b, w_out, norm2_g, w_ffn_gate, w_ffn_up, w_ffn_down, norm_f_g):
    depth = w_in.shape[0]
    assert depth == 1
    bp, tp, d_model = x_prompt.shape
    bs, ts, _ = x_sample.shape
    _, _, past, n_kv, dh = cache_k.shape
    d_idx = cache_kidx.shape[-1]
    d_shift = state_shift.shape[-1]
    _, _, h_a, head, _ = state_wkv.shape
    d_a = h_a * head
    r_w, r_a, r_g = w2.shape[1], a2.shape[1], g2.shape[1]
    n_q = w_br_b.shape[1] // dh
    d_in = w_in.shape[-1]
    h_i = (d_in - d_shift - n_q * dh - 2 * n_kv * dh - d_idx - 2 * d_model) // (d_idx + 1)
    d_ff = w_ffn_gate.shape[-1]
    kvw = n_kv * dh
    l = 0
    mp, ms = bp * tp, bs * ts
    m_tot = mp + ms
    xp2, xs2 = x_prompt.reshape(mp, d_model), x_sample.reshape(ms, d_model)
    tm_mm = _pick_tile(m_tot, ROW_TILE_MM, 2 * SUBLANES)

    win = w_in[l]
    o = 0
    def take(n):
        nonlocal o
        blk = win[:, o:o + n]
        o += n
        return blk
    nwp, nap, ngp = _round_up(r_w, LANES), _round_up(r_a, LANES), _round_up(r_g, LANES)
    lp = nwp + nap + ngp
    w_rkv, w_lw, w_la, w_lg = take(3 * d_a), take(r_w), take(r_a), take(r_g)
    w_rkvl = jnp.concatenate([w_rkv, _pad_cols(w_lw, nwp), _pad_cols(w_la, nap), _pad_cols(w_lg, ngp)],
                             axis=1).astype(_BF16)
    w_q = take(n_q * dh).astype(_BF16)
    w_k, w_v = take(kvw), take(kvw)
    w_qi = take(h_i * d_idx).astype(_BF16)
    w_ki, w_wi = take(d_idx), take(h_i)
    w_kv = jnp.concatenate([w_k, w_v, _pad_cols(w_ki, LANES), _pad_cols(w_wi, LANES)], axis=1).astype(_BF16)
    w_gates = jnp.concatenate([take(d_model), take(d_model)], axis=1).astype(_BF16)

    h1 = _rmsnorm_cat(xp2, xs2, norm1_g[l], _BF16)
    ident = lambda accs, ex: [accs[0]]
    n_rkvl = 3 * d_a + lp
    (z_a,) = _matmul([h1], [w_rkvl], [], ident, [(n_rkvl, _F32, None)],
                     tm=tm_mm, tn=_pick_tile(n_rkvl, COL_TILE_MM, LANES), name="proj_rkv_lora")
    n_kvz = 2 * kvw + 2 * LANES
    (z_kv,) = _matmul([h1], [w_kv], [], ident, [(n_kvz, _F32, None)],
                      tm=tm_mm, tn=_pick_tile(n_kvz, COL_TILE_KV, LANES), name="proj_kv_idx")
    q_scale = float(dh) ** -0.5 * LOG2_E
    (q_hm,) = _matmul([h1], [w_q], [], lambda accs, ex: [accs[0] * q_scale],
                      [(n_q * dh, _BF16, dh)], tm=tm_mm, tn=_pick_tile(n_q * dh, COL_TILE_MM, dh), name="proj_q")
    (qi_hm,) = _matmul([h1], [w_qi], [], ident, [(h_i * d_idx, _BF16, d_idx)],
                       tm=tm_mm, tn=_pick_tile(h_i * d_idx, COL_TILE_MM, LANES), name="proj_qi")
    (gates,) = _matmul([h1], [w_gates], [], ident, [(2 * d_model, _BF16, None)],
                       tm=tm_mm, tn=_pick_tile(2 * d_model, COL_TILE_MM, LANES), name="proj_gates")
    kx, vx = z_kv[:, :kvw], z_kv[:, kvw:2 * kvw]
    ki_new = z_kv[:, 2 * kvw:2 * kvw + d_idx]

    mu = mu_shift[l]
    def lora_row(vec):
        return jnp.concatenate([
            jnp.pad(vec[3 * d_a:3 * d_a + r_w], (0, nwp - r_w)),
            jnp.pad(vec[3 * d_a + r_w:3 * d_a + r_w + r_a], (0, nap - r_a)),
            jnp.pad(vec[3 * d_a + r_w + r_a:], (0, ngp - r_g))])
    n_pairs = d_a // (2 * head)
    gw = min(WKV_PAIRS_PER_STEP, n_pairs) * 2 * head
    n_groups = d_a // gw
    def rkv_rows(vec):
        lead = vec.shape[:-1]
        x = vec[..., :3 * d_a].reshape(lead + (3, n_groups, gw))
        x = jnp.moveaxis(x, -3, -2)
        return jnp.pad(x, [(0, 0)] * (len(lead) + 1) + [(0, SUBLANES - 3), (0, 0)])
    row = lambda v: v[l].reshape(1, d_a).astype(_F32)
    params = dict(
        mu_rkv=rkv_rows(mu).reshape(n_groups * SUBLANES, gw),
        mu_lora=lora_row(mu).reshape(1, lp),
        w0=row(w0), a0=row(a0), k_k=row(k_k), k_a=row(k_a),
        r_k=r_k[l].reshape(1, d_a).astype(_F32), lnx_g=row(lnx_g), lnx_b=row(lnx_b),
        w2=_pad_rows(w2[l], nwp).astype(_BF16), a2=_pad_rows(a2[l], nap).astype(_BF16),
        g2=_pad_rows(g2[l], ngp).astype(_BF16))

    def shift_state(s):
        s = s[:, 0]
        return dict(rkv=rkv_rows(s), lora=jax.vmap(lora_row)(s).reshape(-1, 1, lp))

    wkv_kw = dict(head=head, d_a=d_a, lora_sizes=(nwp, nap, ngp), m_out=m_tot)
    y_a, sp_bd = _wkv(z_a, shift_state(jnp.zeros((bp, 1, d_shift), _F32)),
                      jnp.zeros((bp, n_pairs, 2 * head, 2 * head), _F32), params,
                      batch=bp, seq=tp, row0=0, out_buf=jnp.zeros((m_tot, d_a), _BF16), **wkv_kw)
    y_a, ss_bd = _wkv(z_a, shift_state(state_shift[l]), _to_block_diag_t(state_wkv[l], head), params,
                      batch=bs, seq=ts, row0=mp, out_buf=y_a, **wkv_kw)
    wkv_p = _from_block_diag_t(sp_bd, head)
    wkv_s = _from_block_diag_t(ss_bd, head)

    topk_p = min(TOPK_MAX, tp // 4)
    tq_p = _pick_tile(tp, DSA_QUERY_TILE, CHUNK)
    n_grp = max(1, min(DSA_GROUPS, tp // tq_p))
    grp = tp // n_grp
    wi_col = (2 * kvw + LANES) // LANES
    ki_col = 2 * kvw // LANES
    y_b = jnp.zeros((n_q, m_tot, dh), _BF16)
    for gi in range(n_grp):
        l_ext = (gi + 1) * grp
        kv_rows = l_ext
        while tp % kv_rows:
            kv_rows += grp
        y_b = _dsa(q_hm, qi_hm, z_kv, 0, 1, z_kv, ki_col, z_kv, wi_col,
                   batch=bp, n_qb=grp // tq_p, tq=tq_p, q_row0=gi * grp, q_stride=tp,
                   kv_rows=kv_rows, kv_stride=tp // kv_rows, l_ext=l_ext, l_true=l_ext,
                   q_pos0=gi * grp, topk=topk_p, n_kv=n_kv, m_out=m_tot, out_buf=y_b)
    ls = past + ts
    lps = _round_up(ls, LANES)
    def with_cache(cache, new, width):
        x = jnp.concatenate([cache.reshape(bs, past, width), new.reshape(bs, ts, width)], axis=1)
        return jnp.pad(x, ((0, 0), (0, lps - ls), (0, 0))).reshape(bs * lps, width)
    kv_s = jnp.concatenate([with_cache(cache_k[l], kx[mp:], kvw), with_cache(cache_v[l], vx[mp:], kvw)], axis=1)
    ki_s = _pad_cols(with_cache(cache_kidx[l], ki_new[mp:], d_idx), LANES)
    y_b = _dsa(q_hm, qi_hm, kv_s, 0, 1, ki_s, 0, z_kv, wi_col,
               batch=bs, n_qb=1, tq=ts, q_row0=mp, q_stride=ts, kv_rows=lps, kv_stride=1, l_ext=lps, l_true=ls,
               q_pos0=past, topk=min(TOPK_MAX, ls // 4), n_kv=n_kv, m_out=m_tot, out_buf=y_b)

    tn_mg = _pick_tile(d_model, COL_TILE_MM, LANES)
    nj_mg = d_model // tn_mg
    def merge_ep(accs, ex):
        return [_sigmoid(ex[0].astype(_F32)) * accs[0] + _sigmoid(ex[1].astype(_F32)) * accs[1]]
    (merged,) = _matmul([y_a, y_b], [w_br_a[l].astype(_BF16), w_br_b[l].astype(_BF16)],
                        [(gates, tn_mg, lambda i, j: (i, j)), (gates, tn_mg, lambda i, j: (i, nj_mg + j))],
                        merge_ep, [(d_model, _BF16, None)], tm=tm_mm, tn=tn_mg, name="branch_merge")
    tn_res = _pick_tile(d_model, COL_TILE_MM, LANES)
    tm_io = _pick_tile(math.gcd(mp, ms), ROW_TILE_NORM, 2 * SUBLANES)
    n_io_p = mp // tm_io
    def resid2_ep(accs, ex):
        return [jnp.where(pl.program_id(1) < n_io_p, ex[0], ex[1]) + accs[0]]
    (x1,) = _matmul([merged], [w_out[l].astype(_BF16)],
                    [(xp2, tn_res, lambda i, j: (jnp.minimum(i, n_io_p - 1), j)),
                     (xs2, tn_res, lambda i, j: (jnp.maximum(i - n_io_p, 0), j))],
                    resid2_ep, [(d_model, _F32, None)], tm=tm_io, tn=tn_res, col_major=True,
                    name="out_proj")

    h2 = _rmsnorm_rows(x1, norm2_g[l], _BF16, 0, m_tot)
    tn_ff = 2 * LANES
    d_ffp = _round_up(d_ff, tn_ff)
    ff_pad_cols = jnp.zeros((d_model, d_ffp - d_ff), _BF16)
    wg = jnp.concatenate([w_ffn_gate[l].astype(_BF16), ff_pad_cols], axis=1)
    wu = jnp.concatenate([w_ffn_up[l].astype(_BF16), ff_pad_cols], axis=1)
    def swiglu_ep(accs, ex):
        return [accs[0] * _sigmoid(accs[0]) * accs[1]]
    (u,) = _matmul([h2], [wg, wu], [], swiglu_ep, [(d_ffp, _BF16, None)],
                   tm=_pick_tile(m_tot, ROW_TILE_FFN_UP, 2 * SUBLANES), tn=tn_ff, name="ffn_up")
    wd = jnp.concatenate([w_ffn_down[l].astype(_BF16), jnp.zeros((d_ffp - d_ff, d_model), _BF16)], axis=0)
    tk_dn = d_ffp // 2 if (d_ffp // 2) % LANES == 0 else d_ffp
    x2 = _matmul_resid(u, wd, x1, tm=tm_mm, tn=_pick_tile(d_model, COL_TILE_FFN_DOWN, LANES), tk=tk_dn, name="ffn_down")
    y_p = _rmsnorm_rows(x2, norm_f_g, _F32, 0, mp).reshape(bp, tp, d_model)
    y_s = _rmsnorm_rows(x2, norm_f_g, _F32, mp, ms).reshape(bs, ts, d_model)

    kx4 = lambda x, b, t: x.reshape(1, b, t, n_kv, dh)
    def zsh_last(rows):
        zl = z_a[rows]
        return jnp.concatenate([zl[:, :3 * d_a + r_w], zl[:, 3 * d_a + nwp:3 * d_a + nwp + r_a],
                                zl[:, 3 * d_a + nwp + nap:3 * d_a + nwp + nap + r_g]], axis=-1)
    last_p = jnp.arange(bp) * tp + (tp - 1)
    last_s = mp + jnp.arange(bs) * ts + (ts - 1)
    return (y_p, y_s,
            kx4(kx[:mp], bp, tp), kx4(vx[:mp], bp, tp), ki_new[:mp].reshape(1, bp, tp, d_idx),
            zsh_last(last_p).reshape(1, bp, 1, d_shift), wkv_p[None],
            kx4(kx[mp:], bs, ts), kx4(vx[mp:], bs, ts), ki_new[mp:].reshape(1, bs, ts, d_idx),
            zsh_last(last_s).reshape(1, bs, 1, d_shift), wkv_s[None])
```

```python
import functools
import math

import jax
import jax.numpy as jnp
from jax import lax
from jax.experimental import pallas as pl
from jax.experimental.pallas import tpu as pltpu

CHUNK = 64
NORM_EPS = 1e-6
GN_EPS = 64e-5
TOPK_MAX = 256

LANES = 128
SUBLANES = 8
V7X_VMEM_LIMIT_BYTES = 56 * 1024 * 1024

WKV_CHUNK = 64
WKV_PAIRS_PER_STEP = 8
WKV_ROWS_PER_STEP = 256
DIST_EXCLUDED = 1e30
DSA_GROUPS = 8
DSA_QUERY_TILE = 256
LOG2_E = 1.4426950408889634

ROW_TILE_NORM = 512
ROW_TILE_MM = 1024
COL_TILE_MM = 1024
COL_TILE_KV = 1280
ROW_TILE_FFN_UP = 1536
COL_TILE_FFN_DOWN = 512

_F32 = jnp.float32
_BF16 = jnp.bfloat16


def _round_up(n, m):
    return (n + m - 1) // m * m


def _pick_tile(n, pref, align):
    if n <= pref:
        return n
    t = pref // align * align
    while t >= align:
        if n % t == 0:
            return t
        t -= align
    return n


def _compiler_params(semantics):
    return pltpu.CompilerParams(dimension_semantics=semantics,
                                vmem_limit_bytes=V7X_VMEM_LIMIT_BYTES)


def _sigmoid(x):
    return 1.0 / (1.0 + jnp.exp(-x))


def _rms(x, g):
    x = x.astype(_F32)
    ms = jnp.mean(x * x, axis=-1, keepdims=True)
    return x * lax.rsqrt(ms + NORM_EPS) * g


def _rmsnorm_cat_kernel(xa_ref, xb_ref, g_ref, o_ref, *, n_a):
    i = pl.program_id(0)

    @pl.when(i < n_a)
    def _():
        o_ref[...] = _rms(xa_ref[...], g_ref[...]).astype(o_ref.dtype)

    @pl.when(i >= n_a)
    def _():
        o_ref[...] = _rms(xb_ref[...], g_ref[...]).astype(o_ref.dtype)


def _rmsnorm_cat(xa, xb, g, out_dtype):
    (ma, d), mb = xa.shape, xb.shape[0]
    tm = _pick_tile(math.gcd(ma, mb), ROW_TILE_NORM, SUBLANES)
    n_a = ma // tm
    return pl.pallas_call(
        functools.partial(_rmsnorm_cat_kernel, n_a=n_a),
        grid=((ma + mb) // tm,),
        in_specs=[pl.BlockSpec((tm, d), lambda i: (jnp.minimum(i, n_a - 1), 0)),
                  pl.BlockSpec((tm, d), lambda i: (jnp.maximum(i - n_a, 0), 0)),
                  pl.BlockSpec((1, d), lambda i: (0, 0))],
        out_specs=pl.BlockSpec((tm, d), lambda i: (i, 0)),
        out_shape=jax.ShapeDtypeStruct((ma + mb, d), out_dtype),
        compiler_params=_compiler_params(("arbitrary",)),
        name="rmsnorm_cat",
    )(xa, xb, g.reshape(1, d).astype(_F32))


def _rmsnorm_rows_kernel(x_ref, g_ref, o_ref):
    o_ref[...] = _rms(x_ref[...], g_ref[...]).astype(o_ref.dtype)


def _rmsnorm_rows(x, g, out_dtype, row0, rows):
    d = x.shape[1]
    tm = _pick_tile(math.gcd(row0, rows) if row0 else rows, ROW_TILE_NORM, SUBLANES)
    rb0 = row0 // tm
    return pl.pallas_call(
        _rmsnorm_rows_kernel,
        grid=(rows // tm,),
        in_specs=[pl.BlockSpec((tm, d), lambda i: (rb0 + i, 0)),
                  pl.BlockSpec((1, d), lambda i: (0, 0))],
        out_specs=pl.BlockSpec((tm, d), lambda i: (i, 0)),
        out_shape=jax.ShapeDtypeStruct((rows, d), out_dtype),
        compiler_params=_compiler_params(("parallel",)),
        name="rmsnorm_rows",
    )(x, g.reshape(1, d).astype(_F32))


def _matmul_kernel(*refs, n_pairs, n_extras, n_outs, n_k, head_major, out_head_w, w_t, epilogue):
    n_x = len(head_major)
    x_refs = refs[:n_x]
    w_refs = refs[n_x:n_x + n_pairs]
    e_refs = refs[n_x + n_pairs:n_x + n_pairs + n_extras]
    o_refs = refs[n_x + n_pairs + n_extras:n_x + n_pairs + n_extras + n_outs]
    acc_refs = refs[n_x + n_pairs + n_extras + n_outs:]

    def load_x(i):
        if head_major[i]:
            xr = x_refs[i]
            return jnp.concatenate([xr[h] for h in range(xr.shape[0])], axis=1)
        return x_refs[i][...]

    def partial(i):
        x = load_x(i if n_x > 1 else 0)
        if w_t:
            return lax.dot_general(x, w_refs[i][...], (((1,), (1,)), ((), ())),
                                   preferred_element_type=_F32)
        return jnp.dot(x, w_refs[i][...], preferred_element_type=_F32)

    def finish(accs):
        outs = epilogue(accs, [e[...] for e in e_refs])
        for o_ref, val, hw in zip(o_refs, outs, out_head_w):
            if hw:
                for h in range(o_ref.shape[0]):
                    o_ref[h] = val[:, h * hw:(h + 1) * hw].astype(o_ref.dtype)
            else:
                o_ref[...] = val.astype(o_ref.dtype)

    if n_k == 1:
        finish([partial(i) for i in range(n_pairs)])
        return

    k = pl.program_id(2)

    @pl.when(k == 0)
    def _():
        for i in range(n_pairs):
            acc_refs[i][...] = partial(i)

    @pl.when(k > 0)
    def _():
        for i in range(n_pairs):
            acc_refs[i][...] += partial(i)

    @pl.when(k == n_k - 1)
    def _():
        finish([a[...] for a in acc_refs])


def _matmul(xs, ws, extras, epilogue, outs, *, tm, tn, tk=None, col_major=False, w_t=False,
            name="matmul"):
    n_pairs = len(ws)
    assert len(xs) in (1, n_pairs)
    head_major = tuple(x.ndim == 3 for x in xs)
    m = xs[0].shape[1] if head_major[0] else xs[0].shape[0]
    kn = (lambda w: w.shape[::-1]) if w_t else (lambda w: w.shape)
    kdim, n = kn(ws[0])
    if tk is None:
        tk = kdim
    n_k = kdim // tk
    assert kdim % tk == 0 and m % tm == 0 and n % tn == 0
    n_j = n // tn
    tks = [tk if n_k > 1 else kn(w)[0] for w in ws]
    assert all(kn(w) == (kdim, n) for w in ws) or n_k == 1

    def order(f):
        return (lambda j, i, k: f(i, j, k)) if col_major else f

    in_specs = []
    for x, hm, tki in zip(xs, head_major, tks):
        if hm:
            assert n_k == 1
            in_specs.append(pl.BlockSpec((x.shape[0], tm, LANES), order(lambda i, j, k: (0, i, 0))))
        else:
            in_specs.append(pl.BlockSpec((tm, tki), order(lambda i, j, k: (i, k))))
    for w, tki in zip(ws, tks):
        if w_t:
            in_specs.append(pl.BlockSpec((tn, tki), order(lambda i, j, k: (j, k))))
        else:
            in_specs.append(pl.BlockSpec((tki, tn), order(lambda i, j, k: (k, j))))
    for arr, cols, imap in extras:
        in_specs.append(pl.BlockSpec((tm, cols), order(lambda i, j, k, imap=imap: imap(i, j))))

    out_specs, out_shapes, out_head_w = [], [], []
    for n_cols, dtype, hw in outs:
        to = n_cols // n_j
        out_head_w.append(hw)
        if hw:
            assert to % hw == 0
            out_specs.append(pl.BlockSpec((to // hw, tm, hw), order(lambda i, j, k: (j, i, 0))))
            out_shapes.append(jax.ShapeDtypeStruct((n_cols // hw, m, hw), dtype))
        else:
            out_specs.append(pl.BlockSpec((tm, to), order(lambda i, j, k: (i, j))))
            out_shapes.append(jax.ShapeDtypeStruct((m, n_cols), dtype))

    scratch = [pltpu.VMEM((tm, tn), _F32) for _ in range(n_pairs)] if n_k > 1 else []
    kern = functools.partial(
        _matmul_kernel, n_pairs=n_pairs, n_extras=len(extras), n_outs=len(outs), n_k=n_k,
        head_major=head_major, out_head_w=tuple(out_head_w), w_t=w_t, epilogue=epilogue)
    return pl.pallas_call(
        kern,
        grid=(n_j, m // tm, n_k) if col_major else (m // tm, n_j, n_k),
        in_specs=in_specs,
        out_specs=out_specs,
        out_shape=out_shapes,
        scratch_shapes=scratch,
        compiler_params=_compiler_params(("parallel", "parallel", "arbitrary")),
        name=name,
    )(*xs, *ws, *[e[0] for e in extras])


def _matmul_resid_kernel(x_ref, w_ref, r_ref, o_ref, acc_ref, *, n_k):
    k = pl.program_id(1)
    j = pl.program_id(2)
    part = jnp.dot(x_ref[...], w_ref[...], preferred_element_type=_F32)
    if n_k == 1:
        o_ref[...] = r_ref[...] + part
        return

    @pl.when(k == 0)
    def _():
        acc_ref[j] = part

    @pl.when((k > 0) & (k < n_k - 1))
    def _():
        acc_ref[j] += part

    @pl.when(k == n_k - 1)
    def _():
        o_ref[...] = r_ref[...] + (acc_ref[j] + part)


def _matmul_resid(x, w, resid, *, tm, tn, tk, name):
    m, kdim = x.shape
    n = w.shape[1]
    assert m % tm == 0 and n % tn == 0 and kdim % tk == 0
    n_j, n_k = n // tn, kdim // tk
    last = lambda i, k, j: (i, jnp.where(k == n_k - 1, j, 0))
    return pl.pallas_call(
        functools.partial(_matmul_resid_kernel, n_k=n_k),
        grid=(m // tm, n_k, n_j),
        in_specs=[pl.BlockSpec((tm, tk), lambda i, k, j: (i, k)),
                  pl.BlockSpec((tk, tn), lambda i, k, j: (k, j)),
                  pl.BlockSpec((tm, tn), last)],
        out_specs=pl.BlockSpec((tm, tn), last),
        out_shape=jax.ShapeDtypeStruct((m, n), _F32),
        scratch_shapes=[pltpu.VMEM((n_j, tm, tn), _F32)],
        compiler_params=_compiler_params(("parallel", "arbitrary", "arbitrary")),
        name=name,
    )(x, w, resid)


def _shift_rows(z, prev_row):
    rolled = pltpu.roll(z, 1, 0)
    row = lax.broadcasted_iota(jnp.int32, z.shape, 0)
    return jnp.where(row == 0, prev_row, rolled)


def _cumsum_rows(x):
    n = x.shape[0]
    row = lax.broadcasted_iota(jnp.int32, x.shape, 0)
    s = 1
    while s < n:
        x = x + jnp.where(row >= s, pltpu.roll(x, s, 0), 0.0)
        s *= 2
    return x


def bf(x):
    return x.astype(_BF16)


def _bdot(a, b):
    assert a.dtype == _BF16 and b.dtype == _BF16
    return jnp.dot(a, b, preferred_element_type=_F32)


def _bdot_nt(a, b):
    assert a.dtype == _BF16 and b.dtype == _BF16
    return lax.dot_general(a, b, (((1,), (1,)), ((), ())), preferred_element_type=_F32)


def _wkv_kernel(*refs, n_chunks, head, n_pp, aliased):
    (zr_ref, zk_ref, zv_ref, zl_ref, sprev_ref, lprev_ref, s0_ref,
     mu_ref, mul_ref, w0_ref, a0_ref, kkp_ref, kap_ref, rk_ref, lg_ref, lb_ref,
     w2_ref, a2_ref, g2_ref) = refs[:19]
    y_ref, sout_ref, st_ref, prev_ref, lprev_scr = refs[19 + aliased:]
    c_idx = pl.program_id(2)
    C = WKV_CHUNK
    P = 2 * head
    assert P == 2 * C
    tb = zr_ref.shape[0]
    n_sub = tb // C

    @pl.when(c_idx == 0)
    def _():
        st_ref[...] = s0_ref[0]
        prev_ref[...] = sprev_ref[0, 0]
        lprev_scr[...] = lprev_ref[0]

    lane = lax.broadcasted_iota(jnp.int32, (1, P), 1)
    m0 = (lane < head).astype(_F32)
    m1 = 1.0 - m0
    r2 = lax.broadcasted_iota(jnp.int32, (P, P), 0)
    c2 = lax.broadcasted_iota(jnp.int32, (P, P), 1)
    same = (r2 // C) == (c2 // C)
    tril_s = jnp.where(same & (c2 < r2), 1.0, 0.0)
    tril_i = jnp.where(same & (c2 <= r2), 1.0, 0.0)
    eye = jnp.where(r2 == c2, 1.0, 0.0)

    def stack(x):
        return jnp.concatenate([x * m0, x * m1], axis=0)

    def head_sum(x):
        s0 = jnp.sum(x * m0, axis=-1, keepdims=True)
        s1 = jnp.sum(x * m1, axis=-1, keepdims=True)
        return s0 * m0 + s1 * m1

    zl = zl_ref[...]
    xl = zl + (_shift_rows(zl, lprev_scr[...]) - zl) * mul_ref[...]
    lprev_scr[...] = zl[tb - 1:tb, :]
    nw = w2_ref.shape[0]
    na = a2_ref.shape[0]
    tw = jnp.tanh(xl[:, 0:nw]).astype(_BF16)
    xa = xl[:, nw:nw + na].astype(_BF16)
    sg = _sigmoid(xl[:, nw + na:]).astype(_BF16)

    pairs = []
    for pi in range(n_pp):
        ls = slice(pi * P, (pi + 1) * P)
        zr, zk, zv = zr_ref[:, ls], zk_ref[:, ls], zv_ref[:, ls]
        r = zr + (_shift_rows(zr, prev_ref[0:1, ls]) - zr) * mu_ref[0:1, ls]
        k = zk + (_shift_rows(zk, prev_ref[1:2, ls]) - zk) * mu_ref[1:2, ls]
        v = zv + (_shift_rows(zv, prev_ref[2:3, ls]) - zv) * mu_ref[2:3, ls]
        prev_ref[0:1, ls] = zr[tb - 1:tb, :]
        prev_ref[1:2, ls] = zk[tb - 1:tb, :]
        prev_ref[2:3, ls] = zv[tb - 1:tb, :]
        u = -(w0_ref[:, ls] + jnp.dot(tw, w2_ref[:, ls], preferred_element_type=_F32))
        softplus = jnp.maximum(u, 0.0) + jnp.log(1.0 + jnp.exp(-jnp.abs(u)))
        logw = -jnp.exp(-softplus - 0.5)
        a = _sigmoid(a0_ref[:, ls] + jnp.dot(xa, a2_ref[:, ls], preferred_element_type=_F32))
        g = jnp.dot(sg, g2_ref[:, ls], preferred_element_type=_F32)
        kk = k * kkp_ref[:, ls]
        kk = kk / jnp.maximum(jnp.sqrt(head_sum(kk * kk)), 1e-12)
        kh = k * (1.0 + (a - 1.0) * kap_ref[:, ls])
        pairs.append(dict(ls=ls, r=r, v=v, logw=logw, kk=kk, kh=kh, beta=kk * a, g=g,
                          bonus=head_sum(r * kh * rk_ref[:, ls]) * v))

    states = [st_ref[pi] for pi in range(n_pp)]
    group = n_sub
    for sc0 in range(0, n_sub, group):
        items = []
        for sc in range(sc0, sc0 + group):
            rs = slice(sc * C, (sc + 1) * C)
            for pi, pr in enumerate(pairs):
                lw = pr["logw"][rs]
                cum = _cumsum_rows(lw)
                cum_c = cum[C - 1:C, :]
                e_neg = jnp.exp(-cum)
                e_end = jnp.exp(cum_c - cum)
                r2s = stack(pr["r"][rs] * jnp.exp(cum))
                items.append(dict(
                    pi=pi, rs=rs, r2s=r2s,
                    ar=bf(jnp.concatenate([stack(-pr["kk"][rs] * jnp.exp(cum - lw)), r2s], axis=0)),
                    bk=bf(jnp.concatenate([stack(pr["beta"][rs] * e_neg), stack(pr["kh"][rs] * e_neg)],
                                          axis=0)),
                    v2s=bf(stack(pr["v"][rs])),
                    bh2t=bf(stack(pr["beta"][rs] * e_end).T),
                    kh2t=bf(stack(pr["kh"][rs] * e_end).T),
                    dec_col=jnp.sum(eye * jnp.exp(cum_c), axis=-1, keepdims=True)))
        for it in items:
            nn = _bdot_nt(it["ar"], it["bk"])
            n_ba = nn[:P, :P] * tril_s
            it["n_ka"] = bf(nn[:P, P:] * tril_s)
            it["p_br"] = bf(nn[P:, :P] * tril_i)
            it["p_kr"] = bf(nn[P:, P:] * tril_i)
            it["t_inv"] = eye + n_ba
            it["pw"] = bf(n_ba)
        for it in items:
            it["pw"] = bf(_bdot(it["pw"], it["pw"]))
        s = 2
        while s < C:
            for it in items:
                sq = _bdot(it["pw"], jnp.concatenate([it["pw"], bf(it["t_inv"])], axis=1))
                it["pw"] = bf(sq[:, :P])
                it["t_inv"] = it["t_inv"] + sq[:, P:]
            s *= 2
        for it in items:
            it["nkv"] = bf(_bdot(it["n_ka"], it["v2s"]))
        for it in items:
            it["tt"] = bf(_bdot(bf(it["t_inv"]), jnp.concatenate([it["ar"][:P], it["nkv"]], axis=1)))
        for it in items:
            it["mg"] = _bdot(it["bh2t"], it["tt"])
        for it in items:
            it["pg"] = _bdot(it["p_br"], it["tt"])
        for it in items:
            it["g_c"] = it["mg"][:, P:] + _bdot(it["kh2t"], it["v2s"])
        for it in items:
            it["yg"] = it["pg"][:, P:] + _bdot(it["p_kr"], it["v2s"])
        for it in items:
            pi, rs, pr = it["pi"], it["rs"], pairs[it["pi"]]
            ls = pr["ls"]
            st = states[pi]
            st_b = bf(st)
            y2 = _bdot(bf(it["r2s"] + it["pg"][:, :P]), st_b) + it["yg"]
            states[pi] = it["dec_col"] * st + _bdot(bf(it["mg"][:, :P]), st_b) + it["g_c"]
            y = y2[0:C, :] + y2[C:2 * C, :]
            mean = head_sum(y) * (1.0 / head)
            yc = y - mean
            var = head_sum(yc * yc) * (1.0 / head)
            yn = yc * lax.rsqrt(var + GN_EPS) * lg_ref[:, ls] + lb_ref[:, ls]
            y_ref[rs, ls] = ((yn + pr["bonus"][rs]) * pr["g"][rs]).astype(y_ref.dtype)
    for pi in range(n_pp):
        st_ref[pi] = states[pi]

    @pl.when(c_idx == n_chunks - 1)
    def _():
        sout_ref[0] = st_ref[...]


def _wkv(z, shift_prev, s0_bd, params, *, batch, seq, row0, head, d_a, lora_sizes, m_out, out_buf):
    pw = 2 * head
    n_pairs = d_a // pw
    n_pp = min(WKV_PAIRS_PER_STEP, n_pairs)
    gw = n_pp * pw
    n_groups = n_pairs // n_pp
    lp = sum(lora_sizes)
    assert (3 * d_a) % lp == 0
    tb = _pick_tile(seq, WKV_ROWS_PER_STEP, WKV_CHUNK)
    n_chunks = seq // tb
    assert row0 % tb == 0
    rb0 = row0 // tb
    rpb = seq // tb
    aliased = out_buf is not None

    def rows(b, c):
        return rb0 + b * rpb + c

    in_specs = [
        pl.BlockSpec((tb, gw), lambda b, p, c: (rows(b, c), p)),
        pl.BlockSpec((tb, gw), lambda b, p, c: (rows(b, c), n_groups + p)),
        pl.BlockSpec((tb, gw), lambda b, p, c: (rows(b, c), 2 * n_groups + p)),
        pl.BlockSpec((tb, lp), lambda b, p, c: (rows(b, c), 3 * d_a // lp)),
        pl.BlockSpec((1, 1, SUBLANES, gw), lambda b, p, c: (b, p, 0, 0)),
        pl.BlockSpec((1, 1, lp), lambda b, p, c: (b, 0, 0)),
        pl.BlockSpec((1, n_pp, pw, pw), lambda b, p, c: (b, p, 0, 0)),
        pl.BlockSpec((SUBLANES, gw), lambda b, p, c: (p, 0)),
        pl.BlockSpec((1, lp), lambda b, p, c: (0, 0)),
    ]
    for _ in range(7):
        in_specs.append(pl.BlockSpec((1, gw), lambda b, p, c: (0, p)))
    nw, na, ng = lora_sizes
    in_specs += [pl.BlockSpec((nw, gw), lambda b, p, c: (0, p)),
                 pl.BlockSpec((na, gw), lambda b, p, c: (0, p)),
                 pl.BlockSpec((ng, gw), lambda b, p, c: (0, p))]
    args = [z, z, z, z, shift_prev["rkv"], shift_prev["lora"], s0_bd,
            params["mu_rkv"], params["mu_lora"], params["w0"], params["a0"], params["k_k"],
            params["k_a"], params["r_k"], params["lnx_g"], params["lnx_b"],
            params["w2"], params["a2"], params["g2"]]
    aliases = {}
    if aliased:
        in_specs.append(pl.BlockSpec(memory_space=pl.ANY))
        aliases = {len(args): 0}
        args.append(out_buf)
    out_specs = [pl.BlockSpec((tb, gw), lambda b, p, c: (rows(b, c), p)),
                 pl.BlockSpec((1, n_pp, pw, pw), lambda b, p, c: (b, p, 0, 0))]
    out_shape = [jax.ShapeDtypeStruct((m_out, d_a), _BF16),
                 jax.ShapeDtypeStruct((batch, n_pairs, pw, pw), _F32)]
    kern = functools.partial(_wkv_kernel, n_chunks=n_chunks, head=head, n_pp=n_pp, aliased=int(aliased))
    return pl.pallas_call(
        kern,
        grid=(batch, n_groups, n_chunks),
        in_specs=in_specs,
        out_specs=out_specs,
        out_shape=out_shape,
        scratch_shapes=[pltpu.VMEM((n_pp, pw, pw), _F32),
                        pltpu.VMEM((SUBLANES, gw), _F32),
                        pltpu.VMEM((1, lp), _F32)],
        input_output_aliases=aliases,
        compiler_params=_compiler_params(("parallel", "parallel", "arbitrary")),
        name="wkv7_chunked",
    )(*args)


def _dsa_kernel(*refs, tq, l_ext, l_true, q_pos0, topk, idx_scale, n_kv, dh, d_idx, aliased):
    q_ref, k_ref, v_ref, qi_ref, wi_ref, ki_ref = refs[:6]
    o_ref, kbf, vbf, kibf, wib, isc_scr, key_scr, dm_scr = refs[6 + aliased:]
    qb = pl.program_id(1)
    n_q = q_ref.shape[0]
    n_i = qi_ref.shape[0]
    rep = n_q // n_kv

    @pl.when(qb == 0)
    def _():
        for g in range(n_kv):
            kbf[g] = k_ref[0:l_ext, g * dh:(g + 1) * dh].astype(_BF16)
            vbf[g] = v_ref[0:l_ext, g * dh:(g + 1) * dh].astype(_BF16)
        kibf[...] = ki_ref[0:l_ext, 0:d_idx].astype(_BF16)

    wi = wi_ref[...]
    for hi in range(n_i):
        wib[hi] = jnp.broadcast_to(wi[:, hi:hi + 1], (tq, LANES))
    isc_scr[...] = jnp.zeros((tq, l_ext), _F32)
    ki = kibf[...]

    hs = max(1, min(4, 512 // tq))
    assert n_i % hs == 0

    def idx_body(i, carry):
        qs = qi_ref[pl.ds(i * hs, hs)].reshape(hs * tq, d_idx)
        sc = lax.dot_general(qs, ki, (((1,), (1,)), ((), ())), preferred_element_type=_F32)
        acc = isc_scr[...]
        for j in range(hs):
            acc = acc + (jnp.maximum(sc[j * tq:(j + 1) * tq], 0.0)
                         * jnp.tile(wib[i * hs + j], (1, l_ext // LANES)))
        isc_scr[...] = acc
        return carry

    lax.fori_loop(0, n_i // hs, idx_body, 0)

    q_pos = q_pos0 + qb * tq + lax.broadcasted_iota(jnp.int32, (tq, l_ext), 0)
    k_pos = lax.broadcasted_iota(jnp.int32, (tq, l_ext), 1)
    cs = CHUNK.bit_length() - 1
    assert 1 << cs == CHUNK
    allowed = ((k_pos >> cs) <= (q_pos >> cs)) & (k_pos < l_true)
    bits = pltpu.bitcast(isc_scr[...] * idx_scale, jnp.int32)
    int_min = jnp.int32(-2 ** 31)
    key_scr[...] = jnp.where(allowed, jnp.where(bits < 0, bits ^ jnp.int32(0x7FFFFFFF), bits), int_min)

    def count_ge(cand):
        return jnp.sum(jnp.where(key_scr[...] >= cand, 1.0, 0.0), axis=-1, keepdims=True)

    kf = jnp.float32(topk)
    zero = jnp.zeros((tq, 1), jnp.int32)
    thr = jnp.where(count_ge(zero) >= kf, zero, jnp.full((tq, 1), int_min, jnp.int32))

    def thr_body(i, thr):
        cand = thr + (jnp.int32(1) << (jnp.int32(30) - i))
        return jnp.where(count_ge(cand) >= kf, cand, thr)

    thr = lax.fori_loop(0, 31, thr_body, thr)
    key = key_scr[...]
    n_gt = jnp.sum(jnp.where(key > thr, 1.0, 0.0), axis=-1, keepdims=True)
    n_eq = jnp.sum(jnp.where(key == thr, 1.0, 0.0), axis=-1, keepdims=True)
    need = kf - n_gt
    n_bits = max(1, (l_ext - 1).bit_length())
    full = jnp.full((tq, 1), 1 << n_bits, jnp.int32)

    def tie_bound():
        def count_tie_below(bound):
            hit = jnp.where(key_scr[...] == thr, jnp.where(k_pos < bound, 1.0, 0.0), 0.0)
            return jnp.sum(hit, axis=-1, keepdims=True)

        def tbody(i, bound):
            cand = bound + (jnp.int32(1) << (jnp.int32(n_bits - 1) - i))
            return jnp.where(count_tie_below(cand) <= need, cand, bound)

        return lax.fori_loop(0, n_bits, tbody, zero)

    has_excess_ties = jnp.max(n_eq - need) > 0.0
    bound = lax.cond(has_excess_ties, tie_bound, lambda: full)
    sel = allowed & ((key > thr) | ((key == thr) & (k_pos < bound)))
    dm_scr[...] = jnp.where(sel, jnp.abs(q_pos - k_pos).astype(_F32), DIST_EXCLUDED)

    def qk(h, kg):
        return lax.dot_general(q_ref[h], kg, (((1,), (1,)), ((), ())), preferred_element_type=_F32)

    def head_body(g, carry):
        kg, vg = kbf[g], vbf[g]
        dm = dm_scr[...]
        s_next = qk(g * rep, kg)
        for j in range(rep):
            h = g * rep + j
            s_cur = s_next
            if j + 1 < rep:
                s_next = qk(h + 1, kg)
            slope2 = jnp.exp2(jnp.full((1, 1), h + 1, jnp.int32).astype(_F32) * (-8.0 / n_q)) * LOG2_E
            s = s_cur - slope2 * dm
            m = jnp.max(s, axis=-1, keepdims=True)
            p = jnp.exp2(s - m)
            l = jnp.sum(p, axis=-1, keepdims=True)
            o = jnp.dot(p.astype(_BF16), vg, preferred_element_type=_F32)
            o_ref[h] = (o / l).astype(o_ref.dtype)
        return carry

    lax.fori_loop(0, n_kv, head_body, 0)


def _dsa(q_hm, qi_hm, kv_src, k_col, v_col, ki_src, ki_col, wi_src, wi_col, *, batch, n_qb, tq,
         q_row0, q_stride, kv_rows, kv_stride, l_ext, l_true, q_pos0, topk, n_kv, m_out, out_buf):
    n_q, _, dh = q_hm.shape
    n_i, _, d_idx = qi_hm.shape
    assert q_row0 % tq == 0 and q_stride % tq == 0 and l_ext % LANES == 0 and l_ext <= kv_rows
    qr0, qst = q_row0 // tq, q_stride // tq
    aliased = out_buf is not None
    kern = functools.partial(
        _dsa_kernel, tq=tq, l_ext=l_ext, l_true=l_true, q_pos0=q_pos0, topk=topk,
        idx_scale=float(n_i) ** -0.5 * float(d_idx) ** -0.5, n_kv=n_kv, dh=dh, d_idx=d_idx,
        aliased=int(aliased))
    qrow = lambda b, i: qr0 + b * qst + i
    in_specs = [
        pl.BlockSpec((n_q, tq, dh), lambda b, i: (0, qrow(b, i), 0)),
        pl.BlockSpec((kv_rows, n_kv * dh), lambda b, i: (b * kv_stride, k_col)),
        pl.BlockSpec((kv_rows, n_kv * dh), lambda b, i: (b * kv_stride, v_col)),
        pl.BlockSpec((n_i, tq, d_idx), lambda b, i: (0, qrow(b, i), 0)),
        pl.BlockSpec((tq, LANES), lambda b, i: (qrow(b, i), wi_col)),
        pl.BlockSpec((kv_rows, LANES), lambda b, i: (b * kv_stride, ki_col)),
    ]
    args = [q_hm, kv_src, kv_src, qi_hm, wi_src, ki_src]
    aliases = {}
    if aliased:
        in_specs.append(pl.BlockSpec(memory_space=pl.ANY))
        aliases = {len(args): 0}
        args.append(out_buf)
    return pl.pallas_call(
        kern,
        grid=(batch, n_qb),
        in_specs=in_specs,
        out_specs=pl.BlockSpec((n_q, tq, dh), lambda b, i: (0, qrow(b, i), 0)),
        out_shape=jax.ShapeDtypeStruct((n_q, m_out, dh), _BF16),
        scratch_shapes=[pltpu.VMEM((n_kv, l_ext, dh), _BF16), pltpu.VMEM((n_kv, l_ext, dh), _BF16),
                        pltpu.VMEM((l_ext, d_idx), _BF16), pltpu.VMEM((n_i, tq, LANES), _F32),
                        pltpu.VMEM((tq, l_ext), _F32), pltpu.VMEM((tq, l_ext), jnp.int32),
                        pltpu.VMEM((tq, l_ext), _F32)],
        input_output_aliases=aliases,
        compiler_params=_compiler_params(("parallel", "arbitrary")),
        name="dsa_attention",
    )(*args)


def _pad_rows(w, rows):
    return jnp.pad(w, ((0, rows - w.shape[0]), (0, 0)))


def _pad_cols(w, cols):
    return jnp.pad(w, ((0, 0), (0, cols - w.shape[1])))


def _to_block_diag_t(s, head):
    b, h = s.shape[:2]
    st = jnp.swapaxes(s, -1, -2).reshape(b, h // 2, 2, head, head).astype(_F32)
    z = jnp.zeros_like(st[:, :, 0])
    top = jnp.concatenate([st[:, :, 0], z], axis=-1)
    bot = jnp.concatenate([z, st[:, :, 1]], axis=-1)
    return jnp.concatenate([top, bot], axis=-2)


def _from_block_diag_t(sbd, head):
    b, hp = sbd.shape[:2]
    s0 = sbd[:, :, :head, :head]
    s1 = sbd[:, :, head:, head:]
    st = jnp.stack([s0, s1], axis=2).reshape(b, hp * 2, head, head)
    return jnp.swapaxes(st, -1, -2)


def kernel(x_prompt, x_sample, cache_k, cache_v, cache_kidx, state_shift, state_wkv, norm1_g, w_in, mu_shift, w0, w2, a0, a2, g2, k_k, k_a, r_k, lnx_g, lnx_b, w_br_a, w_br_b, w_out, norm2_g, w_ffn_gate, w_ffn_up, w_ffn_down, norm_f_g):
    depth = w_in.shape[0]
    assert depth == 1
    bp, tp, d_model = x_prompt.shape
    bs, ts, _ = x_sample.shape
    _, _, past, n_kv, dh = cache_k.shape
    d_idx = cache_kidx.shape[-1]
    d_shift = state_shift.shape[-1]
    _, _, h_a, head, _ = state_wkv.shape
    d_a = h_a * head
    r_w, r_a, r_g = w2.shape[1], a2.shape[1], g2.shape[1]
    n_q = w_br_b.shape[1] // dh
    d_in = w_in.shape[-1]
    h_i = (d_in - d_shift - n_q * dh - 2 * n_kv * dh - d_idx - 2 * d_model) // (d_idx + 1)
    d_ff = w_ffn_gate.shape[-1]
    kvw = n_kv * dh
    l = 0
    mp, ms = bp * tp, bs * ts
    m_tot = mp + ms
    xp2, xs2 = x_prompt.reshape(mp, d_model), x_sample.reshape(ms, d_model)
    tm_mm = _pick_tile(m_tot, ROW_TILE_MM, 2 * SUBLANES)

    win_t = jnp.swapaxes(w_in[l], 0, 1)
    o = 0
    def take(n):
        nonlocal o
        blk = win_t[o:o + n]
        o += n
        return blk
    nwp, nap, ngp = _round_up(r_w, LANES), _round_up(r_a, LANES), _round_up(r_g, LANES)
    lp = nwp + nap + ngp
    w_rkv, w_lw, w_la, w_lg = take(3 * d_a), take(r_w), take(r_a), take(r_g)
    w_rkvl = jnp.concatenate([w_rkv, _pad_rows(w_lw, nwp), _pad_rows(w_la, nap), _pad_rows(w_lg, ngp)],
                             axis=0).astype(_BF16)
    w_q = take(n_q * dh).astype(_BF16)
    w_k, w_v = take(kvw), take(kvw)
    w_qi = take(h_i * d_idx).astype(_BF16)
    w_ki, w_wi = take(d_idx), take(h_i)
    w_kv = jnp.concatenate([w_k, w_v, _pad_rows(w_ki, LANES), _pad_rows(w_wi, LANES)], axis=0).astype(_BF16)
    w_gates = jnp.concatenate([take(d_model), take(d_model)], axis=0).astype(_BF16)

    h1 = _rmsnorm_cat(xp2, xs2, norm1_g[l], _BF16)
    ident = lambda accs, ex: [accs[0]]
    n_rkvl = 3 * d_a + lp
    (z_a,) = _matmul([h1], [w_rkvl], [], ident, [(n_rkvl, _F32, None)],
                     tm=tm_mm, tn=_pick_tile(n_rkvl, COL_TILE_MM, LANES), w_t=True, name="proj_rkv_lora")
    n_kvz = 2 * kvw + 2 * LANES
    (z_kv,) = _matmul([h1], [w_kv], [], ident, [(n_kvz, _F32, None)],
                      tm=tm_mm, tn=_pick_tile(n_kvz, COL_TILE_KV, LANES), w_t=True, name="proj_kv_idx")
    q_scale = float(dh) ** -0.5 * LOG2_E
    (q_hm,) = _matmul([h1], [w_q], [], lambda accs, ex: [accs[0] * q_scale],
                      [(n_q * dh, _BF16, dh)], tm=tm_mm, tn=_pick_tile(n_q * dh, COL_TILE_MM, dh), w_t=True,
                      name="proj_q")
    (qi_hm,) = _matmul([h1], [w_qi], [], ident, [(h_i * d_idx, _BF16, d_idx)],
                       tm=tm_mm, tn=_pick_tile(h_i * d_idx, COL_TILE_MM, LANES), w_t=True, name="proj_qi")
    (gates,) = _matmul([h1], [w_gates], [], ident, [(2 * d_model, _BF16, None)],
                       tm=tm_mm, tn=_pick_tile(2 * d_model, COL_TILE_MM, LANES), w_t=True, name="proj_gates")
    kx, vx = z_kv[:, :kvw], z_kv[:, kvw:2 * kvw]
    ki_new = z_kv[:, 2 * kvw:2 * kvw + d_idx]

    mu = mu_shift[l]
    def lora_row(vec):
        return jnp.concatenate([
            jnp.pad(vec[3 * d_a:3 * d_a + r_w], (0, nwp - r_w)),
            jnp.pad(vec[3 * d_a + r_w:3 * d_a + r_w + r_a], (0, nap - r_a)),
            jnp.pad(vec[3 * d_a + r_w + r_a:], (0, ngp - r_g))])
    n_pairs = d_a // (2 * head)
    gw = min(WKV_PAIRS_PER_STEP, n_pairs) * 2 * head
    n_groups = d_a // gw
    def rkv_rows(vec):
        lead = vec.shape[:-1]
        x = vec[..., :3 * d_a].reshape(lead + (3, n_groups, gw))
        x = jnp.moveaxis(x, -3, -2)
        return jnp.pad(x, [(0, 0)] * (len(lead) + 1) + [(0, SUBLANES - 3), (0, 0)])
    row = lambda v: v[l].reshape(1, d_a).astype(_F32)
    params = dict(
        mu_rkv=rkv_rows(mu).reshape(n_groups * SUBLANES, gw),
        mu_lora=lora_row(mu).reshape(1, lp),
        w0=row(w0), a0=row(a0), k_k=row(k_k), k_a=row(k_a),
        r_k=r_k[l].reshape(1, d_a).astype(_F32), lnx_g=row(lnx_g), lnx_b=row(lnx_b),
        w2=_pad_rows(w2[l], nwp).astype(_BF16), a2=_pad_rows(a2[l], nap).astype(_BF16),
        g2=_pad_rows(g2[l], ngp).astype(_BF16))

    def shift_state(s):
        s = s[:, 0]
        return dict(rkv=rkv_rows(s), lora=jax.vmap(lora_row)(s).reshape(-1, 1, lp))

    wkv_kw = dict(head=head, d_a=d_a, lora_sizes=(nwp, nap, ngp), m_out=m_tot)
    y_a, sp_bd = _wkv(z_a, shift_state(jnp.zeros((bp, 1, d_shift), _F32)),
                      jnp.zeros((bp, n_pairs, 2 * head, 2 * head), _F32), params,
                      batch=bp, seq=tp, row0=0, out_buf=jnp.zeros((m_tot, d_a), _BF16), **wkv_kw)
    y_a, ss_bd = _wkv(z_a, shift_state(state_shift[l]), _to_block_diag_t(state_wkv[l], head), params,
                      batch=bs, seq=ts, row0=mp, out_buf=y_a, **wkv_kw)
    wkv_p = _from_block_diag_t(sp_bd, head)
    wkv_s = _from_block_diag_t(ss_bd, head)

    topk_p = min(TOPK_MAX, tp // 4)
    tq_p = _pick_tile(tp, DSA_QUERY_TILE, CHUNK)
    n_grp = max(1, min(DSA_GROUPS, tp // tq_p))
    grp = tp // n_grp
    wi_col = (2 * kvw + LANES) // LANES
    ki_col = 2 * kvw // LANES
    y_b = jnp.zeros((n_q, m_tot, dh), _BF16)
    for gi in range(n_grp):
        l_ext = (gi + 1) * grp
        kv_rows = l_ext
        while tp % kv_rows:
            kv_rows += grp
        y_b = _dsa(q_hm, qi_hm, z_kv, 0, 1, z_kv, ki_col, z_kv, wi_col,
                   batch=bp, n_qb=grp // tq_p, tq=tq_p, q_row0=gi * grp, q_stride=tp,
                   kv_rows=kv_rows, kv_stride=tp // kv_rows, l_ext=l_ext, l_true=l_ext,
                   q_pos0=gi * grp, topk=topk_p, n_kv=n_kv, m_out=m_tot, out_buf=y_b)
    ls = past + ts
    lps = _round_up(ls, LANES)
    def with_cache(cache, new, width):
        x = jnp.concatenate([cache.reshape(bs, past, width), new.reshape(bs, ts, width)], axis=1)
        return jnp.pad(x, ((0, 0), (0, lps - ls), (0, 0))).reshape(bs * lps, width)
    kv_s = jnp.concatenate([with_cache(cache_k[l], kx[mp:], kvw), with_cache(cache_v[l], vx[mp:], kvw)], axis=1)
    ki_s = _pad_cols(with_cache(cache_kidx[l], ki_new[mp:], d_idx), LANES)
    y_b = _dsa(q_hm, qi_hm, kv_s, 0, 1, ki_s, 0, z_kv, wi_col,
               batch=bs, n_qb=1, tq=ts, q_row0=mp, q_stride=ts, kv_rows=lps, kv_stride=1, l_ext=lps, l_true=ls,
               q_pos0=past, topk=min(TOPK_MAX, ls // 4), n_kv=n_kv, m_out=m_tot, out_buf=y_b)

    tn_mg = _pick_tile(d_model, COL_TILE_MM, LANES)
    nj_mg = d_model // tn_mg
    def merge_ep(accs, ex):
        return [_sigmoid(ex[0].astype(_F32)) * accs[0] + _sigmoid(ex[1].astype(_F32)) * accs[1]]
    (merged,) = _matmul([y_a, y_b], [w_br_a[l].astype(_BF16), w_br_b[l].astype(_BF16)],
                        [(gates, tn_mg, lambda i, j: (i, j)), (gates, tn_mg, lambda i, j: (i, nj_mg + j))],
                        merge_ep, [(d_model, _BF16, None)], tm=tm_mm, tn=tn_mg, name="branch_merge")
    tn_res = _pick_tile(d_model, COL_TILE_MM, LANES)
    tm_io = _pick_tile(math.gcd(mp, ms), ROW_TILE_NORM, 2 * SUBLANES)
    n_io_p = mp // tm_io
    def resid2_ep(accs, ex):
        return [jnp.where(pl.program_id(1) < n_io_p, ex[0], ex[1]) + accs[0]]
    (x1,) = _matmul([merged], [w_out[l].astype(_BF16)],
                    [(xp2, tn_res, lambda i, j: (jnp.minimum(i, n_io_p - 1), j)),
                     (xs2, tn_res, lambda i, j: (jnp.maximum(i - n_io_p, 0), j))],
                    resid2_ep, [(d_model, _F32, None)], tm=tm_io, tn=tn_res, col_major=True,
                    name="out_proj")

    h2 = _rmsnorm_rows(x1, norm2_g[l], _BF16, 0, m_tot)
    tn_ff = 2 * LANES
    d_ffp = _round_up(d_ff, tn_ff)
    ff_pad_cols = jnp.zeros((d_model, d_ffp - d_ff), _BF16)
    wg = jnp.concatenate([w_ffn_gate[l].astype(_BF16), ff_pad_cols], axis=1)
    wu = jnp.concatenate([w_ffn_up[l].astype(_BF16), ff_pad_cols], axis=1)
    def swiglu_ep(accs, ex):
        return [accs[0] * _sigmoid(accs[0]) * accs[1]]
    (u,) = _matmul([h2], [wg, wu], [], swiglu_ep, [(d_ffp, _BF16, None)],
                   tm=_pick_tile(m_tot, ROW_TILE_FFN_UP, 2 * SUBLANES), tn=tn_ff, name="ffn_up")
    wd = jnp.concatenate([w_ffn_down[l].astype(_BF16), jnp.zeros((d_ffp - d_ff, d_model), _BF16)], axis=0)
    tk_dn = d_ffp // 2 if (d_ffp // 2) % LANES == 0 else d_ffp
    x2 = _matmul_resid(u, wd, x1, tm=tm_mm, tn=_pick_tile(d_model, COL_TILE_FFN_DOWN, LANES), tk=tk_dn, name="ffn_down")
    y_p = _rmsnorm_rows(x2, norm_f_g, _F32, 0, mp).reshape(bp, tp, d_model)
    y_s = _rmsnorm_rows(x2, norm_f_g, _F32, mp, ms).reshape(bs, ts, d_model)

    kx4 = lambda x, b, t: x.reshape(1, b, t, n_kv, dh)
    def zsh_last(rows):
        zl = z_a[rows]
        return jnp.concatenate([zl[:, :3 * d_a + r_w], zl[:, 3 * d_a + nwp:3 * d_a + nwp + r_a],
                                zl[:, 3 * d_a + nwp + nap:3 * d_a + nwp + nap + r_g]], axis=-1)
    last_p = jnp.arange(bp) * tp + (tp - 1)
    last_s = mp + jnp.arange(bs) * ts + (ts - 1)
    return (y_p, y_s,
            kx4(kx[:mp], bp, tp), kx4(vx[:mp], bp, tp), ki_new[:mp].reshape(1, bp, tp, d_idx),
            zsh_last(last_p).reshape(1, bp, 1, d_shift), wkv_p[None],
            kx4(kx[mp:], bs, ts), kx4(vx[mp:], bs, ts), ki_new[mp:].reshape(1, bs, ts, d_idx),
            zsh_last(last_s).reshape(1, bs, 1, d_shift), wkv_s[None])
```

```python
import functools
import math

import jax
import jax.numpy as jnp
from jax import lax
from jax.experimental import pallas as pl
from jax.experimental.pallas import tpu as pltpu

CHUNK = 64
NORM_EPS = 1e-6
GN_EPS = 64e-5
TOPK_MAX = 256

LANES = 128
SUBLANES = 8
V7X_VMEM_LIMIT_BYTES = 56 * 1024 * 1024

WKV_CHUNK = 64
WKV_PAIRS_PER_STEP = 8
WKV_ROWS_PER_STEP = 256
DIST_EXCLUDED = 1e30
DSA_GROUPS = 8
DSA_QUERY_TILE = 256
LOG2_E = 1.4426950408889634

ROW_TILE_NORM = 512
ROW_TILE_MM = 1024
COL_TILE_MM = 1024
COL_TILE_KV = 1280
ROW_TILE_FFN_UP = 1536
COL_TILE_FFN_DOWN = 512

_F32 = jnp.float32
_BF16 = jnp.bfloat16


def _round_up(n, m):
    return (n + m - 1) // m * m


def _pick_tile(n, pref, align):
    if n <= pref:
        return n
    t = pref // align * align
    while t >= align:
        if n % t == 0:
            return t
        t -= align
    return n


def _compiler_params(semantics):
    return pltpu.CompilerParams(dimension_semantics=semantics,
                                vmem_limit_bytes=V7X_VMEM_LIMIT_BYTES)


def _sigmoid(x):
    return 1.0 / (1.0 + jnp.exp(-x))


def _rms(x, g):
    x = x.astype(_F32)
    ms = jnp.mean(x * x, axis=-1, keepdims=True)
    return x * lax.rsqrt(ms + NORM_EPS) * g


def _rmsnorm_cat_kernel(xa_ref, xb_ref, g_ref, o_ref, *, n_a):
    i = pl.program_id(0)

    @pl.when(i < n_a)
    def _():
        o_ref[...] = _rms(xa_ref[...], g_ref[...]).astype(o_ref.dtype)

    @pl.when(i >= n_a)
    def _():
        o_ref[...] = _rms(xb_ref[...], g_ref[...]).astype(o_ref.dtype)


def _rmsnorm_cat(xa, xb, g, out_dtype):
    (ma, d), mb = xa.shape, xb.shape[0]
    tm = _pick_tile(math.gcd(ma, mb), ROW_TILE_NORM, SUBLANES)
    n_a = ma // tm
    return pl.pallas_call(
        functools.partial(_rmsnorm_cat_kernel, n_a=n_a),
        grid=((ma + mb) // tm,),
        in_specs=[pl.BlockSpec((tm, d), lambda i: (jnp.minimum(i, n_a - 1), 0)),
                  pl.BlockSpec((tm, d), lambda i: (jnp.maximum(i - n_a, 0), 0)),
                  pl.BlockSpec((1, d), lambda i: (0, 0))],
        out_specs=pl.BlockSpec((tm, d), lambda i: (i, 0)),
        out_shape=jax.ShapeDtypeStruct((ma + mb, d), out_dtype),
        compiler_params=_compiler_params(("arbitrary",)),
        name="rmsnorm_cat",
    )(xa, xb, g.reshape(1, d).astype(_F32))


def _rmsnorm_rows_kernel(x_ref, g_ref, o_ref):
    o_ref[...] = _rms(x_ref[...], g_ref[...]).astype(o_ref.dtype)


def _rmsnorm_rows(x, g, out_dtype, row0, rows):
    d = x.shape[1]
    tm = _pick_tile(math.gcd(row0, rows) if row0 else rows, ROW_TILE_NORM, SUBLANES)
    rb0 = row0 // tm
    return pl.pallas_call(
        _rmsnorm_rows_kernel,
        grid=(rows // tm,),
        in_specs=[pl.BlockSpec((tm, d), lambda i: (rb0 + i, 0)),
                  pl.BlockSpec((1, d), lambda i: (0, 0))],
        out_specs=pl.BlockSpec((tm, d), lambda i: (i, 0)),
        out_shape=jax.ShapeDtypeStruct((rows, d), out_dtype),
        compiler_params=_compiler_params(("parallel",)),
        name="rmsnorm_rows",
    )(x, g.reshape(1, d).astype(_F32))


def _matmul_kernel(*refs, n_pairs, n_extras, n_outs, n_k, head_major, out_head_w, epilogue):
    n_x = len(head_major)
    x_refs = refs[:n_x]
    w_refs = refs[n_x:n_x + n_pairs]
    e_refs = refs[n_x + n_pairs:n_x + n_pairs + n_extras]
    o_refs = refs[n_x + n_pairs + n_extras:n_x + n_pairs + n_extras + n_outs]
    acc_refs = refs[n_x + n_pairs + n_extras + n_outs:]

    def load_x(i):
        if head_major[i]:
            xr = x_refs[i]
            return jnp.concatenate([xr[h] for h in range(xr.shape[0])], axis=1)
        return x_refs[i][...]

    def partial(i):
        return jnp.dot(load_x(i if n_x > 1 else 0), w_refs[i][...], preferred_element_type=_F32)

    def finish(accs):
        outs = epilogue(accs, [e[...] for e in e_refs])
        for o_ref, val, hw in zip(o_refs, outs, out_head_w):
            if hw:
                for h in range(o_ref.shape[0]):
                    o_ref[h] = val[:, h * hw:(h + 1) * hw].astype(o_ref.dtype)
            else:
                o_ref[...] = val.astype(o_ref.dtype)

    if n_k == 1:
        finish([partial(i) for i in range(n_pairs)])
        return

    k = pl.program_id(2)

    @pl.when(k == 0)
    def _():
        for i in range(n_pairs):
            acc_refs[i][...] = partial(i)

    @pl.when(k > 0)
    def _():
        for i in range(n_pairs):
            acc_refs[i][...] += partial(i)

    @pl.when(k == n_k - 1)
    def _():
        finish([a[...] for a in acc_refs])


def _matmul(xs, ws, extras, epilogue, outs, *, tm, tn, tk=None, col_major=False, name="matmul"):
    n_pairs = len(ws)
    assert len(xs) in (1, n_pairs)
    head_major = tuple(x.ndim == 3 for x in xs)
    m = xs[0].shape[1] if head_major[0] else xs[0].shape[0]
    kdim, n = ws[0].shape
    if tk is None:
        tk = kdim
    n_k = kdim // tk
    assert kdim % tk == 0 and m % tm == 0 and n % tn == 0
    n_j = n // tn
    tks = [tk if n_k > 1 else w.shape[0] for w in ws]
    assert all(w.shape == (kdim, n) for w in ws) or n_k == 1

    def order(f):
        return (lambda j, i, k: f(i, j, k)) if col_major else f

    in_specs = []
    for x, hm, tki in zip(xs, head_major, tks):
        if hm:
            assert n_k == 1
            in_specs.append(pl.BlockSpec((x.shape[0], tm, LANES), order(lambda i, j, k: (0, i, 0))))
        else:
            in_specs.append(pl.BlockSpec((tm, tki), order(lambda i, j, k: (i, k))))
    for w, tki in zip(ws, tks):
        in_specs.append(pl.BlockSpec((tki, tn), order(lambda i, j, k: (k, j))))
    for arr, cols, imap in extras:
        in_specs.append(pl.BlockSpec((tm, cols), order(lambda i, j, k, imap=imap: imap(i, j))))

    out_specs, out_shapes, out_head_w = [], [], []
    for n_cols, dtype, hw in outs:
        to = n_cols // n_j
        out_head_w.append(hw)
        if hw:
            assert to % hw == 0
            out_specs.append(pl.BlockSpec((to // hw, tm, hw), order(lambda i, j, k: (j, i, 0))))
            out_shapes.append(jax.ShapeDtypeStruct((n_cols // hw, m, hw), dtype))
        else:
            out_specs.append(pl.BlockSpec((tm, to), order(lambda i, j, k: (i, j))))
            out_shapes.append(jax.ShapeDtypeStruct((m, n_cols), dtype))

    scratch = [pltpu.VMEM((tm, tn), _F32) for _ in range(n_pairs)] if n_k > 1 else []
    kern = functools.partial(
        _matmul_kernel, n_pairs=n_pairs, n_extras=len(extras), n_outs=len(outs), n_k=n_k,
        head_major=head_major, out_head_w=tuple(out_head_w), epilogue=epilogue)
    return pl.pallas_call(
        kern,
        grid=(n_j, m // tm, n_k) if col_major else (m // tm, n_j, n_k),
        in_specs=in_specs,
        out_specs=out_specs,
        out_shape=out_shapes,
        scratch_shapes=scratch,
        compiler_params=_compiler_params(("parallel", "parallel", "arbitrary")),
        name=name,
    )(*xs, *ws, *[e[0] for e in extras])


def _matmul_resid_kernel(x_ref, w_ref, r_ref, o_ref, acc_ref, *, n_k):
    k = pl.program_id(1)
    j = pl.program_id(2)
    part = jnp.dot(x_ref[...], w_ref[...], preferred_element_type=_F32)
    if n_k == 1:
        o_ref[...] = r_ref[...] + part
        return

    @pl.when(k == 0)
    def _():
        acc_ref[j] = part

    @pl.when((k > 0) & (k < n_k - 1))
    def _():
        acc_ref[j] += part

    @pl.when(k == n_k - 1)
    def _():
        o_ref[...] = r_ref[...] + (acc_ref[j] + part)


def _matmul_resid(x, w, resid, *, tm, tn, tk, name):
    m, kdim = x.shape
    n = w.shape[1]
    assert m % tm == 0 and n % tn == 0 and kdim % tk == 0
    n_j, n_k = n // tn, kdim // tk
    last = lambda i, k, j: (i, jnp.where(k == n_k - 1, j, 0))
    return pl.pallas_call(
        functools.partial(_matmul_resid_kernel, n_k=n_k),
        grid=(m // tm, n_k, n_j),
        in_specs=[pl.BlockSpec((tm, tk), lambda i, k, j: (i, k)),
                  pl.BlockSpec((tk, tn), lambda i, k, j: (k, j)),
                  pl.BlockSpec((tm, tn), last)],
        out_specs=pl.BlockSpec((tm, tn), last),
        out_shape=jax.ShapeDtypeStruct((m, n), _F32),
        scratch_shapes=[pltpu.VMEM((n_j, tm, tn), _F32)],
        compiler_params=_compiler_params(("parallel", "arbitrary", "arbitrary")),
        name=name,
    )(x, w, resid)


def _shift_rows(z, prev_row):
    rolled = pltpu.roll(z, 1, 0)
    row = lax.broadcasted_iota(jnp.int32, z.shape, 0)
    return jnp.where(row == 0, prev_row, rolled)


def _cumsum_rows(x):
    n = x.shape[0]
    row = lax.broadcasted_iota(jnp.int32, x.shape, 0)
    s = 1
    while s < n:
        x = x + jnp.where(row >= s, pltpu.roll(x, s, 0), 0.0)
        s *= 2
    return x


def bf(x):
    return x.astype(_BF16)


def _bdot(a, b):
    assert a.dtype == _BF16 and b.dtype == _BF16
    return jnp.dot(a, b, preferred_element_type=_F32)


def _bdot_nt(a, b):
    assert a.dtype == _BF16 and b.dtype == _BF16
    return lax.dot_general(a, b, (((1,), (1,)), ((), ())), preferred_element_type=_F32)


def _wkv_kernel(*refs, n_chunks, head, n_pp, aliased):
    (zr_ref, zk_ref, zv_ref, zl_ref, sprev_ref, lprev_ref, s0_ref,
     mu_ref, mul_ref, w0_ref, a0_ref, kkp_ref, kap_ref, rk_ref, lg_ref, lb_ref,
     w2_ref, a2_ref, g2_ref) = refs[:19]
    y_ref, sout_ref, st_ref, prev_ref, lprev_scr = refs[19 + aliased:]
    c_idx = pl.program_id(2)
    C = WKV_CHUNK
    P = 2 * head
    assert P == 2 * C
    tb = zr_ref.shape[0]
    n_sub = tb // C

    @pl.when(c_idx == 0)
    def _():
        st_ref[...] = s0_ref[0]
        prev_ref[...] = sprev_ref[0, 0]
        lprev_scr[...] = lprev_ref[0]

    lane = lax.broadcasted_iota(jnp.int32, (1, P), 1)
    m0 = (lane < head).astype(_F32)
    m1 = 1.0 - m0
    r2 = lax.broadcasted_iota(jnp.int32, (P, P), 0)
    c2 = lax.broadcasted_iota(jnp.int32, (P, P), 1)
    same = (r2 // C) == (c2 // C)
    tril_s = jnp.where(same & (c2 < r2), 1.0, 0.0)
    tril_i = jnp.where(same & (c2 <= r2), 1.0, 0.0)
    eye = jnp.where(r2 == c2, 1.0, 0.0)

    def stack(x):
        return jnp.concatenate([x * m0, x * m1], axis=0)

    def head_sum(x):
        s0 = jnp.sum(x * m0, axis=-1, keepdims=True)
        s1 = jnp.sum(x * m1, axis=-1, keepdims=True)
        return s0 * m0 + s1 * m1

    zl = zl_ref[...]
    xl = zl + (_shift_rows(zl, lprev_scr[...]) - zl) * mul_ref[...]
    lprev_scr[...] = zl[tb - 1:tb, :]
    nw = w2_ref.shape[0]
    na = a2_ref.shape[0]
    tw = jnp.tanh(xl[:, 0:nw]).astype(_BF16)
    xa = xl[:, nw:nw + na].astype(_BF16)
    sg = _sigmoid(xl[:, nw + na:]).astype(_BF16)

    pairs = []
    for pi in range(n_pp):
        ls = slice(pi * P, (pi + 1) * P)
        zr, zk, zv = zr_ref[:, ls], zk_ref[:, ls], zv_ref[:, ls]
        r = zr + (_shift_rows(zr, prev_ref[0:1, ls]) - zr) * mu_ref[0:1, ls]
        k = zk + (_shift_rows(zk, prev_ref[1:2, ls]) - zk) * mu_ref[1:2, ls]
        v = zv + (_shift_rows(zv, prev_ref[2:3, ls]) - zv) * mu_ref[2:3, ls]
        prev_ref[0:1, ls] = zr[tb - 1:tb, :]
        prev_ref[1:2, ls] = zk[tb - 1:tb, :]
        prev_ref[2:3, ls] = zv[tb - 1:tb, :]
        u = -(w0_ref[:, ls] + jnp.dot(tw, w2_ref[:, ls], preferred_element_type=_F32))
        softplus = jnp.maximum(u, 0.0) + jnp.log(1.0 + jnp.exp(-jnp.abs(u)))
        logw = -jnp.exp(-softplus - 0.5)
        a = _sigmoid(a0_ref[:, ls] + jnp.dot(xa, a2_ref[:, ls], preferred_element_type=_F32))
        g = jnp.dot(sg, g2_ref[:, ls], preferred_element_type=_F32)
        kk = k * kkp_ref[:, ls]
        kk = kk / jnp.maximum(jnp.sqrt(head_sum(kk * kk)), 1e-12)
        kh = k * (1.0 + (a - 1.0) * kap_ref[:, ls])
        pairs.append(dict(ls=ls, r=r, v=v, logw=logw, kk=kk, kh=kh, beta=kk * a, g=g,
                          bonus=head_sum(r * kh * rk_ref[:, ls]) * v))

    states = [st_ref[pi] for pi in range(n_pp)]
    group = n_sub
    for sc0 in range(0, n_sub, group):
        items = []
        for sc in range(sc0, sc0 + group):
            rs = slice(sc * C, (sc + 1) * C)
            for pi, pr in enumerate(pairs):
                lw = pr["logw"][rs]
                cum = _cumsum_rows(lw)
                cum_c = cum[C - 1:C, :]
                e_neg = jnp.exp(-cum)
                e_end = jnp.exp(cum_c - cum)
                r2s = stack(pr["r"][rs] * jnp.exp(cum))
                items.append(dict(
                    pi=pi, rs=rs, r2s=r2s,
                    ar=bf(jnp.concatenate([stack(-pr["kk"][rs] * jnp.exp(cum - lw)), r2s], axis=0)),
                    bk=bf(jnp.concatenate([stack(pr["beta"][rs] * e_neg), stack(pr["kh"][rs] * e_neg)],
                                          axis=0)),
                    v2s=bf(stack(pr["v"][rs])),
                    bh2t=bf(stack(pr["beta"][rs] * e_end).T),
                    kh2t=bf(stack(pr["kh"][rs] * e_end).T),
                    dec_col=jnp.sum(eye * jnp.exp(cum_c), axis=-1, keepdims=True)))
        for it in items:
            nn = _bdot_nt(it["ar"], it["bk"])
            n_ba = nn[:P, :P] * tril_s
            it["n_ka"] = bf(nn[:P, P:] * tril_s)
            it["p_br"] = bf(nn[P:, :P] * tril_i)
            it["p_kr"] = bf(nn[P:, P:] * tril_i)
            it["t_inv"] = eye + n_ba
            it["pw"] = bf(n_ba)
        for it in items:
            it["pw"] = bf(_bdot(it["pw"], it["pw"]))
        s = 2
        while s < C:
            for it in items:
                sq = _bdot(it["pw"], jnp.concatenate([it["pw"], bf(it["t_inv"])], axis=1))
                it["pw"] = bf(sq[:, :P])
                it["t_inv"] = it["t_inv"] + sq[:, P:]
            s *= 2
        for it in items:
            it["nkv"] = bf(_bdot(it["n_ka"], it["v2s"]))
        for it in items:
            it["tt"] = bf(_bdot(bf(it["t_inv"]), jnp.concatenate([it["ar"][:P], it["nkv"]], axis=1)))
        for it in items:
            it["mg"] = _bdot(it["bh2t"], it["tt"])
        for it in items:
            it["pg"] = _bdot(it["p_br"], it["tt"])
        for it in items:
            it["g_c"] = it["mg"][:, P:] + _bdot(it["kh2t"], it["v2s"])
        for it in items:
            it["yg"] = it["pg"][:, P:] + _bdot(it["p_kr"], it["v2s"])
        for it in items:
            pi, rs, pr = it["pi"], it["rs"], pairs[it["pi"]]
            ls = pr["ls"]
            st = states[pi]
            st_b = bf(st)
            y2 = _bdot(bf(it["r2s"] + it["pg"][:, :P]), st_b) + it["yg"]
            states[pi] = it["dec_col"] * st + _bdot(bf(it["mg"][:, :P]), st_b) + it["g_c"]
            y = y2[0:C, :] + y2[C:2 * C, :]
            mean = head_sum(y) * (1.0 / head)
            yc = y - mean
            var = head_sum(yc * yc) * (1.0 / head)
            yn = yc * lax.rsqrt(var + GN_EPS) * lg_ref[:, ls] + lb_ref[:, ls]
            y_ref[rs, ls] = ((yn + pr["bonus"][rs]) * pr["g"][rs]).astype(y_ref.dtype)
    for pi in range(n_pp):
        st_ref[pi] = states[pi]

    @pl.when(c_idx == n_chunks - 1)
    def _():
        sout_ref[0] = st_ref[...]


def _wkv(z, shift_prev, s0_bd, params, *, batch, seq, row0, head, d_a, lora_sizes, m_out, out_buf):
    pw = 2 * head
    n_pairs = d_a // pw
    n_pp = min(WKV_PAIRS_PER_STEP, n_pairs)
    gw = n_pp * pw
    n_groups = n_pairs // n_pp
    lp = sum(lora_sizes)
    assert (3 * d_a) % lp == 0
    tb = _pick_tile(seq, WKV_ROWS_PER_STEP, WKV_CHUNK)
    n_chunks = seq // tb
    assert row0 % tb == 0
    rb0 = row0 // tb
    rpb = seq // tb
    aliased = out_buf is not None

    def rows(b, c):
        return rb0 + b * rpb + c

    in_specs = [
        pl.BlockSpec((tb, gw), lambda b, p, c: (rows(b, c), p)),
        pl.BlockSpec((tb, gw), lambda b, p, c: (rows(b, c), n_groups + p)),
        pl.BlockSpec((tb, gw), lambda b, p, c: (rows(b, c), 2 * n_groups + p)),
        pl.BlockSpec((tb, lp), lambda b, p, c: (rows(b, c), 3 * d_a // lp)),
        pl.BlockSpec((1, 1, SUBLANES, gw), lambda b, p, c: (b, p, 0, 0)),
        pl.BlockSpec((1, 1, lp), lambda b, p, c: (b, 0, 0)),
        pl.BlockSpec((1, n_pp, pw, pw), lambda b, p, c: (b, p, 0, 0)),
        pl.BlockSpec((SUBLANES, gw), lambda b, p, c: (p, 0)),
        pl.BlockSpec((1, lp), lambda b, p, c: (0, 0)),
    ]
    for _ in range(7):
        in_specs.append(pl.BlockSpec((1, gw), lambda b, p, c: (0, p)))
    nw, na, ng = lora_sizes
    in_specs += [pl.BlockSpec((nw, gw), lambda b, p, c: (0, p)),
                 pl.BlockSpec((na, gw), lambda b, p, c: (0, p)),
                 pl.BlockSpec((ng, gw), lambda b, p, c: (0, p))]
    args = [z, z, z, z, shift_prev["rkv"], shift_prev["lora"], s0_bd,
            params["mu_rkv"], params["mu_lora"], params["w0"], params["a0"], params["k_k"],
            params["k_a"], params["r_k"], params["lnx_g"], params["lnx_b"],
            params["w2"], params["a2"], params["g2"]]
    aliases = {}
    if aliased:
        in_specs.append(pl.BlockSpec(memory_space=pl.ANY))
        aliases = {len(args): 0}
        args.append(out_buf)
    out_specs = [pl.BlockSpec((tb, gw), lambda b, p, c: (rows(b, c), p)),
                 pl.BlockSpec((1, n_pp, pw, pw), lambda b, p, c: (b, p, 0, 0))]
    out_shape = [jax.ShapeDtypeStruct((m_out, d_a), _BF16),
                 jax.ShapeDtypeStruct((batch, n_pairs, pw, pw), _F32)]
    kern = functools.partial(_wkv_kernel, n_chunks=n_chunks, head=head, n_pp=n_pp, aliased=int(aliased))
    return pl.pallas_call(
        kern,
        grid=(batch, n_groups, n_chunks),
        in_specs=in_specs,
        out_specs=out_specs,
        out_shape=out_shape,
        scratch_shapes=[pltpu.VMEM((n_pp, pw, pw), _F32),
                        pltpu.VMEM((SUBLANES, gw), _F32),
                        pltpu.VMEM((1, lp), _F32)],
        input_output_aliases=aliases,
        compiler_params=_compiler_params(("parallel", "parallel", "arbitrary")),
        name="wkv7_chunked",
    )(*args)


def _dsa_kernel(*refs, tq, l_ext, l_true, q_pos0, topk, idx_scale, n_kv, dh, d_idx, aliased):
    q_ref, k_ref, v_ref, qi_ref, wi_ref, ki_ref = refs[:6]
    o_ref, kibf, wib, isc_scr, key_scr, dm_scr = refs[6 + aliased:]
    qb = pl.program_id(1)
    n_q = q_ref.shape[0]
    n_i = qi_ref.shape[0]
    rep = n_q // n_kv

    @pl.when(qb == 0)
    def _():
        kibf[...] = ki_ref[0:l_ext, 0:d_idx].astype(_BF16)

    wi = wi_ref[...]
    for hi in range(n_i):
        wib[hi] = jnp.broadcast_to(wi[:, hi:hi + 1], (tq, LANES))
    isc_scr[...] = jnp.zeros((tq, l_ext), _F32)
    ki = kibf[...]

    hs = max(1, min(4, 512 // tq))
    assert n_i % hs == 0

    def idx_body(i, carry):
        qs = qi_ref[pl.ds(i * hs, hs)].reshape(hs * tq, d_idx)
        sc = lax.dot_general(qs, ki, (((1,), (1,)), ((), ())), preferred_element_type=_F32)
        acc = isc_scr[...]
        for j in range(hs):
            acc = acc + (jnp.maximum(sc[j * tq:(j + 1) * tq], 0.0)
                         * jnp.tile(wib[i * hs + j], (1, l_ext // LANES)))
        isc_scr[...] = acc
        return carry

    lax.fori_loop(0, n_i // hs, idx_body, 0)

    q_pos = q_pos0 + qb * tq + lax.broadcasted_iota(jnp.int32, (tq, l_ext), 0)
    k_pos = lax.broadcasted_iota(jnp.int32, (tq, l_ext), 1)
    cs = CHUNK.bit_length() - 1
    assert 1 << cs == CHUNK
    allowed = ((k_pos >> cs) <= (q_pos >> cs)) & (k_pos < l_true)
    bits = pltpu.bitcast(isc_scr[...] * idx_scale, jnp.int32)
    int_min = jnp.int32(-2 ** 31)
    key_scr[...] = jnp.where(allowed, jnp.where(bits < 0, bits ^ jnp.int32(0x7FFFFFFF), bits), int_min)

    def count_ge(cand):
        return jnp.sum(jnp.where(key_scr[...] >= cand, 1.0, 0.0), axis=-1, keepdims=True)

    kf = jnp.float32(topk)
    zero = jnp.zeros((tq, 1), jnp.int32)
    thr = jnp.where(count_ge(zero) >= kf, zero, jnp.full((tq, 1), int_min, jnp.int32))

    def thr_body(i, thr):
        cand = thr + (jnp.int32(1) << (jnp.int32(30) - i))
        return jnp.where(count_ge(cand) >= kf, cand, thr)

    thr = lax.fori_loop(0, 31, thr_body, thr)
    key = key_scr[...]
    n_gt = jnp.sum(jnp.where(key > thr, 1.0, 0.0), axis=-1, keepdims=True)
    n_eq = jnp.sum(jnp.where(key == thr, 1.0, 0.0), axis=-1, keepdims=True)
    need = kf - n_gt
    n_bits = max(1, (l_ext - 1).bit_length())
    full = jnp.full((tq, 1), 1 << n_bits, jnp.int32)

    def tie_bound():
        def count_tie_below(bound):
            hit = jnp.where(key_scr[...] == thr, jnp.where(k_pos < bound, 1.0, 0.0), 0.0)
            return jnp.sum(hit, axis=-1, keepdims=True)

        def tbody(i, bound):
            cand = bound + (jnp.int32(1) << (jnp.int32(n_bits - 1) - i))
            return jnp.where(count_tie_below(cand) <= need, cand, bound)

        return lax.fori_loop(0, n_bits, tbody, zero)

    has_excess_ties = jnp.max(n_eq - need) > 0.0
    bound = lax.cond(has_excess_ties, tie_bound, lambda: full)
    sel = allowed & ((key > thr) | ((key == thr) & (k_pos < bound)))
    dm_scr[...] = jnp.where(sel, jnp.abs(q_pos - k_pos).astype(_F32), DIST_EXCLUDED)

    def qk(h, kg):
        return lax.dot_general(q_ref[h], kg, (((1,), (1,)), ((), ())), preferred_element_type=_F32)

    def head_body(g, carry):
        kg, vg = k_ref[g, pl.ds(0, l_ext), :], v_ref[g, pl.ds(0, l_ext), :]
        dm = dm_scr[...]
        s_next = qk(g * rep, kg)
        for j in range(rep):
            h = g * rep + j
            s_cur = s_next
            if j + 1 < rep:
                s_next = qk(h + 1, kg)
            slope2 = jnp.exp2(jnp.full((1, 1), h + 1, jnp.int32).astype(_F32) * (-8.0 / n_q)) * LOG2_E
            s = s_cur - slope2 * dm
            m = jnp.max(s, axis=-1, keepdims=True)
            p = jnp.exp2(s - m)
            l = jnp.sum(p, axis=-1, keepdims=True)
            o = jnp.dot(p.astype(_BF16), vg, preferred_element_type=_F32)
            o_ref[h] = (o / l).astype(o_ref.dtype)
        return carry

    lax.fori_loop(0, n_kv, head_body, 0)


def _dsa(q_hm, qi_hm, k_hm, v_hm, ki_src, ki_col, wi_src, wi_col, *, batch, n_qb, tq,
         q_row0, q_stride, kv_rows, kv_stride, l_ext, l_true, q_pos0, topk, n_kv, m_out, out_buf):
    n_q, _, dh = q_hm.shape
    n_i, _, d_idx = qi_hm.shape
    assert q_row0 % tq == 0 and q_stride % tq == 0 and l_ext % LANES == 0 and l_ext <= kv_rows
    qr0, qst = q_row0 // tq, q_stride // tq
    aliased = out_buf is not None
    kern = functools.partial(
        _dsa_kernel, tq=tq, l_ext=l_ext, l_true=l_true, q_pos0=q_pos0, topk=topk,
        idx_scale=float(n_i) ** -0.5 * float(d_idx) ** -0.5, n_kv=n_kv, dh=dh, d_idx=d_idx,
        aliased=int(aliased))
    qrow = lambda b, i: qr0 + b * qst + i
    in_specs = [
        pl.BlockSpec((n_q, tq, dh), lambda b, i: (0, qrow(b, i), 0)),
        pl.BlockSpec((n_kv, kv_rows, dh), lambda b, i: (0, b * kv_stride, 0)),
        pl.BlockSpec((n_kv, kv_rows, dh), lambda b, i: (0, b * kv_stride, 0)),
        pl.BlockSpec((n_i, tq, d_idx), lambda b, i: (0, qrow(b, i), 0)),
        pl.BlockSpec((tq, LANES), lambda b, i: (qrow(b, i), wi_col)),
        pl.BlockSpec((kv_rows, LANES), lambda b, i: (b * kv_stride, ki_col)),
    ]
    args = [q_hm, k_hm, v_hm, qi_hm, wi_src, ki_src]
    aliases = {}
    if aliased:
        in_specs.append(pl.BlockSpec(memory_space=pl.ANY))
        aliases = {len(args): 0}
        args.append(out_buf)
    return pl.pallas_call(
        kern,
        grid=(batch, n_qb),
        in_specs=in_specs,
        out_specs=pl.BlockSpec((n_q, tq, dh), lambda b, i: (0, qrow(b, i), 0)),
        out_shape=jax.ShapeDtypeStruct((n_q, m_out, dh), _BF16),
        scratch_shapes=[pltpu.VMEM((l_ext, d_idx), _BF16), pltpu.VMEM((n_i, tq, LANES), _F32),
                        pltpu.VMEM((tq, l_ext), _F32), pltpu.VMEM((tq, l_ext), jnp.int32),
                        pltpu.VMEM((tq, l_ext), _F32)],
        input_output_aliases=aliases,
        compiler_params=_compiler_params(("parallel", "arbitrary")),
        name="dsa_attention",
    )(*args)


def _pad_rows(w, rows):
    return jnp.pad(w, ((0, rows - w.shape[0]), (0, 0)))


def _pad_cols(w, cols):
    return jnp.pad(w, ((0, 0), (0, cols - w.shape[1])))


def _to_block_diag_t(s, head):
    b, h = s.shape[:2]
    st = jnp.swapaxes(s, -1, -2).reshape(b, h // 2, 2, head, head).astype(_F32)
    z = jnp.zeros_like(st[:, :, 0])
    top = jnp.concatenate([st[:, :, 0], z], axis=-1)
    bot = jnp.concatenate([z, st[:, :, 1]], axis=-1)
    return jnp.concatenate([top, bot], axis=-2)


def _from_block_diag_t(sbd, head):
    b, hp = sbd.shape[:2]
    s0 = sbd[:, :, :head, :head]
    s1 = sbd[:, :, head:, head:]
    st = jnp.stack([s0, s1], axis=2).reshape(b, hp * 2, head, head)
    return jnp.swapaxes(st, -1, -2)


def kernel(x_prompt, x_sample, cache_k, cache_v, cache_kidx, state_shift, state_wkv, norm1_g, w_in, mu_shift, w0, w2, a0, a2, g2, k_k, k_a, r_k, lnx_g, lnx_b, w_br_a, w_br_b, w_out, norm2_g, w_ffn_gate, w_ffn_up, w_ffn_down, norm_f_g):
    depth = w_in.shape[0]
    assert depth == 1
    bp, tp, d_model = x_prompt.shape
    bs, ts, _ = x_sample.shape
    _, _, past, n_kv, dh = cache_k.shape
    d_idx = cache_kidx.shape[-1]
    d_shift = state_shift.shape[-1]
    _, _, h_a, head, _ = state_wkv.shape
    d_a = h_a * head
    r_w, r_a, r_g = w2.shape[1], a2.shape[1], g2.shape[1]
    n_q = w_br_b.shape[1] // dh
    d_in = w_in.shape[-1]
    h_i = (d_in - d_shift - n_q * dh - 2 * n_kv * dh - d_idx - 2 * d_model) // (d_idx + 1)
    d_ff = w_ffn_gate.shape[-1]
    kvw = n_kv * dh
    l = 0
    mp, ms = bp * tp, bs * ts
    m_tot = mp + ms
    xp2, xs2 = x_prompt.reshape(mp, d_model), x_sample.reshape(ms, d_model)
    tm_mm = _pick_tile(m_tot, ROW_TILE_MM, 2 * SUBLANES)

    win = w_in[l]
    o = 0
    def take(n):
        nonlocal o
        blk = win[:, o:o + n]
        o += n
        return blk
    nwp, nap, ngp = _round_up(r_w, LANES), _round_up(r_a, LANES), _round_up(r_g, LANES)
    lp = nwp + nap + ngp
    w_rkv, w_lw, w_la, w_lg = take(3 * d_a), take(r_w), take(r_a), take(r_g)
    w_rkvl = jnp.concatenate([w_rkv, _pad_cols(w_lw, nwp), _pad_cols(w_la, nap), _pad_cols(w_lg, ngp)],
                             axis=1).astype(_BF16)
    w_q = take(n_q * dh).astype(_BF16)
    w_k, w_v = take(kvw), take(kvw)
    w_qi = take(h_i * d_idx).astype(_BF16)
    w_ki, w_wi = take(d_idx), take(h_i)
    w_kv = jnp.concatenate([w_k, w_v, _pad_cols(w_ki, LANES), _pad_cols(w_wi, LANES)], axis=1).astype(_BF16)
    w_gates = jnp.concatenate([take(d_model), take(d_model)], axis=1).astype(_BF16)

    h1 = _rmsnorm_cat(xp2, xs2, norm1_g[l], _BF16)
    ident = lambda accs, ex: [accs[0]]
    n_rkvl = 3 * d_a + lp
    (z_a,) = _matmul([h1], [w_rkvl], [], ident, [(n_rkvl, _F32, None)],
                     tm=tm_mm, tn=_pick_tile(n_rkvl, COL_TILE_MM, LANES), name="proj_rkv_lora")
    n_kvz = 2 * kvw + 2 * LANES
    assert n_kvz <= COL_TILE_KV
    z_kv, k_hm, v_hm = _matmul(
        [h1], [w_kv], [], lambda accs, ex: [accs[0], accs[0][:, :kvw], accs[0][:, kvw:2 * kvw]],
        [(n_kvz, _F32, None), (kvw, _BF16, dh), (kvw, _BF16, dh)],
        tm=tm_mm, tn=n_kvz, name="proj_kv_idx")
    q_scale = float(dh) ** -0.5 * LOG2_E
    (q_hm,) = _matmul([h1], [w_q], [], lambda accs, ex: [accs[0] * q_scale],
                      [(n_q * dh, _BF16, dh)], tm=tm_mm, tn=_pick_tile(n_q * dh, COL_TILE_MM, dh), name="proj_q")
    (qi_hm,) = _matmul([h1], [w_qi], [], ident, [(h_i * d_idx, _BF16, d_idx)],
                       tm=tm_mm, tn=_pick_tile(h_i * d_idx, COL_TILE_MM, LANES), name="proj_qi")
    (gates,) = _matmul([h1], [w_gates], [], ident, [(2 * d_model, _BF16, None)],
                       tm=tm_mm, tn=_pick_tile(2 * d_model, COL_TILE_MM, LANES), name="proj_gates")
    kx, vx = z_kv[:, :kvw], z_kv[:, kvw:2 * kvw]
    ki_new = z_kv[:, 2 * kvw:2 * kvw + d_idx]

    mu = mu_shift[l]
    def lora_row(vec):
        return jnp.concatenate([
            jnp.pad(vec[3 * d_a:3 * d_a + r_w], (0, nwp - r_w)),
            jnp.pad(vec[3 * d_a + r_w:3 * d_a + r_w + r_a], (0, nap - r_a)),
            jnp.pad(vec[3 * d_a + r_w + r_a:], (0, ngp - r_g))])
    n_pairs = d_a // (2 * head)
    gw = min(WKV_PAIRS_PER_STEP, n_pairs) * 2 * head
    n_groups = d_a // gw
    def rkv_rows(vec):
        lead = vec.shape[:-1]
        x = vec[..., :3 * d_a].reshape(lead + (3, n_groups, gw))
        x = jnp.moveaxis(x, -3, -2)
        return jnp.pad(x, [(0, 0)] * (len(lead) + 1) + [(0, SUBLANES - 3), (0, 0)])
    row = lambda v: v[l].reshape(1, d_a).astype(_F32)
    params = dict(
        mu_rkv=rkv_rows(mu).reshape(n_groups * SUBLANES, gw),
        mu_lora=lora_row(mu).reshape(1, lp),
        w0=row(w0), a0=row(a0), k_k=row(k_k), k_a=row(k_a),
        r_k=r_k[l].reshape(1, d_a).astype(_F32), lnx_g=row(lnx_g), lnx_b=row(lnx_b),
        w2=_pad_rows(w2[l], nwp).astype(_BF16), a2=_pad_rows(a2[l], nap).astype(_BF16),
        g2=_pad_rows(g2[l], ngp).astype(_BF16))

    def shift_state(s):
        s = s[:, 0]
        return dict(rkv=rkv_rows(s), lora=jax.vmap(lora_row)(s).reshape(-1, 1, lp))

    wkv_kw = dict(head=head, d_a=d_a, lora_sizes=(nwp, nap, ngp), m_out=m_tot)
    y_a, sp_bd = _wkv(z_a, shift_state(jnp.zeros((bp, 1, d_shift), _F32)),
                      jnp.zeros((bp, n_pairs, 2 * head, 2 * head), _F32), params,
                      batch=bp, seq=tp, row0=0, out_buf=jnp.zeros((m_tot, d_a), _BF16), **wkv_kw)
    y_a, ss_bd = _wkv(z_a, shift_state(state_shift[l]), _to_block_diag_t(state_wkv[l], head), params,
                      batch=bs, seq=ts, row0=mp, out_buf=y_a, **wkv_kw)
    wkv_p = _from_block_diag_t(sp_bd, head)
    wkv_s = _from_block_diag_t(ss_bd, head)

    topk_p = min(TOPK_MAX, tp // 4)
    tq_p = _pick_tile(tp, DSA_QUERY_TILE, CHUNK)
    n_grp = max(1, min(DSA_GROUPS, tp // tq_p))
    grp = tp // n_grp
    wi_col = (2 * kvw + LANES) // LANES
    ki_col = 2 * kvw // LANES
    y_b = jnp.zeros((n_q, m_tot, dh), _BF16)
    for gi in range(n_grp):
        l_ext = (gi + 1) * grp
        kv_rows = l_ext
        while tp % kv_rows:
            kv_rows += grp
        y_b = _dsa(q_hm, qi_hm, k_hm, v_hm, z_kv, ki_col, z_kv, wi_col,
                   batch=bp, n_qb=grp // tq_p, tq=tq_p, q_row0=gi * grp, q_stride=tp,
                   kv_rows=kv_rows, kv_stride=tp // kv_rows, l_ext=l_ext, l_true=l_ext,
                   q_pos0=gi * grp, topk=topk_p, n_kv=n_kv, m_out=m_tot, out_buf=y_b)
    ls = past + ts
    lps = _round_up(ls, LANES)
    def with_cache(cache, new, width):
        x = jnp.concatenate([cache.reshape(bs, past, width), new.reshape(bs, ts, width)], axis=1)
        return jnp.pad(x, ((0, 0), (0, lps - ls), (0, 0))).reshape(bs * lps, width)
    def with_cache_hm(cache, new_hm):
        x = jnp.concatenate([jnp.moveaxis(cache, 2, 0).astype(_BF16),
                             new_hm[:, mp:].reshape(n_kv, bs, ts, dh)], axis=2)
        return jnp.pad(x, ((0, 0), (0, 0), (0, lps - ls), (0, 0))).reshape(n_kv, bs * lps, dh)
    ki_s = _pad_cols(with_cache(cache_kidx[l], ki_new[mp:], d_idx), LANES)
    y_b = _dsa(q_hm, qi_hm, with_cache_hm(cache_k[l], k_hm), with_cache_hm(cache_v[l], v_hm), ki_s, 0, z_kv, wi_col,
               batch=bs, n_qb=1, tq=ts, q_row0=mp, q_stride=ts, kv_rows=lps, kv_stride=1, l_ext=lps, l_true=ls,
               q_pos0=past, topk=min(TOPK_MAX, ls // 4), n_kv=n_kv, m_out=m_tot, out_buf=y_b)

    tn_mg = _pick_tile(d_model, COL_TILE_MM, LANES)
    nj_mg = d_model // tn_mg
    def merge_ep(accs, ex):
        return [_sigmoid(ex[0].astype(_F32)) * accs[0] + _sigmoid(ex[1].astype(_F32)) * accs[1]]
    (merged,) = _matmul([y_a, y_b], [w_br_a[l].astype(_BF16), w_br_b[l].astype(_BF16)],
                        [(gates, tn_mg, lambda i, j: (i, j)), (gates, tn_mg, lambda i, j: (i, nj_mg + j))],
                        merge_ep, [(d_model, _BF16, None)], tm=tm_mm, tn=tn_mg, name="branch_merge")
    tn_res = _pick_tile(d_model, COL_TILE_MM, LANES)
    tm_io = _pick_tile(math.gcd(mp, ms), ROW_TILE_NORM, 2 * SUBLANES)
    n_io_p = mp // tm_io
    def resid2_ep(accs, ex):
        return [jnp.where(pl.program_id(1) < n_io_p, ex[0], ex[1]) + accs[0]]
    (x1,) = _matmul([merged], [w_out[l].astype(_BF16)],
                    [(xp2, tn_res, lambda i, j: (jnp.minimum(i, n_io_p - 1), j)),
                     (xs2, tn_res, lambda i, j: (jnp.maximum(i - n_io_p, 0), j))],
                    resid2_ep, [(d_model, _F32, None)], tm=tm_io, tn=tn_res, col_major=True,
                    name="out_proj")

    h2 = _rmsnorm_rows(x1, norm2_g[l], _BF16, 0, m_tot)
    tn_ff = 2 * LANES
    d_ffp = _round_up(d_ff, tn_ff)
    ff_pad_cols = jnp.zeros((d_model, d_ffp - d_ff), _BF16)
    wg = jnp.concatenate([w_ffn_gate[l].astype(_BF16), ff_pad_cols], axis=1)
    wu = jnp.concatenate([w_ffn_up[l].astype(_BF16), ff_pad_cols], axis=1)
    def swiglu_ep(accs, ex):
        return [accs[0] * _sigmoid(accs[0]) * accs[1]]
    (u,) = _matmul([h2], [wg, wu], [], swiglu_ep, [(d_ffp, _BF16, None)],
                   tm=_pick_tile(m_tot, ROW_TILE_FFN_UP, 2 * SUBLANES), tn=tn_ff, name="ffn_up")
    wd = jnp.concatenate([w_ffn_down[l].astype(_BF16), jnp.zeros((d_ffp - d_ff, d_model), _BF16)], axis=0)
    tk_dn = d_ffp // 2 if (d_ffp // 2) % LANES == 0 else d_ffp
    x2 = _matmul_resid(u, wd, x1, tm=tm_mm, tn=_pick_tile(d_model, COL_TILE_FFN_DOWN, LANES), tk=tk_dn, name="ffn_down")
    y_p = _rmsnorm_rows(x2, norm_f_g, _F32, 0, mp).reshape(bp, tp, d_model)
    y_s = _rmsnorm_rows(x2, norm_f_g, _F32, mp, ms).reshape(bs, ts, d_model)

    kx4 = lambda x, b, t: x.reshape(1, b, t, n_kv, dh)
    def zsh_last(rows):
        zl = z_a[rows]
        return jnp.concatenate([zl[:, :3 * d_a + r_w], zl[:, 3 * d_a + nwp:3 * d_a + nwp + r_a],
                                zl[:, 3 * d_a + nwp + nap:3 * d_a + nwp + nap + r_g]], axis=-1)
    last_p = jnp.arange(bp) * tp + (tp - 1)
    last_s = mp + jnp.arange(bs) * ts + (ts - 1)
    return (y_p, y_s,
            kx4(kx[:mp], bp, tp), kx4(vx[:mp], bp, tp), ki_new[:mp].reshape(1, bp, tp, d_idx),
            zsh_last(last_p).reshape(1, bp, 1, d_shift), wkv_p[None],
            kx4(kx[mp:], bs, ts), kx4(vx[mp:], bs, ts), ki_new[mp:].reshape(1, bs, ts, d_idx),
            zsh_last(last_s).reshape(1, bs, 1, d_shift), wkv_s[None])
```
